```python
import jax
import jax.numpy as jnp
from jax import lax
import numpy as np

D_MODEL = 1024
BATCH = 8
SEQ = 4096
DEPTH = 2

CTX_LEN = 256
GRID_W = 64
HEAD_DIM = 64
MIX_DIM = D_MODEL
NORM_EPS = 1e-6
RWKV_HEADS = D_MODEL // 4 // HEAD_DIM
RWKV_DIM = RWKV_HEADS * HEAD_DIM
RWKV_W_LORA = 64
RWKV_A_LORA = 64
RWKV_G_LORA = 128
RWKV_COLS = 3 * RWKV_DIM + 2 * RWKV_W_LORA + 2 * RWKV_A_LORA + RWKV_G_LORA
RWKV_GN_EPS = 64e-5
MLA_HEADS = D_MODEL // 2 // HEAD_DIM
MLA_NOPE = 64
MLA_ROPE = 32
MLA_V = HEAD_DIM
MLA_QK = MLA_NOPE + MLA_ROPE
MLA_Q_RANK = D_MODEL // 2
MLA_KV_RANK = D_MODEL // 4
MLA_DIM = MLA_HEADS * MLA_V
MLA_COLS = MLA_Q_RANK + MLA_KV_RANK + MLA_ROPE
Q_BLOCK = 128
ROPE_BASE = 10000.0
MLSTM_HEADS = D_MODEL // 4 // HEAD_DIM
MLSTM_QK = HEAD_DIM // 2
MLSTM_V = HEAD_DIM
MLSTM_DIM = MLSTM_HEADS * MLSTM_V
MLSTM_CHUNK = 64
GATE_SOFTCAP = 15.0
MLSTM_COLS = 2 * MLSTM_HEADS * MLSTM_QK + 2 * MLSTM_DIM + 4 * MLSTM_HEADS
IN_COLS = RWKV_COLS + MLA_COLS + MLSTM_COLS
FFN_HIDDEN = -(-8 * D_MODEL // (3 * 256)) * 256

kernel_name = 'hybrid_rwkv7_mla_mlstm_dit_trunk'


def rmsnorm(x, g, eps=NORM_EPS):
    xf = x.astype(jnp.float32)
    y = xf * lax.rsqrt(jnp.mean(xf * xf, axis=-1, keepdims=True) + eps)
    return (y * g.astype(jnp.float32)).astype(x.dtype)


def modulate(h, shift, scale):
    return h * (1 + scale) + shift


def neighbours(z):
    zp = jnp.pad(z, ((0, 0), (1, 1), (0, 0)))
    return zp[:, :-2], zp[:, 2:]


def soft_cap(x):
    return GATE_SOFTCAP * jnp.tanh(x / GATE_SOFTCAP)


def swiglu(h, w_in, w_out):
    gate, up = jnp.split(h @ w_in, 2, axis=-1)
    return (jax.nn.silu(gate) * up) @ w_out


def axial_rope_tables(seq_len):
    rows = seq_len // GRID_W
    row = jnp.repeat(jnp.arange(rows, dtype=jnp.float32), GRID_W)
    col = jnp.tile(jnp.arange(GRID_W, dtype=jnp.float32), rows)
    n_freq = MLA_ROPE // 4
    inv = jnp.power(ROPE_BASE, -jnp.arange(n_freq, dtype=jnp.float32) / n_freq)
    ang = jnp.concatenate([row[:, None] * inv, col[:, None] * inv], axis=-1)
    return jnp.cos(ang), jnp.sin(ang)


def apply_rope(x, cos, sin):
    half = x.shape[-1] // 2
    xf = x.astype(jnp.float32)
    x1, x2 = xf[..., :half], xf[..., half:]
    c, s = cos[:, None, :], sin[:, None, :]
    return jnp.concatenate([x1 * c - x2 * s, x1 * s + x2 * c], axis=-1).astype(x.dtype)


def rwkv7_scan(r, w, k, v, kk, a, state, reverse):
    def step(S, xs):
        r_t, w_t, k_t, v_t, kk_t, a_t = xs
        s_kk = jnp.einsum('bhvk,bhk->bhv', S, kk_t)
        S = (S * w_t[:, :, None, :] - s_kk[..., None] * (kk_t * a_t)[:, :, None, :]
             + v_t[..., None] * k_t[:, :, None, :])
        return S, jnp.einsum('bhvk,bhk->bhv', S, r_t)
    xs = tuple(jnp.moveaxis(t, 1, 0) for t in (r, w, k, v, kk, a))
    state, y = lax.scan(step, state, xs, reverse=reverse)
    return jnp.moveaxis(y, 0, 1), state


def rwkv7_mixer(zl, zc, mu, w0, w2, a0, a2, g2, k_k, k_a, r_k, ln_g, ln_b, ctx_out):
    H, N = RWKV_HEADS, HEAD_DIM
    split_at = [RWKV_DIM, 2 * RWKV_DIM, 3 * RWKV_DIM,
                3 * RWKV_DIM + 2 * RWKV_W_LORA,
                3 * RWKV_DIM + 2 * RWKV_W_LORA + 2 * RWKV_A_LORA]

    def prep(z):
        B, T, _ = z.shape
        prev, nxt = neighbours(z)
        z = z + mu * (0.5 * (prev + nxt) - z)
        r, k, v, wd, ad, gd = jnp.split(z, split_at, axis=-1)
        heads = lambda t: t.reshape(B, T, H, N).astype(jnp.float32)
        kk = heads(k * k_k)
        kk = kk / jnp.maximum(jnp.sqrt(jnp.sum(kk * kk, axis=-1, keepdims=True)), 1e-12)
        per_dir = []
        for d in range(2):
            w_lo = jnp.tanh(wd[..., d * RWKV_W_LORA:(d + 1) * RWKV_W_LORA]) @ w2[d]
            log_w = -jax.nn.softplus(-(w0[d] + w_lo)) - 0.5
            decay = jnp.exp(-jnp.exp(log_w.astype(jnp.float32)))
            a = jax.nn.sigmoid(a0[d] + ad[..., d * RWKV_A_LORA:(d + 1) * RWKV_A_LORA] @ a2[d])
            k_d = k * (1 + (a - 1) * k_a)
            per_dir.append((heads(decay), heads(k_d), heads(a)))
        return heads(r), heads(k), heads(v), kk, per_dir, gd

    def post(y, r, k, v, gd):
        B, T = y.shape[:2]
        mean = jnp.mean(y, axis=-1, keepdims=True)
        var = jnp.mean(jnp.square(y - mean), axis=-1, keepdims=True)
        yn = ((y - mean) * lax.rsqrt(var + RWKV_GN_EPS)).reshape(B, T, H * N) * ln_g + ln_b
        bonus = jnp.sum(r * k * r_k, axis=-1, keepdims=True) * v
        g = jax.nn.sigmoid(gd) @ g2
        return ((yn + bonus.reshape(B, T, H * N)) * g).astype(zl.dtype)

    rl, kl, vl, kkl, dl, gdl = prep(zl)
    rc, kc, vc, kkc, dc, gdc = prep(zc)
    S0 = jnp.zeros((zl.shape[0], H, N, N), jnp.float32)
    y_lat, y_ctx = [], []
    for d, rev in ((0, False), (1, True)):
        yc_d, S_ctx = rwkv7_scan(rc, dc[d][0], dc[d][1], vc, kkc, dc[d][2], S0, rev)
        yl_d, _ = rwkv7_scan(rl, dl[d][0], dl[d][1], vl, kkl, dl[d][2], S_ctx, rev)
        y_lat.append(yl_d)
        y_ctx.append(yc_d)
    out_l = post(y_lat[0] + y_lat[1], rl, kl, vl, gdl)
    out_c = post(y_ctx[0] + y_ctx[1], rc, kc, vc, gdc) if ctx_out else None
    return out_l, out_c


def attend(q, keys, vals):
    s = jnp.einsum('bhqd,bhkd->bhqk', q, keys).astype(jnp.float32) * (MLA_QK ** -0.5)
    p = jax.nn.softmax(s, axis=-1)
    return jnp.einsum('bhqk,bhkd->bhqd', p.astype(vals.dtype), vals)


def mla_mixer(zl, zc, q_norm_g, w_uq, kv_norm_g, w_ukv, q_qk_g, k_qk_g, cos, sin, ctx_out):
    H = MLA_HEADS

    def project(z, rope):
        B, T, _ = z.shape
        cq, ckv, kr = jnp.split(z, [MLA_Q_RANK, MLA_Q_RANK + MLA_KV_RANK], axis=-1)
        q = (rmsnorm(cq, q_norm_g) @ w_uq).reshape(B, T, H, MLA_QK)
        kv = (rmsnorm(ckv, kv_norm_g) @ w_ukv).reshape(B, T, H, MLA_NOPE + MLA_V)
        k_nope, v = jnp.split(kv, [MLA_NOPE], axis=-1)
        k = jnp.concatenate([k_nope, jnp.broadcast_to(kr[:, :, None, :], (B, T, H, MLA_ROPE))], axis=-1)
        q = rmsnorm(q, q_qk_g)
        k = rmsnorm(k, k_qk_g)
        if rope:
            q = jnp.concatenate([q[..., :MLA_NOPE], apply_rope(q[..., MLA_NOPE:], cos, sin)], axis=-1)
            k = jnp.concatenate([k[..., :MLA_NOPE], apply_rope(k[..., MLA_NOPE:], cos, sin)], axis=-1)
        return q.transpose(0, 2, 1, 3), k.transpose(0, 2, 1, 3), v.transpose(0, 2, 1, 3)

    ql, kl, vl = project(zl, True)
    qc, kc, vc = project(zc, False)
    B, _, S, _ = ql.shape
    keys = jnp.concatenate([kl, kc], axis=2)
    vals = jnp.concatenate([vl, vc], axis=2)
    n_blk = S // Q_BLOCK
    qb = ql.reshape(B, H, n_blk, Q_BLOCK, MLA_QK).transpose(2, 0, 1, 3, 4)
    out = lax.map(lambda q_blk: attend(q_blk, keys, vals), qb)
    out_l = out.transpose(1, 0, 3, 2, 4).reshape(B, S, H * MLA_V)
    out_c = None
    if ctx_out:
        oc = attend(qc, kc, vc)
        out_c = oc.transpose(0, 2, 1, 3).reshape(B, zc.shape[1], H * MLA_V)
    return out_l, out_c


def mlstm_chunkwise(q, k, v, ig, fg, state):
    f32 = jnp.float32
    B, H, T, DK = q.shape
    DV = v.shape[-1]
    L = MLSTM_CHUNK
    NC = T // L
    q = q.astype(f32).reshape(B, H, NC, L, DK) * (DK ** -0.5)
    k = k.astype(f32).reshape(B, H, NC, L, DK)
    v = v.astype(f32).reshape(B, H, NC, L, DV)
    log_i = soft_cap(ig.astype(f32)).reshape(B, H, NC, L)
    log_f = jax.nn.log_sigmoid(soft_cap(fg.astype(f32))).reshape(B, H, NC, L)
    b = jnp.cumsum(log_f, axis=-1)
    g = b[..., -1]
    w_end = g[..., None] - b + log_i
    m_loc = jnp.max(w_end, axis=-1)
    e_end = jnp.exp(w_end - m_loc[..., None])
    C_loc = jnp.einsum('bhcl,bhclv,bhclk->bhcvk', e_end, v, k)
    n_loc = jnp.einsum('bhcl,bhclk->bhck', e_end, k)

    def step(carry, xs):
        C, n, m = carry
        g_c, m_l, C_l, n_l = xs
        m_new = jnp.maximum(g_c + m, m_l)
        s_old = jnp.exp(g_c + m - m_new)
        s_loc = jnp.exp(m_l - m_new)
        new = (s_old[..., None, None] * C + s_loc[..., None, None] * C_l,
               s_old[..., None] * n + s_loc[..., None] * n_l, m_new)
        return new, (C, n, m)

    xs = tuple(jnp.moveaxis(t, 2, 0) for t in (g, m_loc, C_loc, n_loc))
    final, starts = lax.scan(step, state, xs)
    C_s, n_s, m_s = (jnp.moveaxis(t, 0, 2) for t in starts)
    log_inter = b + m_s[..., None]
    causal = jnp.tril(jnp.ones((L, L), dtype=bool))
    log_d = jnp.where(causal, b[..., :, None] - b[..., None, :] + log_i[..., None, :], -jnp.inf)
    m_out = jnp.maximum(log_inter, jnp.max(log_d, axis=-1))
    w_intra = jnp.einsum('bhcsk,bhcjk->bhcsj', q, k) * jnp.exp(log_d - m_out[..., None])
    s_inter = jnp.exp(log_inter - m_out)
    num = (jnp.einsum('bhcsj,bhcjv->bhcsv', w_intra, v)
           + s_inter[..., None] * jnp.einsum('bhcvk,bhcsk->bhcsv', C_s, q))
    den = jnp.sum(w_intra, axis=-1) + s_inter * jnp.einsum('bhck,bhcsk->bhcs', n_s, q)
    h = num / jnp.maximum(jnp.abs(den), jnp.exp(-m_out))[..., None]
    return h.reshape(B, H, T, DV), final


def mlstm_mixer(zl, zc, conv_w, conv_b, i_b, f_b, norm_g, ctx_out):
    H = MLSTM_HEADS
    qk_w = 2 * H * MLSTM_QK

    def prep(z):
        B, T, _ = z.shape
        qk, v, o, gates = jnp.split(z, [qk_w, qk_w + MLSTM_DIM, qk_w + 2 * MLSTM_DIM], axis=-1)
        prev, nxt = neighbours(qk)
        qk = jax.nn.silu(prev * conv_w[0] + qk * conv_w[1] + nxt * conv_w[2] + conv_b)
        q, k = jnp.split(qk, 2, axis=-1)
        to_heads = lambda t, d: t.reshape(B, T, H, d).transpose(0, 2, 1, 3)
        gates = gates.reshape(B, T, 2, 2, H).transpose(2, 3, 0, 4, 1)
        ig = gates[:, 0] + i_b[:, None, :, None]
        fg = gates[:, 1] + f_b[:, None, :, None]
        return to_heads(q, MLSTM_QK), to_heads(k, MLSTM_QK), to_heads(v, MLSTM_V), ig, fg, o

    ql, kl, vl, igl, fgl, ol = prep(zl)
    qc, kc, vc, igc, fgc, oc = prep(zc)
    B = zl.shape[0]
    state0 = (jnp.zeros((B, H, MLSTM_V, MLSTM_QK), jnp.float32),
              jnp.zeros((B, H, MLSTM_QK), jnp.float32),
              jnp.zeros((B, H), jnp.float32))
    h_lat, h_ctx = [], []
    for d in range(2):
        flip = (lambda t: jnp.flip(t, axis=2)) if d == 1 else (lambda t: t)
        hc_d, st = mlstm_chunkwise(flip(qc), flip(kc), flip(vc), flip(igc[d]), flip(fgc[d]), state0)
        hl_d, _ = mlstm_chunkwise(flip(ql), flip(kl), flip(vl), flip(igl[d]), flip(fgl[d]), st)
        h_lat.append(flip(hl_d))
        h_ctx.append(flip(hc_d))

    def post(h, o):
        h = h.transpose(0, 2, 1, 3)
        hn = h * lax.rsqrt(jnp.mean(h * h, axis=-1, keepdims=True) + NORM_EPS)
        Bq, T = h.shape[:2]
        return (hn.reshape(Bq, T, MLSTM_DIM) * norm_g * jax.nn.sigmoid(o)).astype(zl.dtype)

    out_l = post(h_lat[0] + h_lat[1], ol)
    out_c = post(h_ctx[0] + h_ctx[1], oc) if ctx_out else None
    return out_l, out_c


def setup_inputs(seed: int = 0) -> dict:
    key = jax.random.key(seed)
    ks = iter(jax.random.split(key, 48))
    nrm = lambda shape, scale: jax.random.normal(next(ks), shape, jnp.float32) * scale
    L, D = DEPTH, D_MODEL
    inp = {}
    inp['x'] = nrm((BATCH, SEQ, D), 1.0)
    inp['c'] = nrm((BATCH, D), 1.0)
    inp['ctx'] = nrm((BATCH, CTX_LEN, D), 1.0)
    inp['c_ctx'] = nrm((D,), 1.0)
    inp['mod_w'] = nrm((L, D, 6 * D), 0.5 * D ** -0.5)
    inp['mod_b'] = nrm((L, 6 * D), 0.01)
    inp['norm1_g'] = 1.0 + nrm((L, D), 0.02)
    inp['norm2_g'] = 1.0 + nrm((L, D), 0.02)
    inp['w_in'] = nrm((L, D, IN_COLS), D ** -0.5)
    inp['w_out'] = nrm((L, MIX_DIM, D), 0.5 * MIX_DIM ** -0.5)
    inp['ffn_w_in'] = nrm((L, D, 2 * FFN_HIDDEN), D ** -0.5)
    inp['ffn_w_out'] = nrm((L, FFN_HIDDEN, D), 0.5 * FFN_HIDDEN ** -0.5)
    inp['rwkv_mu'] = jax.random.uniform(next(ks), (L, RWKV_COLS), jnp.float32)
    inp['rwkv_w0'] = nrm((L, 2, RWKV_DIM), 0.5)
    inp['rwkv_w2'] = nrm((L, 2, RWKV_W_LORA, RWKV_DIM), 0.1)
    inp['rwkv_a0'] = nrm((L, 2, RWKV_DIM), 0.5)
    inp['rwkv_a2'] = nrm((L, 2, RWKV_A_LORA, RWKV_DIM), 0.1)
    inp['rwkv_g2'] = nrm((L, RWKV_G_LORA, RWKV_DIM), RWKV_G_LORA ** -0.5)
    inp['rwkv_k_k'] = 0.85 + nrm((L, RWKV_DIM), 0.05)
    inp['rwkv_k_a'] = 1.0 + nrm((L, RWKV_DIM), 0.05)
    inp['rwkv_r_k'] = nrm((L, RWKV_HEADS, HEAD_DIM), 0.1)
    inp['rwkv_ln_g'] = 1.0 + nrm((L, RWKV_DIM), 0.02)
    inp['rwkv_ln_b'] = nrm((L, RWKV_DIM), 0.01)
    inp['mla_q_norm_g'] = 1.0 + nrm((L, MLA_Q_RANK), 0.02)
    inp['mla_w_uq'] = nrm((L, MLA_Q_RANK, MLA_HEADS * MLA_QK), MLA_Q_RANK ** -0.5)
    inp['mla_kv_norm_g'] = 1.0 + nrm((L, MLA_KV_RANK), 0.02)
    inp['mla_w_ukv'] = nrm((L, MLA_KV_RANK, MLA_HEADS * (MLA_NOPE + MLA_V)), MLA_KV_RANK ** -0.5)
    inp['mla_q_qknorm_g'] = 1.0 + nrm((L, MLA_QK), 0.02)
    inp['mla_k_qknorm_g'] = 1.0 + nrm((L, MLA_QK), 0.02)
    inp['mlstm_conv_w'] = nrm((L, 3, 2 * MLSTM_HEADS * MLSTM_QK), 3 ** -0.5)
    inp['mlstm_conv_b'] = nrm((L, 2 * MLSTM_HEADS * MLSTM_QK), 0.01)
    inp['mlstm_i_b'] = nrm((L, 2, MLSTM_HEADS), 0.1)
    inp['mlstm_f_b'] = jnp.linspace(3.0, 6.0, MLSTM_HEADS, dtype=jnp.float32) + nrm((L, 2, MLSTM_HEADS), 0.1)
    inp['mlstm_norm_g'] = 1.0 + nrm((L, MLSTM_DIM), 0.02)
    return inp


def reference(x, c, ctx, c_ctx, mod_w, mod_b, norm1_g, norm2_g, w_in, w_out, ffn_w_in, ffn_w_out,
              rwkv_mu, rwkv_w0, rwkv_w2, rwkv_a0, rwkv_a2, rwkv_g2, rwkv_k_k, rwkv_k_a, rwkv_r_k,
              rwkv_ln_g, rwkv_ln_b, mla_q_norm_g, mla_w_uq, mla_kv_norm_g, mla_w_ukv,
              mla_q_qknorm_g, mla_k_qknorm_g, mlstm_conv_w, mlstm_conv_b, mlstm_i_b, mlstm_f_b,
              mlstm_norm_g):
    cos, sin = axial_rope_tables(x.shape[1])
    xl, xc = x, ctx
    silu_c = jax.nn.silu(c)
    silu_cc = jax.nn.silu(c_ctx)
    for i in range(DEPTH):
        ctx_out = i < DEPTH - 1
        mod_l = silu_c @ mod_w[i] + mod_b[i]
        mod_c = silu_cc @ mod_w[i] + mod_b[i]
        sh1_l, sc1_l, gt1_l, sh2_l, sc2_l, gt2_l = jnp.split(mod_l[:, None, :], 6, axis=-1)
        sh1_c, sc1_c, gt1_c, sh2_c, sc2_c, gt2_c = jnp.split(mod_c, 6, axis=-1)
        zl = modulate(rmsnorm(xl, norm1_g[i]), sh1_l, sc1_l) @ w_in[i]
        zc = modulate(rmsnorm(xc, norm1_g[i]), sh1_c, sc1_c) @ w_in[i]
        za_l, zb_l, zm_l = jnp.split(zl, [RWKV_COLS, RWKV_COLS + MLA_COLS], axis=-1)
        za_c, zb_c, zm_c = jnp.split(zc, [RWKV_COLS, RWKV_COLS + MLA_COLS], axis=-1)
        ya_l, ya_c = rwkv7_mixer(za_l, za_c, rwkv_mu[i], rwkv_w0[i], rwkv_w2[i], rwkv_a0[i], rwkv_a2[i],
                                 rwkv_g2[i], rwkv_k_k[i], rwkv_k_a[i], rwkv_r_k[i], rwkv_ln_g[i],
                                 rwkv_ln_b[i], ctx_out)
        yb_l, yb_c = mla_mixer(zb_l, zb_c, mla_q_norm_g[i], mla_w_uq[i], mla_kv_norm_g[i], mla_w_ukv[i],
                               mla_q_qknorm_g[i], mla_k_qknorm_g[i], cos, sin, ctx_out)
        ym_l, ym_c = mlstm_mixer(zm_l, zm_c, mlstm_conv_w[i], mlstm_conv_b[i], mlstm_i_b[i],
                                 mlstm_f_b[i], mlstm_norm_g[i], ctx_out)
        xl = xl + gt1_l * (jnp.concatenate([ya_l, yb_l, ym_l], axis=-1) @ w_out[i])
        xl = xl + gt2_l * swiglu(modulate(rmsnorm(xl, norm2_g[i]), sh2_l, sc2_l), ffn_w_in[i], ffn_w_out[i])
        if ctx_out:
            xc = xc + gt1_c * (jnp.concatenate([ya_c, yb_c, ym_c], axis=-1) @ w_out[i])
            xc = xc + gt2_c * swiglu(modulate(rmsnorm(xc, norm2_g[i]), sh2_c, sc2_c), ffn_w_in[i], ffn_w_out[i])
    return xl
```

```python
import functools

import jax
import jax.numpy as jnp
from jax import lax
from jax.experimental import pallas as pl
from jax.experimental.pallas import tpu as pltpu

F32 = jnp.float32
BF16 = jnp.bfloat16

HEAD_DIM = 64
NORM_EPS = 1e-6
GRID_W = 64
ROPE_BASE = 10000.0
RWKV_HEADS = 4
RWKV_DIM = RWKV_HEADS * HEAD_DIM
RWKV_W_LORA = 64
RWKV_A_LORA = 64
RWKV_G_LORA = 128
RWKV_COLS = 3 * RWKV_DIM + 2 * RWKV_W_LORA + 2 * RWKV_A_LORA + RWKV_G_LORA
RWKV_GN_EPS = 64e-5
MLA_HEADS = 8
MLA_NOPE = 64
MLA_ROPE = 32
MLA_V = HEAD_DIM
MLA_QK = MLA_NOPE + MLA_ROPE
MLA_Q_RANK = 512
MLA_KV_RANK = 256
MLA_DIM = MLA_HEADS * MLA_V
MLA_COLS = MLA_Q_RANK + MLA_KV_RANK + MLA_ROPE
MLSTM_HEADS = 4
MLSTM_QK = 32
MLSTM_V = HEAD_DIM
MLSTM_DIM = MLSTM_HEADS * MLSTM_V
MLSTM_QKW = 2 * MLSTM_HEADS * MLSTM_QK
GATE_SOFTCAP = 15.0
MLSTM_COLS = MLSTM_QKW + 2 * MLSTM_DIM + 4 * MLSTM_HEADS

LANES = 128
SUBLANES = 8
CHUNK = 64
QPAD = LANES
ZB_COLS = MLA_Q_RANK + MLA_KV_RANK + LANES
ZM_COLS = MLSTM_QKW + 2 * MLSTM_DIM + LANES
SHIFT_COLS = RWKV_COLS + MLSTM_QKW
IN_COLS_PAD = RWKV_COLS + ZM_COLS + ZB_COLS
VMEM_LIMIT = 56 * 1024 * 1024


def _split(a):
    hi = a.astype(BF16)
    lo = (a - hi.astype(F32)).astype(BF16)
    return hi, lo


_NN = (((1,), (0,)), ((), ()))
_NT = (((1,), (1,)), ((), ()))
_TN = (((0,), (0,)), ((), ()))


def _dg(a, b, dims):
    return lax.dot_general(a, b, dims, preferred_element_type=F32)


def _dot1(a, b, dims=_NN):
    return _dg(a.astype(BF16), b.astype(BF16), dims)


def _dot3(a, b, dims=_NN):
    ah, al = _split(a)
    bh, bl = _split(b)
    return _dg(ah, bh, dims) + (_dg(ah, bl, dims) + _dg(al, bh, dims))


def _dot_exact_rhs(a, b01, dims=_NN):
    a1 = a.astype(BF16)
    r1 = a - a1.astype(F32)
    a2 = r1.astype(BF16)
    a3 = (r1 - a2.astype(F32)).astype(BF16)
    return _dg(a1, b01, dims) + (_dg(a2, b01, dims) + _dg(a3, b01, dims))


def _dot_exact_lhs(a01, b, dims=_NN):
    b1 = b.astype(BF16)
    r1 = b - b1.astype(F32)
    b2 = r1.astype(BF16)
    b3 = (r1 - b2.astype(F32)).astype(BF16)
    return _dg(a01, b1, dims) + (_dg(a01, b2, dims) + _dg(a01, b3, dims))


def _iota(shape, dim):
    return lax.broadcasted_iota(jnp.int32, shape, dim)


def _seg_ones(n, seg):
    r = _iota((n, n), 0) // seg
    c = _iota((n, n), 1) // seg
    return (r == c).astype(BF16)


def _segsum(x, ones_bd):
    return _dot_exact_rhs(x, ones_bd)


def _transpose(x):
    n = x.shape[1]
    eye = (_iota((n, n), 0) == _iota((n, n), 1)).astype(BF16)
    return _dot_exact_lhs(eye, x, _NT)


def _softplus(x):
    return jnp.maximum(x, 0.0) + jnp.log(1.0 + jnp.exp(-jnp.abs(x)))


def _sigmoid(x):
    return 1.0 / (1.0 + jnp.exp(-x))


def _silu(x):
    return x * _sigmoid(x)


def _rmsnorm_mod(x, g, shift, scale):
    y = x * lax.rsqrt(jnp.mean(x * x, axis=-1, keepdims=True) + NORM_EPS)
    return (y * g) * (1.0 + scale) + shift


def _chunk_index(i, n_ctx_chunks, n_chunks, rev):
    if not rev:
        return i
    return jnp.where(i < n_ctx_chunks, n_ctx_chunks - 1 - i,
                     n_chunks - 1 - (i - n_ctx_chunks))


def _tri_masks(n, rev):
    r = _iota((n, n), 0)
    c = _iota((n, n), 1)
    if rev:
        return c > r, c >= r
    return c < r, c <= r


def _mod_kernel(c_ref, w_ref, b_ref, o_ref):
    o_ref[0] = _dot3(_silu(c_ref[...]), w_ref[0]) + b_ref[0]


def _modulation(cvec, mod_w, mod_b):
    depth, d, n = mod_w.shape
    tn = 1536
    rows = cvec.shape[0]
    return pl.pallas_call(
        _mod_kernel,
        grid=(depth, n // tn),
        in_specs=[
            pl.BlockSpec((rows, d), lambda l, j: (0, 0)),
            pl.BlockSpec((1, d, tn), lambda l, j: (l, 0, j)),
            pl.BlockSpec((1, 1, tn), lambda l, j: (l, 0, j)),
        ],
        out_specs=pl.BlockSpec((1, rows, tn), lambda l, j: (l, 0, j)),
        out_shape=jax.ShapeDtypeStruct((depth, rows, n), F32),
        compiler_params=pltpu.CompilerParams(
            dimension_semantics=("arbitrary", "arbitrary"), vmem_limit_bytes=VMEM_LIMIT),
        name="adaln_mod",
    )(cvec, mod_w, mod_b.reshape(depth, 1, n))


def _inproj_kernel(x_ref, xp_ref, xn_ref, mod_ref, g_ref, w_ref, mu_ref, cw_ref, cb_ref,
                   za_ref, zm_ref, zb_ref, *, tm, n_ctx_tiles, n_tiles):
    j = pl.program_id(1)
    g = g_ref[...]
    shift = mod_ref[0, 0, 0:1, :]
    scale = mod_ref[0, 0, 1:2, :]
    h = _rmsnorm_mod(x_ref[0], g, shift, scale).astype(BF16)
    z = _dg(h, w_ref[...], _NN)
    halo = jnp.concatenate([xp_ref[0], xn_ref[0]], axis=0)
    hh = _rmsnorm_mod(halo, g, shift, scale).astype(BF16)
    zh = _dg(hh, w_ref[:, :SHIFT_COLS], _NN)
    prev_ok = jnp.logical_and(j != 0, j != n_ctx_tiles).astype(F32)
    next_ok = jnp.logical_and(j != n_ctx_tiles - 1, j != n_tiles - 1).astype(F32)
    zs = z[:, :SHIFT_COLS]
    row = _iota((tm, SHIFT_COLS), 0)
    prev = jnp.where(row == 0, zh[SUBLANES - 1:SUBLANES, :] * prev_ok, pltpu.roll(zs, 1, 0))
    nxt = jnp.where(row == tm - 1, zh[SUBLANES:SUBLANES + 1, :] * next_ok,
                    pltpu.roll(zs, tm - 1, 0))
    za = zs[:, :RWKV_COLS]
    za_ref[0] = za + mu_ref[...] * (0.5 * (prev[:, :RWKV_COLS] + nxt[:, :RWKV_COLS]) - za)
    qk = (prev[:, RWKV_COLS:] * cw_ref[0:1, :] + zs[:, RWKV_COLS:] * cw_ref[1:2, :]
          + nxt[:, RWKV_COLS:] * cw_ref[2:3, :] + cb_ref[...])
    zm_ref[0, :, :MLSTM_QKW] = _silu(qk)
    zm_ref[0, :, MLSTM_QKW:] = z[:, SHIFT_COLS:RWKV_COLS + ZM_COLS]
    zb_ref[0] = z[:, RWKV_COLS + ZM_COLS:]


def _in_proj(x, modsel, norm_g, w_r, mu, conv_w, conv_b, *, tm, ctx_len):
    b, l, d = x.shape
    n_tiles = l // tm
    n_ctx_tiles = ctx_len // tm
    tm8 = tm // SUBLANES
    kern = functools.partial(_inproj_kernel, tm=tm, n_ctx_tiles=n_ctx_tiles, n_tiles=n_tiles)
    seg = lambda j: jnp.where(j >= n_ctx_tiles, 1, 0)
    return pl.pallas_call(
        kern,
        grid=(b, n_tiles),
        in_specs=[
            pl.BlockSpec((1, tm, d), lambda i, j: (i, j, 0)),
            pl.BlockSpec((1, SUBLANES, d), lambda i, j: (i, jnp.maximum(j * tm8 - 1, 0), 0)),
            pl.BlockSpec((1, SUBLANES, d),
                         lambda i, j: (i, jnp.minimum((j + 1) * tm8, l // SUBLANES - 1), 0)),
            pl.BlockSpec((1, 1, 2, d), lambda i, j: (i, seg(j), 0, 0)),
            pl.BlockSpec((1, d), lambda i, j: (0, 0)),
            pl.BlockSpec((d, IN_COLS_PAD), lambda i, j: (0, 0)),
            pl.BlockSpec((1, RWKV_COLS), lambda i, j: (0, 0)),
            pl.BlockSpec((3, MLSTM_QKW), lambda i, j: (0, 0)),
            pl.BlockSpec((1, MLSTM_QKW), lambda i, j: (0, 0)),
        ],
        out_specs=[
            pl.BlockSpec((1, tm, RWKV_COLS), lambda i, j: (i, j, 0)),
            pl.BlockSpec((1, tm, ZM_COLS), lambda i, j: (i, j, 0)),
            pl.BlockSpec((1, tm, ZB_COLS), lambda i, j: (i, j, 0)),
        ],
        out_shape=[
            jax.ShapeDtypeStruct((b, l, RWKV_COLS), F32),
            jax.ShapeDtypeStruct((b, l, ZM_COLS), F32),
            jax.ShapeDtypeStruct((b, l, ZB_COLS), F32),
        ],
        compiler_params=pltpu.CompilerParams(
            dimension_semantics=("arbitrary", "arbitrary"), vmem_limit_bytes=VMEM_LIMIT),
        name="in_proj",
    )(x, x, x, modsel, norm_g.reshape(1, d), w_r, mu.reshape(1, RWKV_COLS), conv_w,
      conv_b.reshape(1, MLSTM_QKW))


def _rwkv_kernel(za_ref, kk_ref, ka_ref, rk_ref, w0_ref, w2_ref, a0_ref, a2_ref, g2_ref,
                 *rest, rev):
    if rev:
        y_ref, m_ref = rest
    else:
        y_ref, bonus_ref, gate_ref, m_ref = rest
    c = CHUNK
    nd = RWKV_DIM

    @pl.when(pl.program_id(1) == 0)
    def _():
        m_ref[...] = jnp.zeros_like(m_ref)

    za = za_ref[0]
    r = za[:, 0:nd]
    k = za[:, nd:2 * nd]
    v = za[:, 2 * nd:3 * nd]
    d = 1 if rev else 0
    wo = 3 * nd + d * RWKV_W_LORA
    ao = 3 * nd + 2 * RWKV_W_LORA + d * RWKV_A_LORA
    wd = za[:, wo:wo + RWKV_W_LORA]
    ad = za[:, ao:ao + RWKV_A_LORA]
    ones_bd = _seg_ones(nd, HEAD_DIM)

    kkr = k * kk_ref[...]
    kk = kkr / jnp.maximum(jnp.sqrt(_segsum(kkr * kkr, ones_bd)), 1e-12)
    w_lo = _dot3(jnp.tanh(wd), w2_ref[...])
    log_w = -_softplus(-(w0_ref[...] + w_lo)) - 0.5
    logdec = -jnp.exp(log_w)
    a = _sigmoid(a0_ref[...] + _dot3(ad, a2_ref[...]))
    kd = k * (1.0 + (a - 1.0) * ka_ref[...])
    bvec = kk * a

    strict, incl = _tri_masks(c, rev)
    cum = _dot_exact_lhs(incl.astype(BF16), logdec)
    if rev:
        total = cum[0:1, :]
    else:
        total = cum[c - 1:c, :]
    pm = kk * jnp.exp(cum - logdec)
    qm = bvec * jnp.exp(-cum)
    km = kd * jnp.exp(-cum)
    rm = r * jnp.exp(cum)
    qe = bvec * jnp.exp(total - cum)
    ke = kd * jnp.exp(total - cum)
    gam = jnp.exp(total)

    eye = _iota((c, c), 0) == _iota((c, c), 1)
    eye_f = eye.astype(F32)
    ys = []
    for h in range(RWKV_HEADS):
        sl = slice(h * HEAD_DIM, (h + 1) * HEAD_DIM)
        ph, qh, kh, rh, vh = pm[:, sl], qm[:, sl], km[:, sl], rm[:, sl], v[:, sl]
        lpq = jnp.where(strict, _dot3(ph, qh, _NT), 0.0)
        lpk = jnp.where(strict, _dot3(ph, kh, _NT), 0.0)
        arq = jnp.where(incl, _dot3(rh, qh, _NT), 0.0)
        ark = jnp.where(incl, _dot3(rh, kh, _NT), 0.0)
        tinv = eye_f - lpq
        pw = _dot3(lpq, lpq)
        n_sq = 1
        while True:
            tinv = tinv + _dot3(tinv, pw)
            n_sq *= 2
            if 2 * n_sq >= c:
                break
            pw = _dot3(pw, pw)
        m0 = m_ref[h]
        rhs = _dot3(ph, m0) + _dot3(lpk, vh)
        u = -_dot3(tinv, rhs)
        ys.append(_dot3(rh, m0) + _dot3(arq, u) + _dot3(ark, vh))
        dg = jnp.where(eye, jnp.broadcast_to(gam[:, sl], (c, c)), 0.0)
        m_ref[h] = (_dot3(dg, m0) + _dot3(qe[:, sl], u, _TN) + _dot3(ke[:, sl], vh, _TN))
    y_ref[0] = jnp.concatenate(ys, axis=1)
    if not rev:
        bonus_ref[0] = _segsum(r * k * rk_ref[...], ones_bd) * v
        gd = za[:, 3 * nd + 2 * RWKV_W_LORA + 2 * RWKV_A_LORA:]
        gate_ref[0] = _dot3(_sigmoid(gd), g2_ref[...])


def _rwkv(za, k_k, k_a, r_k, w0, w2, a0, a2, g2, *, ctx_len, rev):
    b, l, _ = za.shape
    c = CHUNK
    n_chunks = l // c
    n_ctx_chunks = ctx_len // c
    d = 1 if rev else 0
    idx = lambda i, j: (i, _chunk_index(j, n_ctx_chunks, n_chunks, rev), 0)
    vec = lambda n: pl.BlockSpec((1, n), lambda i, j: (0, 0))
    y_spec = pl.BlockSpec((1, c, RWKV_DIM), idx)
    y_shape = jax.ShapeDtypeStruct((b, l, RWKV_DIM), F32)
    n_out = 1 if rev else 3
    return pl.pallas_call(
        functools.partial(_rwkv_kernel, rev=rev),
        grid=(b, n_chunks),
        in_specs=[
            pl.BlockSpec((1, c, RWKV_COLS), idx),
            vec(RWKV_DIM), vec(RWKV_DIM), vec(RWKV_DIM), vec(RWKV_DIM),
            pl.BlockSpec((RWKV_W_LORA, RWKV_DIM), lambda i, j: (0, 0)),
            vec(RWKV_DIM),
            pl.BlockSpec((RWKV_A_LORA, RWKV_DIM), lambda i, j: (0, 0)),
            pl.BlockSpec((RWKV_G_LORA, RWKV_DIM), lambda i, j: (0, 0)),
        ],
        out_specs=[y_spec] * n_out,
        out_shape=[y_shape] * n_out,
        scratch_shapes=[pltpu.VMEM((RWKV_HEADS, HEAD_DIM, HEAD_DIM), F32)],
        compiler_params=pltpu.CompilerParams(
            dimension_semantics=("arbitrary", "arbitrary"), vmem_limit_bytes=VMEM_LIMIT),
        name="rwkv_rev" if rev else "rwkv_fwd",
    )(za, k_k.reshape(1, -1), k_a.reshape(1, -1), r_k.reshape(1, -1), w0[d].reshape(1, -1),
      w2[d], a0[d].reshape(1, -1), a2[d], g2)


def _mlstm_kernel(zm_ref, gb_ref, h_ref, c_ref, n_ref, m_ref, *, rev):
    c = CHUNK
    dk = MLSTM_QK
    dv = MLSTM_V
    d = 1 if rev else 0

    @pl.when(pl.program_id(1) == 0)
    def _():
        c_ref[...] = jnp.zeros_like(c_ref)
        n_ref[...] = jnp.zeros_like(n_ref)
        m_ref[...] = jnp.zeros_like(m_ref)

    zm = zm_ref[0]
    gates = zm[:, MLSTM_QKW + 2 * MLSTM_DIM:] + gb_ref[...]
    capped = GATE_SOFTCAP * jnp.tanh(gates * (1.0 / GATE_SOFTCAP))
    log_f_all = -_softplus(-capped)
    strict, incl = _tri_masks(c, rev)
    b_all = _dot_exact_lhs(incl.astype(BF16), log_f_all)
    li_t = _transpose(capped)
    b_t = _transpose(b_all)
    last = 0 if rev else c - 1
    hs = []
    for h in range(MLSTM_HEADS):
        ci = d * 2 * MLSTM_HEADS + h
        cf = ci + MLSTM_HEADS
        q = zm[:, h * dk:(h + 1) * dk] * (dk ** -0.5)
        k = zm[:, MLSTM_HEADS * dk + h * dk:MLSTM_HEADS * dk + (h + 1) * dk]
        v = zm[:, MLSTM_QKW + h * dv:MLSTM_QKW + (h + 1) * dv]
        li_col = capped[:, ci:ci + 1]
        li_row = li_t[ci:ci + 1, :]
        b_col = b_all[:, cf:cf + 1]
        b_row = b_t[cf:cf + 1, :]
        g = b_col[last:last + 1, :]
        m_s = m_ref[h]
        c_s = c_ref[h]
        n_s = n_ref[h]
        w_end_row = g - b_row + li_row
        m_loc = jnp.max(w_end_row, axis=1, keepdims=True)
        e_row = jnp.exp(w_end_row - m_loc)
        e_col = jnp.exp(g - b_col + li_col - m_loc)
        c_loc = _dot3(k * e_col, v, _TN)
        n_loc = _dot3(jnp.broadcast_to(e_row, (SUBLANES, c)), k)[0:1, :]
        log_inter = b_col + m_s
        log_d = jnp.where(incl, b_col - b_row + li_row, -jnp.inf)
        m_out = jnp.maximum(log_inter, jnp.max(log_d, axis=1, keepdims=True))
        w_intra = _dot3(q, k, _NT) * jnp.exp(log_d - m_out)
        s_inter = jnp.exp(log_inter - m_out)
        num = _dot3(w_intra, v) + s_inter * _dot3(q, c_s)
        den = (jnp.sum(w_intra, axis=1, keepdims=True)
               + s_inter * jnp.sum(q * n_s, axis=1, keepdims=True))
        hs.append(num / jnp.maximum(jnp.abs(den), jnp.exp(-m_out)))
        m_new = jnp.maximum(g + m_s, m_loc)
        s_old = jnp.exp(g + m_s - m_new)
        s_loc = jnp.exp(m_loc - m_new)
        c_ref[h] = s_old * c_s + s_loc * c_loc
        n_ref[h] = s_old * n_s + s_loc * n_loc
        m_ref[h] = m_new
    h_ref[0] = jnp.concatenate(hs, axis=1)


def _mlstm(zm, gate_bias, *, ctx_len, rev):
    b, l, _ = zm.shape
    c = CHUNK
    n_chunks = l // c
    n_ctx_chunks = ctx_len // c
    idx = lambda i, j: (i, _chunk_index(j, n_ctx_chunks, n_chunks, rev), 0)
    return pl.pallas_call(
        functools.partial(_mlstm_kernel, rev=rev),
        grid=(b, n_chunks),
        in_specs=[
            pl.BlockSpec((1, c, ZM_COLS), idx),
            pl.BlockSpec((1, LANES), lambda i, j: (0, 0)),
        ],
        out_specs=pl.BlockSpec((1, c, MLSTM_DIM), idx),
        out_shape=jax.ShapeDtypeStruct((b, l, MLSTM_DIM), F32),
        scratch_shapes=[
            pltpu.VMEM((MLSTM_HEADS, MLSTM_QK, MLSTM_V), F32),
            pltpu.VMEM((MLSTM_HEADS, 1, MLSTM_QK), F32),
            pltpu.VMEM((MLSTM_HEADS, 1, 1), F32),
        ],
        compiler_params=pltpu.CompilerParams(
            dimension_semantics=("arbitrary", "arbitrary"), vmem_limit_bytes=VMEM_LIMIT),
        name="mlstm_rev" if rev else "mlstm_fwd",
    )(zm, gate_bias)


def _rope_rot(x):
    lane = _iota(x.shape, 1)
    half = MLA_ROPE // 2
    return jnp.where(lane < MLA_NOPE + half, -pltpu.roll(x, LANES - half, 1),
                     pltpu.roll(x, half, 1))


def _mla_proj_kernel(zb_ref, cos_ref, sin_ref, qg_ref, kvg_ref, gq_ref, gk_ref,
                     wq_ref, wk_ref, wv_ref, q_ref, k_ref, v_ref):
    zb = zb_ref[0]
    cq = zb[:, :MLA_Q_RANK]
    ckv = zb[:, MLA_Q_RANK:MLA_Q_RANK + MLA_KV_RANK]
    kr = zb[:, MLA_Q_RANK + MLA_KV_RANK:]
    cos = cos_ref[...]
    sin = sin_ref[...]
    cqn = cq * lax.rsqrt(jnp.mean(cq * cq, axis=-1, keepdims=True) + NORM_EPS) * qg_ref[...]
    ckn = ckv * lax.rsqrt(jnp.mean(ckv * ckv, axis=-1, keepdims=True) + NORM_EPS) * kvg_ref[...]
    q_all = _dot1(cqn, wq_ref[...])
    k_all = _dot1(ckn, wk_ref[...])
    v_ref[0] = _dot1(ckn, wv_ref[...]).astype(BF16)
    scale = MLA_QK ** -0.5
    for h in range(MLA_HEADS):
        sl = slice(h * QPAD, (h + 1) * QPAD)
        qh = q_all[:, sl]
        qn = qh * lax.rsqrt(jnp.sum(qh * qh, axis=-1, keepdims=True) * (1.0 / MLA_QK) + NORM_EPS)
        qn = qn * gq_ref[...]
        q_ref[0, :, sl] = ((qn * cos + _rope_rot(qn) * sin) * scale).astype(BF16)
        kh = k_all[:, sl] + kr
        kn = kh * lax.rsqrt(jnp.sum(kh * kh, axis=-1, keepdims=True) * (1.0 / MLA_QK) + NORM_EPS)
        kn = kn * gk_ref[...]
        k_ref[0, :, sl] = (kn * cos + _rope_rot(kn) * sin).astype(BF16)


def _mla_proj(zb, cos_f, sin_f, q_norm_g, kv_norm_g, gq, gk, wq_r, wk_r, wv_r, *, tm):
    b, l, _ = zb.shape
    hq = MLA_HEADS * QPAD
    full = lambda shape: pl.BlockSpec(shape, lambda i, j: (0,) * len(shape))
    return pl.pallas_call(
        _mla_proj_kernel,
        grid=(b, l // tm),
        in_specs=[
            pl.BlockSpec((1, tm, ZB_COLS), lambda i, j: (i, j, 0)),
            pl.BlockSpec((tm, QPAD), lambda i, j: (j, 0)),
            pl.BlockSpec((tm, QPAD), lambda i, j: (j, 0)),
            full((1, MLA_Q_RANK)), full((1, MLA_KV_RANK)), full((1, QPAD)), full((1, QPAD)),
            full((MLA_Q_RANK, hq)), full((MLA_KV_RANK, hq)), full((MLA_KV_RANK, MLA_DIM)),
        ],
        out_specs=[
            pl.BlockSpec((1, tm, hq), lambda i, j: (i, j, 0)),
            pl.BlockSpec((1, tm, hq), lambda i, j: (i, j, 0)),
            pl.BlockSpec((1, tm, MLA_DIM), lambda i, j: (i, j, 0)),
        ],
        out_shape=[
            jax.ShapeDtypeStruct((b, l, hq), BF16),
            jax.ShapeDtypeStruct((b, l, hq), BF16),
            jax.ShapeDtypeStruct((b, l, MLA_DIM), BF16),
        ],
        compiler_params=pltpu.CompilerParams(
            dimension_semantics=("arbitrary", "arbitrary"), vmem_limit_bytes=VMEM_LIMIT),
        name="mla_proj",
    )(zb, cos_f, sin_f, q_norm_g.reshape(1, -1), kv_norm_g.reshape(1, -1), gq, gk,
      wq_r, wk_r, wv_r)


def _mla_attn_kernel(q_ref, k_ref, v_ref, o_ref, *, ctx_len, first_tile, n_ctx_tiles):
    tq = q_ref.shape[1]
    l = k_ref.shape[1]
    is_ctx = (pl.program_id(2) + first_tile) < n_ctx_tiles
    col = _iota((tq, l), 1)
    visible = jnp.logical_or(col < ctx_len, jnp.logical_not(is_ctx))
    lane = _iota((l, 2 * MLA_V), 1)
    v = v_ref[0]
    acc = jnp.zeros((tq, 2 * MLA_V), F32)
    for h in range(2):
        sl = slice(h * QPAD, (h + 1) * QPAD)
        s = _dg(q_ref[0, :, sl], k_ref[0, :, sl], _NT)
        s = jnp.where(visible, s, -jnp.inf)
        m = jnp.max(s, axis=-1, keepdims=True)
        p = jnp.exp(s - m)
        denom = jnp.sum(p, axis=-1, keepdims=True)
        vh = jnp.where((lane // MLA_V) == h, v, jnp.zeros_like(v))
        acc = acc + _dg(p.astype(BF16), vh, _NN) / denom
    o_ref[0] = acc


def _mla_attn(q, k, v, *, tq, ctx_len, first_tile):
    b, l, _ = q.shape
    n_q = l // tq - first_tile
    pairs = MLA_HEADS // 2
    kern = functools.partial(_mla_attn_kernel, ctx_len=ctx_len, first_tile=first_tile,
                             n_ctx_tiles=ctx_len // tq)
    return pl.pallas_call(
        kern,
        grid=(b, pairs, n_q),
        in_specs=[
            pl.BlockSpec((1, tq, 2 * QPAD), lambda i, p, j: (i, j + first_tile, p)),
            pl.BlockSpec((1, l, 2 * QPAD), lambda i, p, j: (i, 0, p)),
            pl.BlockSpec((1, l, 2 * MLA_V), lambda i, p, j: (i, 0, p)),
        ],
        out_specs=pl.BlockSpec((1, tq, 2 * MLA_V), lambda i, p, j: (i, j + first_tile, p)),
        out_shape=jax.ShapeDtypeStruct((b, l, MLA_DIM), F32),
        compiler_params=pltpu.CompilerParams(
            dimension_semantics=("arbitrary", "arbitrary", "arbitrary"),
            vmem_limit_bytes=VMEM_LIMIT),
        name="mla_attn",
    )(q, k, v)


def _outproj_kernel(x_ref, yf_ref, yr_ref, bonus_ref, gate_ref, attn_ref, hf_ref, hr_ref,
                    o_ref, mod_ref, lng_ref, lnb_ref, mng_ref, wa_ref, wb_ref, wm_ref, out_ref):
    ones_bd = _seg_ones(RWKV_DIM, HEAD_DIM)
    inv = 1.0 / HEAD_DIM
    y = yf_ref[0] + yr_ref[0]
    mean = _segsum(y, ones_bd) * inv
    yc = y - mean
    var = _segsum(yc * yc, ones_bd) * inv
    yn = yc * lax.rsqrt(var + RWKV_GN_EPS) * lng_ref[...] + lnb_ref[...]
    ya = (yn + bonus_ref[0]) * gate_ref[0]
    hm = hf_ref[0] + hr_ref[0]
    hn = hm * lax.rsqrt(_segsum(hm * hm, ones_bd) * inv + NORM_EPS)
    ym = hn * mng_ref[...] * _sigmoid(o_ref[0])
    mix = (_dot1(ya, wa_ref[...]) + _dot1(attn_ref[0], wb_ref[...]) + _dot1(ym, wm_ref[...]))
    out_ref[0] = x_ref[0] + mod_ref[0, 0] * mix


def _out_proj(x, yf, yr, bonus, gate, attn, hf, hr, zm, modsel, ln_g, ln_b, mnorm_g,
              wa, wb, wm, *, tm, ctx_len, first_tile):
    b, l, d = x.shape
    n_ctx_tiles = ctx_len // tm
    n_tiles = l // tm - first_tile
    row = lambda w: pl.BlockSpec((1, tm, w), lambda i, j: (i, j + first_tile, 0))
    full = lambda shape: pl.BlockSpec(shape, lambda i, j: (0,) * len(shape))
    seg = lambda j: jnp.where(j + first_tile >= n_ctx_tiles, 1, 0)
    o_block = (MLSTM_QKW + MLSTM_DIM) // MLSTM_DIM
    return pl.pallas_call(
        _outproj_kernel,
        grid=(b, n_tiles),
        in_specs=[
            row(d), row(RWKV_DIM), row(RWKV_DIM), row(RWKV_DIM), row(RWKV_DIM), row(MLA_DIM),
            row(MLSTM_DIM), row(MLSTM_DIM),
            pl.BlockSpec((1, tm, MLSTM_DIM), lambda i, j: (i, j + first_tile, o_block)),
            pl.BlockSpec((1, 1, 1, d), lambda i, j: (i, seg(j), 0, 0)),
            full((1, RWKV_DIM)), full((1, RWKV_DIM)), full((1, MLSTM_DIM)),
            full((RWKV_DIM, d)), full((MLA_DIM, d)), full((MLSTM_DIM, d)),
        ],
        out_specs=pl.BlockSpec((1, tm, d), lambda i, j: (i, j, 0)),
        out_shape=jax.ShapeDtypeStruct((b, n_tiles * tm, d), F32),
        compiler_params=pltpu.CompilerParams(
            dimension_semantics=("arbitrary", "arbitrary"), vmem_limit_bytes=VMEM_LIMIT),
        name="out_proj",
    )(x, yf, yr, bonus, gate, attn, hf, hr, zm, modsel, ln_g.reshape(1, -1),
      ln_b.reshape(1, -1), mnorm_g.reshape(1, -1), wa, wb, wm)


def _ffn_kernel(x_ref, mod_ref, g_ref, w1_ref, w2_ref, out_ref, *, hidden):
    x = x_ref[0]
    h = _rmsnorm_mod(x, g_ref[...], mod_ref[0, 0, 0:1, :], mod_ref[0, 0, 1:2, :]).astype(BF16)
    gu = _dg(h, w1_ref[...], _NN)
    act = (_silu(gu[:, :hidden]) * gu[:, hidden:]).astype(BF16)
    out_ref[0] = x + mod_ref[0, 0, 2:3, :] * _dg(act, w2_ref[...], _NN)


def _ffn(x, modsel, norm_g, w1, w2, *, tm, n_ctx_tiles_here):
    b, l, d = x.shape
    hidden = w2.shape[0]
    seg = lambda j: jnp.where(j >= n_ctx_tiles_here, 1, 0)
    return pl.pallas_call(
        functools.partial(_ffn_kernel, hidden=hidden),
        grid=(b, l // tm),
        in_specs=[
            pl.BlockSpec((1, tm, d), lambda i, j: (i, j, 0)),
            pl.BlockSpec((1, 1, 3, d), lambda i, j: (i, seg(j), 0, 0)),
            pl.BlockSpec((1, d), lambda i, j: (0, 0)),
            pl.BlockSpec((d, 2 * hidden), lambda i, j: (0, 0)),
            pl.BlockSpec((hidden, d), lambda i, j: (0, 0)),
        ],
        out_specs=pl.BlockSpec((1, tm, d), lambda i, j: (i, j, 0)),
        out_shape=jax.ShapeDtypeStruct((b, l, d), F32),
        compiler_params=pltpu.CompilerParams(
            dimension_semantics=("arbitrary", "arbitrary"), vmem_limit_bytes=VMEM_LIMIT),
        name="ffn",
    )(x, modsel, norm_g.reshape(1, d), w1, w2)


def _arrange_w_in(w_in):
    d = w_in.shape[0]
    wa = w_in[:, :RWKV_COLS]
    wb = w_in[:, RWKV_COLS:RWKV_COLS + MLA_COLS]
    wm = w_in[:, RWKV_COLS + MLA_COLS:]
    gates = wm[:, MLSTM_QKW + 2 * MLSTM_DIM:]
    gate_blk = jnp.pad(gates, ((0, 0), (0, LANES - gates.shape[1])))
    kr = wb[:, MLA_Q_RANK + MLA_KV_RANK:]
    kr_blk = jnp.pad(kr, ((0, 0), (MLA_NOPE, LANES - MLA_NOPE - MLA_ROPE)))
    out = jnp.concatenate([wa, wm[:, :MLSTM_QKW + 2 * MLSTM_DIM], gate_blk,
                           wb[:, :MLA_Q_RANK + MLA_KV_RANK], kr_blk], axis=1)
    assert out.shape == (d, IN_COLS_PAD)
    return out.astype(BF16)


def _arrange_mla_weights(w_uq, w_ukv):
    rq = w_uq.shape[0]
    wq = w_uq.reshape(rq, MLA_HEADS, MLA_QK)
    wq = jnp.pad(wq, ((0, 0), (0, 0), (0, QPAD - MLA_QK))).reshape(rq, MLA_HEADS * QPAD)
    rk = w_ukv.shape[0]
    wkv = w_ukv.reshape(rk, MLA_HEADS, MLA_NOPE + MLA_V)
    wk = jnp.pad(wkv[:, :, :MLA_NOPE], ((0, 0), (0, 0), (0, QPAD - MLA_NOPE)))
    wk = wk.reshape(rk, MLA_HEADS * QPAD)
    wv = wkv[:, :, MLA_NOPE:].reshape(rk, MLA_DIM)
    return wq.astype(BF16), wk.astype(BF16), wv.astype(BF16)


def _rope_tables(seq_len, ctx_len):
    rows = seq_len // GRID_W
    row = jnp.repeat(jnp.arange(rows, dtype=F32), GRID_W)
    col = jnp.tile(jnp.arange(GRID_W, dtype=F32), rows)
    n_freq = MLA_ROPE // 4
    inv = jnp.power(ROPE_BASE, -jnp.arange(n_freq, dtype=F32) / n_freq)
    ang = jnp.concatenate([row[:, None] * inv, col[:, None] * inv], axis=-1)
    cos, sin = jnp.cos(ang), jnp.sin(ang)
    pad_l, pad_r = MLA_NOPE, QPAD - MLA_QK
    cos_f = jnp.concatenate([jnp.ones((seq_len, pad_l), F32), cos, cos,
                             jnp.ones((seq_len, pad_r), F32)], axis=1)
    sin_f = jnp.concatenate([jnp.zeros((seq_len, pad_l), F32), sin, sin,
                             jnp.zeros((seq_len, pad_r), F32)], axis=1)
    cos_f = jnp.concatenate([jnp.ones((ctx_len, QPAD), F32), cos_f], axis=0)
    sin_f = jnp.concatenate([jnp.zeros((ctx_len, QPAD), F32), sin_f], axis=0)
    return cos_f, sin_f


def _pad_gain(g):
    return jnp.pad(g, (0, QPAD - g.shape[0])).reshape(1, QPAD)


def _gate_bias(i_b, f_b):
    gb = jnp.stack([i_b, f_b], axis=1).reshape(-1)
    return jnp.pad(gb, (0, LANES - gb.shape[0])).reshape(1, LANES)


def kernel(x, c, ctx, c_ctx, mod_w, mod_b, norm1_g, norm2_g, w_in, w_out, ffn_w_in, ffn_w_out, rwkv_mu, rwkv_w0, rwkv_w2, rwkv_a0, rwkv_a2, rwkv_g2, rwkv_k_k, rwkv_k_a, rwkv_r_k, rwkv_ln_g, rwkv_ln_b, mla_q_norm_g, mla_w_uq, mla_kv_norm_g, mla_w_ukv, mla_q_qknorm_g, mla_k_qknorm_g, mlstm_conv_w, mlstm_conv_b, mlstm_i_b, mlstm_f_b, mlstm_norm_g):
    bsz, seq, d = x.shape
    ctx_len = ctx.shape[1]
    depth = mod_w.shape[0]
    tm = min(256, ctx_len)
    assert ctx_len % tm == 0 and seq % tm == 0 and ctx_len % CHUNK == 0 and seq % CHUNK == 0

    xu = jnp.concatenate([ctx, x], axis=1)
    n_ctx_tiles = ctx_len // tm

    rows = -(-(bsz + 1) // SUBLANES) * SUBLANES
    cvec = jnp.zeros((rows, d), F32).at[:bsz].set(c).at[bsz].set(c_ctx)
    mod = _modulation(cvec, mod_w, mod_b).reshape(depth, rows, 6, d)
    mod_lat = mod[:, :bsz]
    mod_ctx = jnp.broadcast_to(mod[:, bsz][:, None], mod_lat.shape)
    modsel = jnp.stack([mod_ctx, mod_lat], axis=2)

    cos_f, sin_f = _rope_tables(seq, ctx_len)

    for i in range(depth):
        last = i == depth - 1
        first_tile = n_ctx_tiles if last else 0
        w_r = _arrange_w_in(w_in[i])
        za, zm, zb = _in_proj(xu, modsel[i, :, :, 0:2], norm1_g[i], w_r, rwkv_mu[i],
                              mlstm_conv_w[i], mlstm_conv_b[i], tm=tm, ctx_len=ctx_len)
        rw = (rwkv_k_k[i], rwkv_k_a[i], rwkv_r_k[i], rwkv_w0[i], rwkv_w2[i], rwkv_a0[i],
              rwkv_a2[i], rwkv_g2[i])
        yf, bonus, gate = _rwkv(za, *rw, ctx_len=ctx_len, rev=False)
        (yr,) = _rwkv(za, *rw, ctx_len=ctx_len, rev=True)
        wq_r, wk_r, wv_r = _arrange_mla_weights(mla_w_uq[i], mla_w_ukv[i])
        q, k, v = _mla_proj(zb, cos_f, sin_f, mla_q_norm_g[i], mla_kv_norm_g[i],
                            _pad_gain(mla_q_qknorm_g[i]), _pad_gain(mla_k_qknorm_g[i]),
                            wq_r, wk_r, wv_r, tm=tm)
        attn = _mla_attn(q, k, v, tq=tm, ctx_len=ctx_len, first_tile=first_tile)
        gb = _gate_bias(mlstm_i_b[i], mlstm_f_b[i])
        hf = _mlstm(zm, gb, ctx_len=ctx_len, rev=False)
        hr = _mlstm(zm, gb, ctx_len=ctx_len, rev=True)
        wo = w_out[i].astype(BF16)
        xu = _out_proj(xu, yf, yr, bonus, gate, attn, hf, hr, zm, modsel[i, :, :, 2:3],
                       rwkv_ln_g[i], rwkv_ln_b[i], mlstm_norm_g[i],
                       wo[:RWKV_DIM], wo[RWKV_DIM:RWKV_DIM + MLA_DIM], wo[RWKV_DIM + MLA_DIM:],
                       tm=tm, ctx_len=ctx_len, first_tile=first_tile)
        xu = _ffn(xu, modsel[i, :, :, 3:6], norm2_g[i], ffn_w_in[i].astype(BF16),
                  ffn_w_out[i].astype(BF16), tm=tm,
                  n_ctx_tiles_here=0 if last else n_ctx_tiles)
    return xu
```

```python
import functools

import jax
import jax.numpy as jnp
from jax import lax
from jax.experimental import pallas as pl
from jax.experimental.pallas import tpu as pltpu

F32 = jnp.float32
BF16 = jnp.bfloat16

HEAD_DIM = 64
NORM_EPS = 1e-6
GRID_W = 64
ROPE_BASE = 10000.0
RWKV_HEADS = 4
RWKV_DIM = RWKV_HEADS * HEAD_DIM
RWKV_W_LORA = 64
RWKV_A_LORA = 64
RWKV_G_LORA = 128
RWKV_COLS = 3 * RWKV_DIM + 2 * RWKV_W_LORA + 2 * RWKV_A_LORA + RWKV_G_LORA
RWKV_GN_EPS = 64e-5
MLA_HEADS = 8
MLA_NOPE = 64
MLA_ROPE = 32
MLA_V = HEAD_DIM
MLA_QK = MLA_NOPE + MLA_ROPE
MLA_Q_RANK = 512
MLA_KV_RANK = 256
MLA_DIM = MLA_HEADS * MLA_V
MLA_COLS = MLA_Q_RANK + MLA_KV_RANK + MLA_ROPE
MLSTM_HEADS = 4
MLSTM_QK = 32
MLSTM_V = HEAD_DIM
MLSTM_DIM = MLSTM_HEADS * MLSTM_V
MLSTM_QKW = 2 * MLSTM_HEADS * MLSTM_QK
GATE_SOFTCAP = 15.0
MLSTM_COLS = MLSTM_QKW + 2 * MLSTM_DIM + 4 * MLSTM_HEADS

LANES = 128
SUBLANES = 8
CHUNK = 64
QPAD = LANES
ZB_COLS = MLA_Q_RANK + MLA_KV_RANK + LANES
ZM_COLS = MLSTM_QKW + 2 * MLSTM_DIM + LANES
SHIFT_COLS = RWKV_COLS + MLSTM_QKW
IN_COLS_PAD = RWKV_COLS + ZM_COLS + ZB_COLS
VMEM_LIMIT = 56 * 1024 * 1024


def _split(a):
    hi = a.astype(BF16)
    lo = (a - hi.astype(F32)).astype(BF16)
    return hi, lo


_NN = (((1,), (0,)), ((), ()))
_NT = (((1,), (1,)), ((), ()))
_TN = (((0,), (0,)), ((), ()))


def _dg(a, b, dims):
    return lax.dot_general(a, b, dims, preferred_element_type=F32)


def _dot1(a, b, dims=_NN):
    return _dg(a.astype(BF16), b.astype(BF16), dims)


def _dot3(a, b, dims=_NN):
    ah, al = _split(a)
    bh, bl = _split(b)
    return _dg(ah, bh, dims) + (_dg(ah, bl, dims) + _dg(al, bh, dims))


def _dot_exact_rhs(a, b01, dims=_NN):
    a1 = a.astype(BF16)
    r1 = a - a1.astype(F32)
    a2 = r1.astype(BF16)
    a3 = (r1 - a2.astype(F32)).astype(BF16)
    return _dg(a1, b01, dims) + (_dg(a2, b01, dims) + _dg(a3, b01, dims))


def _dot_exact_lhs(a01, b, dims=_NN):
    b1 = b.astype(BF16)
    r1 = b - b1.astype(F32)
    b2 = r1.astype(BF16)
    b3 = (r1 - b2.astype(F32)).astype(BF16)
    return _dg(a01, b1, dims) + (_dg(a01, b2, dims) + _dg(a01, b3, dims))


def _iota(shape, dim):
    return lax.broadcasted_iota(jnp.int32, shape, dim)


def _seg_ones(n, seg):
    r = _iota((n, n), 0) // seg
    c = _iota((n, n), 1) // seg
    return (r == c).astype(BF16)


def _segsum(x, ones_bd):
    return _dot_exact_rhs(x, ones_bd)


def _transpose(x):
    n = x.shape[1]
    eye = (_iota((n, n), 0) == _iota((n, n), 1)).astype(BF16)
    return _dot_exact_lhs(eye, x, _NT)


def _softplus(x):
    return jnp.maximum(x, 0.0) + jnp.log(1.0 + jnp.exp(-jnp.abs(x)))


def _sigmoid(x):
    return 1.0 / (1.0 + jnp.exp(-x))


def _silu(x):
    return x * _sigmoid(x)


def _rmsnorm_mod(x, g, shift, scale):
    y = x * lax.rsqrt(jnp.mean(x * x, axis=-1, keepdims=True) + NORM_EPS)
    return (y * g) * (1.0 + scale) + shift


def _chunk_index(i, n_ctx_chunks, n_chunks, rev):
    if not rev:
        return i
    return jnp.where(i < n_ctx_chunks, n_ctx_chunks - 1 - i,
                     n_chunks - 1 - (i - n_ctx_chunks))


def _tri_masks(n, rev):
    r = _iota((n, n), 0)
    c = _iota((n, n), 1)
    if rev:
        return c > r, c >= r
    return c < r, c <= r


def _mod_kernel(c_ref, w_ref, b_ref, o_ref):
    o_ref[0] = _dot3(_silu(c_ref[...]), w_ref[0]) + b_ref[0]


def _modulation(cvec, mod_w, mod_b):
    depth, d, n = mod_w.shape
    tn = 1536
    rows = cvec.shape[0]
    return pl.pallas_call(
        _mod_kernel,
        grid=(depth, n // tn),
        in_specs=[
            pl.BlockSpec((rows, d), lambda l, j: (0, 0)),
            pl.BlockSpec((1, d, tn), lambda l, j: (l, 0, j)),
            pl.BlockSpec((1, 1, tn), lambda l, j: (l, 0, j)),
        ],
        out_specs=pl.BlockSpec((1, rows, tn), lambda l, j: (l, 0, j)),
        out_shape=jax.ShapeDtypeStruct((depth, rows, n), F32),
        compiler_params=pltpu.CompilerParams(
            dimension_semantics=("arbitrary", "arbitrary"), vmem_limit_bytes=VMEM_LIMIT),
        name="adaln_mod",
    )(cvec, mod_w, mod_b.reshape(depth, 1, n))


def _inproj_kernel(x_ref, xp_ref, xn_ref, mod_ref, g_ref, w_ref, mu_ref, cw_ref, cb_ref,
                   za_ref, zm_ref, zb_ref, *, tm, n_ctx_tiles, n_tiles):
    j = pl.program_id(1)
    g = g_ref[...]
    shift = mod_ref[0, 0, 0:1, :]
    scale = mod_ref[0, 0, 1:2, :]
    h = _rmsnorm_mod(x_ref[0], g, shift, scale).astype(BF16)
    z = _dg(h, w_ref[...], _NN)
    halo = jnp.concatenate([xp_ref[0], xn_ref[0]], axis=0)
    hh = _rmsnorm_mod(halo, g, shift, scale).astype(BF16)
    zh = _dg(hh, w_ref[:, :SHIFT_COLS], _NN)
    prev_ok = jnp.logical_and(j != 0, j != n_ctx_tiles).astype(F32)
    next_ok = jnp.logical_and(j != n_ctx_tiles - 1, j != n_tiles - 1).astype(F32)
    zs = z[:, :SHIFT_COLS]
    row = _iota((tm, SHIFT_COLS), 0)
    prev = jnp.where(row == 0, zh[SUBLANES - 1:SUBLANES, :] * prev_ok, pltpu.roll(zs, 1, 0))
    nxt = jnp.where(row == tm - 1, zh[SUBLANES:SUBLANES + 1, :] * next_ok,
                    pltpu.roll(zs, tm - 1, 0))
    za = zs[:, :RWKV_COLS]
    za_ref[0] = za + mu_ref[...] * (0.5 * (prev[:, :RWKV_COLS] + nxt[:, :RWKV_COLS]) - za)
    qk = (prev[:, RWKV_COLS:] * cw_ref[0:1, :] + zs[:, RWKV_COLS:] * cw_ref[1:2, :]
          + nxt[:, RWKV_COLS:] * cw_ref[2:3, :] + cb_ref[...])
    zm_ref[0, :, :MLSTM_QKW] = _silu(qk)
    zm_ref[0, :, MLSTM_QKW:] = z[:, SHIFT_COLS:RWKV_COLS + ZM_COLS]
    zb_ref[0] = z[:, RWKV_COLS + ZM_COLS:]


def _in_proj(x, modsel, norm_g, w_r, mu, conv_w, conv_b, *, tm, ctx_len):
    b, l, d = x.shape
    n_tiles = l // tm
    n_ctx_tiles = ctx_len // tm
    tm8 = tm // SUBLANES
    kern = functools.partial(_inproj_kernel, tm=tm, n_ctx_tiles=n_ctx_tiles, n_tiles=n_tiles)
    seg = lambda j: jnp.where(j >= n_ctx_tiles, 1, 0)
    return pl.pallas_call(
        kern,
        grid=(b, n_tiles),
        in_specs=[
            pl.BlockSpec((1, tm, d), lambda i, j: (i, j, 0)),
            pl.BlockSpec((1, SUBLANES, d), lambda i, j: (i, jnp.maximum(j * tm8 - 1, 0), 0)),
            pl.BlockSpec((1, SUBLANES, d),
                         lambda i, j: (i, jnp.minimum((j + 1) * tm8, l // SUBLANES - 1), 0)),
            pl.BlockSpec((1, 1, 2, d), lambda i, j: (i, seg(j), 0, 0)),
            pl.BlockSpec((1, d), lambda i, j: (0, 0)),
            pl.BlockSpec((d, IN_COLS_PAD), lambda i, j: (0, 0)),
            pl.BlockSpec((1, RWKV_COLS), lambda i, j: (0, 0)),
            pl.BlockSpec((3, MLSTM_QKW), lambda i, j: (0, 0)),
            pl.BlockSpec((1, MLSTM_QKW), lambda i, j: (0, 0)),
        ],
        out_specs=[
            pl.BlockSpec((1, tm, RWKV_COLS), lambda i, j: (i, j, 0)),
            pl.BlockSpec((1, tm, ZM_COLS), lambda i, j: (i, j, 0)),
            pl.BlockSpec((1, tm, ZB_COLS), lambda i, j: (i, j, 0)),
        ],
        out_shape=[
            jax.ShapeDtypeStruct((b, l, RWKV_COLS), F32),
            jax.ShapeDtypeStruct((b, l, ZM_COLS), F32),
            jax.ShapeDtypeStruct((b, l, ZB_COLS), F32),
        ],
        compiler_params=pltpu.CompilerParams(
            dimension_semantics=("arbitrary", "arbitrary"), vmem_limit_bytes=VMEM_LIMIT),
        name="in_proj",
    )(x, x, x, modsel, norm_g.reshape(1, d), w_r, mu.reshape(1, RWKV_COLS), conv_w,
      conv_b.reshape(1, MLSTM_QKW))


def _rwkv_local_kernel(za_ref, kk_ref, ka_ref, rk_ref, w0_ref, w2_ref, a0_ref, a2_ref, g2_ref,
                       g_ref, yl_ref, at_ref, bs_ref, bonus_ref, gate_ref):
    c = CHUNK
    nd = RWKV_DIM
    hd = HEAD_DIM
    za = za_ref[0]
    r = za[:, 0:nd]
    k = za[:, nd:2 * nd]
    v = za[:, 2 * nd:3 * nd]
    ones_bd = _seg_ones(nd, hd)
    kkr = k * kk_ref[...]
    kk = kkr / jnp.maximum(jnp.sqrt(_segsum(kkr * kkr, ones_bd)), 1e-12)
    bonus_ref[0] = _segsum(r * k * rk_ref[...], ones_bd) * v
    gd = za[:, 3 * nd + 2 * RWKV_W_LORA + 2 * RWKV_A_LORA:]
    gate_ref[0] = _dot1(_sigmoid(gd), g2_ref[...])

    eye = _iota((hd, hd), 0) == _iota((hd, hd), 1)
    eye_f = eye.astype(F32)
    row2 = _iota((2 * c, 2 * c), 0)
    rr = row2 % c
    cc = _iota((2 * c, 2 * c), 1) % c
    pr, qk, vh, pcol, rcol, qe, ke, gam, gmask = [], [], [], [], [], [], [], [], []
    diag_ok = jnp.logical_and(cc == rr, row2 >= c)
    for d in range(2):
        rev = d == 1
        wo = 3 * nd + d * RWKV_W_LORA
        ao = 3 * nd + 2 * RWKV_W_LORA + d * RWKV_A_LORA
        w_lo = _dot1(jnp.tanh(za[:, wo:wo + RWKV_W_LORA]), w2_ref[d])
        log_w = -_softplus(-(w0_ref[d] + w_lo)) - 0.5
        logdec = -jnp.exp(log_w)
        a = _sigmoid(a0_ref[d] + _dot1(za[:, ao:ao + RWKV_A_LORA], a2_ref[d]))
        kd = k * (1.0 + (a - 1.0) * ka_ref[...])
        bvec = kk * a
        _, incl = _tri_masks(c, rev)
        cum = _dot_exact_lhs(incl.astype(BF16), logdec)
        total = cum[0:1, :] if rev else cum[c - 1:c, :]
        e_neg = jnp.exp(-cum)
        e_end = jnp.exp(total - cum)
        pm = kk * jnp.exp(cum - logdec)
        qm = bvec * e_neg
        km = kd * e_neg
        rm = r * jnp.exp(cum)
        qe_d = (bvec * e_end).astype(BF16)
        ke_d = (kd * e_end).astype(BF16)
        gam_d = jnp.exp(total)
        gmask_d = jnp.logical_or(cc > rr if rev else cc < rr, diag_ok)
        for h in range(RWKV_HEADS):
            sl = slice(h * hd, (h + 1) * hd)
            pr.append(jnp.concatenate([pm[:, sl], rm[:, sl]], axis=0).astype(BF16))
            qk.append(jnp.concatenate([qm[:, sl], km[:, sl]], axis=0).astype(BF16))
            vh.append(v[:, sl].astype(BF16))
            pcol.append(pm[:, sl])
            rcol.append(rm[:, sl])
            qe.append(qe_d[:, sl])
            ke.append(ke_d[:, sl])
            gam.append(gam_d[:, sl])
            gmask.append(gmask_d)
    n = len(pr)
    a4 = [jnp.where(gmask[i], _dg(pr[i], qk[i], _NT), 0.0) for i in range(n)]
    a4b = [x.astype(BF16) for x in a4]
    lpq = [x[:c, :c] for x in a4b]
    pw = [_dg(x, x, _NN) for x in lpq]
    lv = [_dg(a4b[i][:, c:], vh[i], _NN) for i in range(n)]
    kv = [_dg(ke[i], vh[i], _TN) for i in range(n)]
    tinv = [eye_f - a4[i][:c, :c] for i in range(n)]
    covered = 2
    while True:
        tinv = [tinv[i] + _dot1(tinv[i], pw[i]) for i in range(n)]
        covered *= 2
        if covered >= c:
            break
        pw = [_dot1(x, x) for x in pw]
    wz = [_dot1(tinv[i], jnp.concatenate([pcol[i], lv[i][:c]], axis=1)).astype(BF16)
          for i in range(n)]
    awz = [_dg(a4b[i][c:, :c], wz[i], _NN) for i in range(n)]
    qwz = [_dg(qe[i], wz[i], _TN) for i in range(n)]
    for d in range(2):
        ids = range(d * RWKV_HEADS, (d + 1) * RWKV_HEADS)
        g_ref[d, 0] = jnp.concatenate([rcol[i] - awz[i][:, :hd] for i in ids], axis=1)
        yl_ref[d, 0] = jnp.concatenate([lv[i][c:] - awz[i][:, hd:] for i in ids], axis=1)
        at_ref[d, 0] = jnp.concatenate(
            [jnp.where(eye, jnp.broadcast_to(gam[i], (hd, hd)), 0.0) - qwz[i][:, :hd]
             for i in ids], axis=1)
        bs_ref[d, 0] = jnp.concatenate([kv[i] - qwz[i][:, hd:] for i in ids], axis=1)


def _rwkv_local(za, k_k, k_a, r_k, w0, w2, a0, a2, g2):
    b, l, _ = za.shape
    c = CHUNK
    nd = RWKV_DIM
    full = lambda shape: pl.BlockSpec(shape, lambda i, j: (0,) * len(shape))
    dir_spec = pl.BlockSpec((2, 1, c, nd), lambda i, j: (0, i, j, 0))
    dir_shape = jax.ShapeDtypeStruct((2, b, l, nd), F32)
    row_spec = pl.BlockSpec((1, c, nd), lambda i, j: (i, j, 0))
    row_shape = jax.ShapeDtypeStruct((b, l, nd), F32)
    return pl.pallas_call(
        _rwkv_local_kernel,
        grid=(b, l // c),
        in_specs=[
            pl.BlockSpec((1, c, RWKV_COLS), lambda i, j: (i, j, 0)),
            full((1, nd)), full((1, nd)), full((1, nd)),
            full((2, 1, nd)), full((2, RWKV_W_LORA, nd)),
            full((2, 1, nd)), full((2, RWKV_A_LORA, nd)),
            full((RWKV_G_LORA, nd)),
        ],
        out_specs=[dir_spec] * 4 + [row_spec] * 2,
        out_shape=[dir_shape] * 4 + [row_shape] * 2,
        compiler_params=pltpu.CompilerParams(
            dimension_semantics=("arbitrary", "arbitrary"), vmem_limit_bytes=VMEM_LIMIT),
        name="rwkv_local",
    )(za, k_k.reshape(1, -1), k_a.reshape(1, -1), r_k.reshape(1, -1), w0.reshape(2, 1, nd),
      w2, a0.reshape(2, 1, nd), a2, g2)


def _rwkv_seq_kernel(gf_ref, ylf_ref, atf_ref, bsf_ref, gr_ref, ylr_ref, atr_ref, bsr_ref,
                     yf_ref, yr_ref, m_ref):
    @pl.when(pl.program_id(1) == 0)
    def _():
        m_ref[...] = jnp.zeros_like(m_ref)

    dirs = ((gf_ref, ylf_ref, atf_ref, bsf_ref, yf_ref), (gr_ref, ylr_ref, atr_ref, bsr_ref, yr_ref))
    for d, (g_ref, yl_ref, at_ref, bs_ref, y_ref) in enumerate(dirs):
        g = g_ref[0, 0]
        at = at_ref[0, 0]
        ys = []
        for h in range(RWKV_HEADS):
            sl = slice(h * HEAD_DIM, (h + 1) * HEAD_DIM)
            i = d * RWKV_HEADS + h
            m0 = m_ref[i].astype(BF16)
            ys.append(_dg(g[:, sl].astype(BF16), m0, _NN))
            m_ref[i] = _dg(at[:, sl].astype(BF16), m0, _NN) + bs_ref[0, 0, :, sl]
        y_ref[0] = jnp.concatenate(ys, axis=1) + yl_ref[0, 0]


def _rwkv_seq(g, yl, at, bs, *, ctx_len):
    _, b, l, nd = g.shape
    c = CHUNK
    n_chunks = l // c
    n_ctx_chunks = ctx_len // c
    rev_idx = lambda j: _chunk_index(j, n_ctx_chunks, n_chunks, True)
    fwd = pl.BlockSpec((1, 1, c, nd), lambda i, j: (0, i, j, 0))
    bwd = pl.BlockSpec((1, 1, c, nd), lambda i, j: (1, i, rev_idx(j), 0))
    y_shape = jax.ShapeDtypeStruct((b, l, nd), F32)
    return pl.pallas_call(
        _rwkv_seq_kernel,
        grid=(b, n_chunks),
        in_specs=[fwd] * 4 + [bwd] * 4,
        out_specs=[pl.BlockSpec((1, c, nd), lambda i, j: (i, j, 0)),
                   pl.BlockSpec((1, c, nd), lambda i, j: (i, rev_idx(j), 0))],
        out_shape=[y_shape, y_shape],
        scratch_shapes=[pltpu.VMEM((2 * RWKV_HEADS, HEAD_DIM, HEAD_DIM), F32)],
        compiler_params=pltpu.CompilerParams(
            dimension_semantics=("arbitrary", "arbitrary"), vmem_limit_bytes=VMEM_LIMIT),
        name="rwkv_seq",
    )(g, yl, at, bs, g, yl, at, bs)


def _mlstm_kernel(zf_ref, zr_ref, gb_ref, hf_ref, hr_ref, c_ref, n_ref, m_ref):
    c = CHUNK
    dk = MLSTM_QK
    dv = MLSTM_V

    @pl.when(pl.program_id(1) == 0)
    def _():
        c_ref[...] = jnp.zeros_like(c_ref)
        n_ref[...] = jnp.zeros_like(n_ref)
        m_ref[...] = jnp.zeros_like(m_ref)

    q, qb, kb, ke, vb, li_row, b_col, b_row, e_row, g, m_loc, incl = ([] for _ in range(12))
    for d, zm_ref in enumerate((zf_ref, zr_ref)):
        rev = d == 1
        zm = zm_ref[0]
        gates = zm[:, MLSTM_QKW + 2 * MLSTM_DIM:] + gb_ref[...]
        capped = GATE_SOFTCAP * jnp.tanh(gates * (1.0 / GATE_SOFTCAP))
        log_f_all = -_softplus(-capped)
        _, incl_d = _tri_masks(c, rev)
        b_all = _dot_exact_lhs(incl_d.astype(BF16), log_f_all)
        li_t = _transpose(capped)
        b_t = _transpose(b_all)
        last = 0 if rev else c - 1
        for h in range(MLSTM_HEADS):
            ci = d * 2 * MLSTM_HEADS + h
            cf = ci + MLSTM_HEADS
            q_h = zm[:, h * dk:(h + 1) * dk] * (dk ** -0.5)
            k_h = zm[:, MLSTM_HEADS * dk + h * dk:MLSTM_HEADS * dk + (h + 1) * dk]
            bc = b_all[:, cf:cf + 1]
            br = b_t[cf:cf + 1, :]
            lr = li_t[ci:ci + 1, :]
            g_h = bc[last:last + 1, :]
            w_end_row = g_h - br + lr
            ml = jnp.max(w_end_row, axis=1, keepdims=True)
            e_col = jnp.exp(g_h - bc + capped[:, ci:ci + 1] - ml)
            q.append(q_h)
            qb.append(q_h.astype(BF16))
            kb.append(k_h.astype(BF16))
            ke.append((k_h * e_col).astype(BF16))
            vb.append(zm[:, MLSTM_QKW + h * dv:MLSTM_QKW + (h + 1) * dv].astype(BF16))
            li_row.append(lr)
            b_col.append(bc)
            b_row.append(br)
            e_row.append(jnp.exp(w_end_row - ml))
            g.append(g_h)
            m_loc.append(ml)
            incl.append(incl_d)
    n = len(q)
    m_s = [m_ref[i] for i in range(n)]
    c_s = [c_ref[i] for i in range(n)]
    n_s = [n_ref[i] for i in range(n)]
    qk = [_dg(qb[i], kb[i], _NT) for i in range(n)]
    qc = [_dg(qb[i], c_s[i].astype(BF16), _NN) for i in range(n)]
    c_loc = [_dg(ke[i], vb[i], _TN) for i in range(n)]
    n_loc = [_dg(jnp.broadcast_to(e_row[i], (SUBLANES, c)).astype(BF16), kb[i], _NN)[0:1, :]
             for i in range(n)]
    log_inter = [b_col[i] + m_s[i] for i in range(n)]
    log_d = [jnp.where(incl[i], b_col[i] - b_row[i] + li_row[i], -jnp.inf) for i in range(n)]
    m_out = [jnp.maximum(log_inter[i], jnp.max(log_d[i], axis=1, keepdims=True))
             for i in range(n)]
    w_intra = [qk[i] * jnp.exp(log_d[i] - m_out[i]) for i in range(n)]
    wv = [_dg(w_intra[i].astype(BF16), vb[i], _NN) for i in range(n)]
    hs = []
    for i in range(n):
        s_inter = jnp.exp(log_inter[i] - m_out[i])
        num = wv[i] + s_inter * qc[i]
        den = (jnp.sum(w_intra[i], axis=1, keepdims=True)
               + s_inter * jnp.sum(q[i] * n_s[i], axis=1, keepdims=True))
        hs.append(num / jnp.maximum(jnp.abs(den), jnp.exp(-m_out[i])))
        m_new = jnp.maximum(g[i] + m_s[i], m_loc[i])
        s_old = jnp.exp(g[i] + m_s[i] - m_new)
        s_loc = jnp.exp(m_loc[i] - m_new)
        c_ref[i] = s_old * c_s[i] + s_loc * c_loc[i]
        n_ref[i] = s_old * n_s[i] + s_loc * n_loc[i]
        m_ref[i] = m_new
    hf_ref[0] = jnp.concatenate(hs[:MLSTM_HEADS], axis=1)
    hr_ref[0] = jnp.concatenate(hs[MLSTM_HEADS:], axis=1)


def _mlstm(zm, gate_bias, *, ctx_len):
    b, l, _ = zm.shape
    c = CHUNK
    n_chunks = l // c
    n_ctx_chunks = ctx_len // c
    fwd = lambda i, j: (i, j, 0)
    bwd = lambda i, j: (i, _chunk_index(j, n_ctx_chunks, n_chunks, True), 0)
    h_shape = jax.ShapeDtypeStruct((b, l, MLSTM_DIM), F32)
    n_state = 2 * MLSTM_HEADS
    return pl.pallas_call(
        _mlstm_kernel,
        grid=(b, n_chunks),
        in_specs=[
            pl.BlockSpec((1, c, ZM_COLS), fwd),
            pl.BlockSpec((1, c, ZM_COLS), bwd),
            pl.BlockSpec((1, LANES), lambda i, j: (0, 0)),
        ],
        out_specs=[pl.BlockSpec((1, c, MLSTM_DIM), fwd), pl.BlockSpec((1, c, MLSTM_DIM), bwd)],
        out_shape=[h_shape, h_shape],
        scratch_shapes=[
            pltpu.VMEM((n_state, MLSTM_QK, MLSTM_V), F32),
            pltpu.VMEM((n_state, 1, MLSTM_QK), F32),
            pltpu.VMEM((n_state, 1, 1), F32),
        ],
        compiler_params=pltpu.CompilerParams(
            dimension_semantics=("arbitrary", "arbitrary"), vmem_limit_bytes=VMEM_LIMIT),
        name="mlstm",
    )(zm, zm, gate_bias)


def _rope_rot(x):
    lane = _iota(x.shape, 1)
    half = MLA_ROPE // 2
    return jnp.where(lane < MLA_NOPE + half, -pltpu.roll(x, LANES - half, 1),
                     pltpu.roll(x, half, 1))


def _mla_proj_kernel(zb_ref, cos_ref, sin_ref, qg_ref, kvg_ref, gq_ref, gk_ref,
                     wq_ref, wk_ref, wv_ref, q_ref, k_ref, v_ref):
    zb = zb_ref[0]
    cq = zb[:, :MLA_Q_RANK]
    ckv = zb[:, MLA_Q_RANK:MLA_Q_RANK + MLA_KV_RANK]
    kr = zb[:, MLA_Q_RANK + MLA_KV_RANK:]
    cos = cos_ref[...]
    sin = sin_ref[...]
    cqn = cq * lax.rsqrt(jnp.mean(cq * cq, axis=-1, keepdims=True) + NORM_EPS) * qg_ref[...]
    ckn = ckv * lax.rsqrt(jnp.mean(ckv * ckv, axis=-1, keepdims=True) + NORM_EPS) * kvg_ref[...]
    q_all = _dot1(cqn, wq_ref[...])
    k_all = _dot1(ckn, wk_ref[...])
    v_ref[0] = _dot1(ckn, wv_ref[...]).astype(BF16)
    scale = MLA_QK ** -0.5
    for h in range(MLA_HEADS):
        sl = slice(h * QPAD, (h + 1) * QPAD)
        qh = q_all[:, sl]
        qn = qh * lax.rsqrt(jnp.sum(qh * qh, axis=-1, keepdims=True) * (1.0 / MLA_QK) + NORM_EPS)
        qn = qn * gq_ref[...]
        q_ref[0, :, sl] = ((qn * cos + _rope_rot(qn) * sin) * scale).astype(BF16)
        kh = k_all[:, sl] + kr
        kn = kh * lax.rsqrt(jnp.sum(kh * kh, axis=-1, keepdims=True) * (1.0 / MLA_QK) + NORM_EPS)
        kn = kn * gk_ref[...]
        k_ref[0, :, sl] = (kn * cos + _rope_rot(kn) * sin).astype(BF16)


def _mla_proj(zb, cos_f, sin_f, q_norm_g, kv_norm_g, gq, gk, wq_r, wk_r, wv_r, *, tm):
    b, l, _ = zb.shape
    hq = MLA_HEADS * QPAD
    full = lambda shape: pl.BlockSpec(shape, lambda i, j: (0,) * len(shape))
    return pl.pallas_call(
        _mla_proj_kernel,
        grid=(b, l // tm),
        in_specs=[
            pl.BlockSpec((1, tm, ZB_COLS), lambda i, j: (i, j, 0)),
            pl.BlockSpec((tm, QPAD), lambda i, j: (j, 0)),
            pl.BlockSpec((tm, QPAD), lambda i, j: (j, 0)),
            full((1, MLA_Q_RANK)), full((1, MLA_KV_RANK)), full((1, QPAD)), full((1, QPAD)),
            full((MLA_Q_RANK, hq)), full((MLA_KV_RANK, hq)), full((MLA_KV_RANK, MLA_DIM)),
        ],
        out_specs=[
            pl.BlockSpec((1, tm, hq), lambda i, j: (i, j, 0)),
            pl.BlockSpec((1, tm, hq), lambda i, j: (i, j, 0)),
            pl.BlockSpec((1, tm, MLA_DIM), lambda i, j: (i, j, 0)),
        ],
        out_shape=[
            jax.ShapeDtypeStruct((b, l, hq), BF16),
            jax.ShapeDtypeStruct((b, l, hq), BF16),
            jax.ShapeDtypeStruct((b, l, MLA_DIM), BF16),
        ],
        compiler_params=pltpu.CompilerParams(
            dimension_semantics=("arbitrary", "arbitrary"), vmem_limit_bytes=VMEM_LIMIT),
        name="mla_proj",
    )(zb, cos_f, sin_f, q_norm_g.reshape(1, -1), kv_norm_g.reshape(1, -1), gq, gk,
      wq_r, wk_r, wv_r)


def _mla_attn_kernel(q_ref, k_ref, v_ref, o_ref, *, ctx_len, first_tile, n_ctx_tiles):
    tq = q_ref.shape[1]
    l = k_ref.shape[1]
    is_ctx = (pl.program_id(2) + first_tile) < n_ctx_tiles
    col = _iota((tq, l), 1)
    visible = jnp.logical_or(col < ctx_len, jnp.logical_not(is_ctx))
    lane = _iota((l, 2 * MLA_V), 1)
    v = v_ref[0]
    acc = jnp.zeros((tq, 2 * MLA_V), F32)
    for h in range(2):
        sl = slice(h * QPAD, (h + 1) * QPAD)
        s = _dg(q_ref[0, :, sl], k_ref[0, :, sl], _NT)
        s = jnp.where(visible, s, -jnp.inf)
        m = jnp.max(s, axis=-1, keepdims=True)
        p = jnp.exp(s - m)
        denom = jnp.sum(p, axis=-1, keepdims=True)
        vh = jnp.where((lane // MLA_V) == h, v, jnp.zeros_like(v))
        acc = acc + _dg(p.astype(BF16), vh, _NN) / denom
    o_ref[0] = acc


def _mla_attn(q, k, v, *, tq, ctx_len, first_tile):
    b, l, _ = q.shape
    n_q = l // tq - first_tile
    pairs = MLA_HEADS // 2
    kern = functools.partial(_mla_attn_kernel, ctx_len=ctx_len, first_tile=first_tile,
                             n_ctx_tiles=ctx_len // tq)
    return pl.pallas_call(
        kern,
        grid=(b, pairs, n_q),
        in_specs=[
            pl.BlockSpec((1, tq, 2 * QPAD), lambda i, p, j: (i, j + first_tile, p)),
            pl.BlockSpec((1, l, 2 * QPAD), lambda i, p, j: (i, 0, p)),
            pl.BlockSpec((1, l, 2 * MLA_V), lambda i, p, j: (i, 0, p)),
        ],
        out_specs=pl.BlockSpec((1, tq, 2 * MLA_V), lambda i, p, j: (i, j + first_tile, p)),
        out_shape=jax.ShapeDtypeStruct((b, l, MLA_DIM), F32),
        compiler_params=pltpu.CompilerParams(
            dimension_semantics=("arbitrary", "arbitrary", "arbitrary"),
            vmem_limit_bytes=VMEM_LIMIT),
        name="mla_attn",
    )(q, k, v)


def _outproj_kernel(x_ref, yf_ref, yr_ref, bonus_ref, gate_ref, attn_ref, hf_ref, hr_ref,
                    o_ref, mod_ref, lng_ref, lnb_ref, mng_ref, wa_ref, wb_ref, wm_ref, out_ref):
    ones_bd = _seg_ones(RWKV_DIM, HEAD_DIM)
    inv = 1.0 / HEAD_DIM
    y = yf_ref[0] + yr_ref[0]
    mean = _segsum(y, ones_bd) * inv
    yc = y - mean
    var = _segsum(yc * yc, ones_bd) * inv
    yn = yc * lax.rsqrt(var + RWKV_GN_EPS) * lng_ref[...] + lnb_ref[...]
    ya = (yn + bonus_ref[0]) * gate_ref[0]
    hm = hf_ref[0] + hr_ref[0]
    hn = hm * lax.rsqrt(_segsum(hm * hm, ones_bd) * inv + NORM_EPS)
    ym = hn * mng_ref[...] * _sigmoid(o_ref[0])
    mix = (_dot1(ya, wa_ref[...]) + _dot1(attn_ref[0], wb_ref[...]) + _dot1(ym, wm_ref[...]))
    out_ref[0] = x_ref[0] + mod_ref[0, 0] * mix


def _out_proj(x, yf, yr, bonus, gate, attn, hf, hr, zm, modsel, ln_g, ln_b, mnorm_g,
              wa, wb, wm, *, tm, ctx_len, first_tile):
    b, l, d = x.shape
    n_ctx_tiles = ctx_len // tm
    n_tiles = l // tm - first_tile
    row = lambda w: pl.BlockSpec((1, tm, w), lambda i, j: (i, j + first_tile, 0))
    full = lambda shape: pl.BlockSpec(shape, lambda i, j: (0,) * len(shape))
    seg = lambda j: jnp.where(j + first_tile >= n_ctx_tiles, 1, 0)
    o_block = (MLSTM_QKW + MLSTM_DIM) // MLSTM_DIM
    return pl.pallas_call(
        _outproj_kernel,
        grid=(b, n_tiles),
        in_specs=[
            row(d), row(RWKV_DIM), row(RWKV_DIM), row(RWKV_DIM), row(RWKV_DIM), row(MLA_DIM),
            row(MLSTM_DIM), row(MLSTM_DIM),
            pl.BlockSpec((1, tm, MLSTM_DIM), lambda i, j: (i, j + first_tile, o_block)),
            pl.BlockSpec((1, 1, 1, d), lambda i, j: (i, seg(j), 0, 0)),
            full((1, RWKV_DIM)), full((1, RWKV_DIM)), full((1, MLSTM_DIM)),
            full((RWKV_DIM, d)), full((MLA_DIM, d)), full((MLSTM_DIM, d)),
        ],
        out_specs=pl.BlockSpec((1, tm, d), lambda i, j: (i, j, 0)),
        out_shape=jax.ShapeDtypeStruct((b, n_tiles * tm, d), F32),
        compiler_params=pltpu.CompilerParams(
            dimension_semantics=("arbitrary", "arbitrary"), vmem_limit_bytes=VMEM_LIMIT),
        name="out_proj",
    )(x, yf, yr, bonus, gate, attn, hf, hr, zm, modsel, ln_g.reshape(1, -1),
      ln_b.reshape(1, -1), mnorm_g.reshape(1, -1), wa, wb, wm)


def _ffn_kernel(x_ref, mod_ref, g_ref, w1_ref, w2_ref, out_ref, *, hidden):
    x = x_ref[0]
    h = _rmsnorm_mod(x, g_ref[...], mod_ref[0, 0, 0:1, :], mod_ref[0, 0, 1:2, :]).astype(BF16)
    gu = _dg(h, w1_ref[...], _NN)
    act = (_silu(gu[:, :hidden]) * gu[:, hidden:]).astype(BF16)
    out_ref[0] = x + mod_ref[0, 0, 2:3, :] * _dg(act, w2_ref[...], _NN)


def _ffn(x, modsel, norm_g, w1, w2, *, tm, n_ctx_tiles_here):
    b, l, d = x.shape
    hidden = w2.shape[0]
    seg = lambda j: jnp.where(j >= n_ctx_tiles_here, 1, 0)
    return pl.pallas_call(
        functools.partial(_ffn_kernel, hidden=hidden),
        grid=(b, l // tm),
        in_specs=[
            pl.BlockSpec((1, tm, d), lambda i, j: (i, j, 0)),
            pl.BlockSpec((1, 1, 3, d), lambda i, j: (i, seg(j), 0, 0)),
            pl.BlockSpec((1, d), lambda i, j: (0, 0)),
            pl.BlockSpec((d, 2 * hidden), lambda i, j: (0, 0)),
            pl.BlockSpec((hidden, d), lambda i, j: (0, 0)),
        ],
        out_specs=pl.BlockSpec((1, tm, d), lambda i, j: (i, j, 0)),
        out_shape=jax.ShapeDtypeStruct((b, l, d), F32),
        compiler_params=pltpu.CompilerParams(
            dimension_semantics=("arbitrary", "arbitrary"), vmem_limit_bytes=VMEM_LIMIT),
        name="ffn",
    )(x, modsel, norm_g.reshape(1, d), w1, w2)


def _arrange_w_in(w_in):
    d = w_in.shape[0]
    wa = w_in[:, :RWKV_COLS]
    wb = w_in[:, RWKV_COLS:RWKV_COLS + MLA_COLS]
    wm = w_in[:, RWKV_COLS + MLA_COLS:]
    gates = wm[:, MLSTM_QKW + 2 * MLSTM_DIM:]
    gate_blk = jnp.pad(gates, ((0, 0), (0, LANES - gates.shape[1])))
    kr = wb[:, MLA_Q_RANK + MLA_KV_RANK:]
    kr_blk = jnp.pad(kr, ((0, 0), (MLA_NOPE, LANES - MLA_NOPE - MLA_ROPE)))
    out = jnp.concatenate([wa, wm[:, :MLSTM_QKW + 2 * MLSTM_DIM], gate_blk,
                           wb[:, :MLA_Q_RANK + MLA_KV_RANK], kr_blk], axis=1)
    assert out.shape == (d, IN_COLS_PAD)
    return out.astype(BF16)


def _arrange_mla_weights(w_uq, w_ukv):
    rq = w_uq.shape[0]
    wq = w_uq.reshape(rq, MLA_HEADS, MLA_QK)
    wq = jnp.pad(wq, ((0, 0), (0, 0), (0, QPAD - MLA_QK))).reshape(rq, MLA_HEADS * QPAD)
    rk = w_ukv.shape[0]
    wkv = w_ukv.reshape(rk, MLA_HEADS, MLA_NOPE + MLA_V)
    wk = jnp.pad(wkv[:, :, :MLA_NOPE], ((0, 0), (0, 0), (0, QPAD - MLA_NOPE)))
    wk = wk.reshape(rk, MLA_HEADS * QPAD)
    wv = wkv[:, :, MLA_NOPE:].reshape(rk, MLA_DIM)
    return wq.astype(BF16), wk.astype(BF16), wv.astype(BF16)


def _rope_tables(seq_len, ctx_len):
    rows = seq_len // GRID_W
    row = jnp.repeat(jnp.arange(rows, dtype=F32), GRID_W)
    col = jnp.tile(jnp.arange(GRID_W, dtype=F32), rows)
    n_freq = MLA_ROPE // 4
    inv = jnp.power(ROPE_BASE, -jnp.arange(n_freq, dtype=F32) / n_freq)
    ang = jnp.concatenate([row[:, None] * inv, col[:, None] * inv], axis=-1)
    cos, sin = jnp.cos(ang), jnp.sin(ang)
    pad_l, pad_r = MLA_NOPE, QPAD - MLA_QK
    cos_f = jnp.concatenate([jnp.ones((seq_len, pad_l), F32), cos, cos,
                             jnp.ones((seq_len, pad_r), F32)], axis=1)
    sin_f = jnp.concatenate([jnp.zeros((seq_len, pad_l), F32), sin, sin,
                             jnp.zeros((seq_len, pad_r), F32)], axis=1)
    cos_f = jnp.concatenate([jnp.ones((ctx_len, QPAD), F32), cos_f], axis=0)
    sin_f = jnp.concatenate([jnp.zeros((ctx_len, QPAD), F32), sin_f], axis=0)
    return cos_f, sin_f


def _pad_gain(g):
    return jnp.pad(g, (0, QPAD - g.shape[0])).reshape(1, QPAD)


def _gate_bias(i_b, f_b):
    gb = jnp.stack([i_b, f_b], axis=1).reshape(-1)
    return jnp.pad(gb, (0, LANES - gb.shape[0])).reshape(1, LANES)


def kernel(x, c, ctx, c_ctx, mod_w, mod_b, norm1_g, norm2_g, w_in, w_out, ffn_w_in, ffn_w_out, rwkv_mu, rwkv_w0, rwkv_w2, rwkv_a0, rwkv_a2, rwkv_g2, rwkv_k_k, rwkv_k_a, rwkv_r_k, rwkv_ln_g, rwkv_ln_b, mla_q_norm_g, mla_w_uq, mla_kv_norm_g, mla_w_ukv, mla_q_qknorm_g, mla_k_qknorm_g, mlstm_conv_w, mlstm_conv_b, mlstm_i_b, mlstm_f_b, mlstm_norm_g):
    bsz, seq, d = x.shape
    ctx_len = ctx.shape[1]
    depth = mod_w.shape[0]
    tm = min(256, ctx_len)
    assert ctx_len % tm == 0 and seq % tm == 0 and ctx_len % CHUNK == 0 and seq % CHUNK == 0

    xu = jnp.concatenate([ctx, x], axis=1)
    n_ctx_tiles = ctx_len // tm

    rows = -(-(bsz + 1) // SUBLANES) * SUBLANES
    cvec = jnp.zeros((rows, d), F32).at[:bsz].set(c).at[bsz].set(c_ctx)
    mod = _modulation(cvec, mod_w, mod_b).reshape(depth, rows, 6, d)
    mod_lat = mod[:, :bsz]
    mod_ctx = jnp.broadcast_to(mod[:, bsz][:, None], mod_lat.shape)
    modsel = jnp.stack([mod_ctx, mod_lat], axis=2)

    cos_f, sin_f = _rope_tables(seq, ctx_len)

    for i in range(depth):
        last = i == depth - 1
        first_tile = n_ctx_tiles if last else 0
        w_r = _arrange_w_in(w_in[i])
        za, zm, zb = _in_proj(xu, modsel[i, :, :, 0:2], norm1_g[i], w_r, rwkv_mu[i],
                              mlstm_conv_w[i], mlstm_conv_b[i], tm=tm, ctx_len=ctx_len)
        g_loc, y_loc, a_tr, b_st, bonus, gate = _rwkv_local(
            za, rwkv_k_k[i], rwkv_k_a[i], rwkv_r_k[i], rwkv_w0[i], rwkv_w2[i], rwkv_a0[i],
            rwkv_a2[i], rwkv_g2[i])
        yf, yr = _rwkv_seq(g_loc, y_loc, a_tr, b_st, ctx_len=ctx_len)
        wq_r, wk_r, wv_r = _arrange_mla_weights(mla_w_uq[i], mla_w_ukv[i])
        q, k, v = _mla_proj(zb, cos_f, sin_f, mla_q_norm_g[i], mla_kv_norm_g[i],
                            _pad_gain(mla_q_qknorm_g[i]), _pad_gain(mla_k_qknorm_g[i]),
                            wq_r, wk_r, wv_r, tm=tm)
        attn = _mla_attn(q, k, v, tq=tm, ctx_len=ctx_len, first_tile=first_tile)
        gb = _gate_bias(mlstm_i_b[i], mlstm_f_b[i])
        hf, hr = _mlstm(zm, gb, ctx_len=ctx_len)
        wo = w_out[i].astype(BF16)
        xu = _out_proj(xu, yf, yr, bonus, gate, attn, hf, hr, zm, modsel[i, :, :, 2:3],
                       rwkv_ln_g[i], rwkv_ln_b[i], mlstm_norm_g[i],
                       wo[:RWKV_DIM], wo[RWKV_DIM:RWKV_DIM + MLA_DIM], wo[RWKV_DIM + MLA_DIM:],
                       tm=tm, ctx_len=ctx_len, first_tile=first_tile)
        xu = _ffn(xu, modsel[i, :, :, 3:6], norm2_g[i], ffn_w_in[i].astype(BF16),
                  ffn_w_out[i].astype(BF16), tm=tm,
                  n_ctx_tiles_here=0 if last else n_ctx_tiles)
    return xu
```

```python
import functools

import jax
import jax.numpy as jnp
from jax import lax
from jax.experimental import pallas as pl
from jax.experimental.pallas import tpu as pltpu

F32 = jnp.float32
BF16 = jnp.bfloat16

HEAD_DIM = 64
NORM_EPS = 1e-6
GRID_W = 64
ROPE_BASE = 10000.0
RWKV_HEADS = 4
RWKV_DIM = RWKV_HEADS * HEAD_DIM
RWKV_W_LORA = 64
RWKV_A_LORA = 64
RWKV_G_LORA = 128
RWKV_COLS = 3 * RWKV_DIM + 2 * RWKV_W_LORA + 2 * RWKV_A_LORA + RWKV_G_LORA
RWKV_GN_EPS = 64e-5
MLA_HEADS = 8
MLA_NOPE = 64
MLA_ROPE = 32
MLA_V = HEAD_DIM
MLA_QK = MLA_NOPE + MLA_ROPE
MLA_Q_RANK = 512
MLA_KV_RANK = 256
MLA_DIM = MLA_HEADS * MLA_V
MLA_COLS = MLA_Q_RANK + MLA_KV_RANK + MLA_ROPE
MLSTM_HEADS = 4
MLSTM_QK = 32
MLSTM_V = HEAD_DIM
MLSTM_DIM = MLSTM_HEADS * MLSTM_V
MLSTM_QKW = 2 * MLSTM_HEADS * MLSTM_QK
GATE_SOFTCAP = 15.0
MLSTM_COLS = MLSTM_QKW + 2 * MLSTM_DIM + 4 * MLSTM_HEADS

LANES = 128
SUBLANES = 8
CHUNK = HEAD_DIM
ROWS_PER_STEP = 4
KV_BLOCK = 256
QPAD = LANES
ZB_COLS = MLA_Q_RANK + MLA_KV_RANK + LANES
ZM_COLS = MLSTM_QKW + 2 * MLSTM_DIM + LANES
SHIFT_COLS = RWKV_COLS + MLSTM_QKW
IN_COLS_PAD = RWKV_COLS + ZM_COLS + ZB_COLS
VMEM_LIMIT = 56 * 1024 * 1024


def _split(a):
    hi = a.astype(BF16)
    lo = (a - hi.astype(F32)).astype(BF16)
    return hi, lo


_NN = (((1,), (0,)), ((), ()))
_NT = (((1,), (1,)), ((), ()))
_TN = (((0,), (0,)), ((), ()))


def _dg(a, b, dims):
    return lax.dot_general(a, b, dims, preferred_element_type=F32)


def _dot1(a, b, dims=_NN):
    return _dg(a.astype(BF16), b.astype(BF16), dims)


def _dot3(a, b, dims=_NN):
    ah, al = _split(a)
    bh, bl = _split(b)
    return _dg(ah, bh, dims) + (_dg(ah, bl, dims) + _dg(al, bh, dims))


def _dot_exact_rhs(a, b01, dims=_NN):
    a1 = a.astype(BF16)
    r1 = a - a1.astype(F32)
    a2 = r1.astype(BF16)
    a3 = (r1 - a2.astype(F32)).astype(BF16)
    return _dg(a1, b01, dims) + (_dg(a2, b01, dims) + _dg(a3, b01, dims))


def _dot_exact_lhs(a01, b, dims=_NN):
    b1 = b.astype(BF16)
    r1 = b - b1.astype(F32)
    b2 = r1.astype(BF16)
    b3 = (r1 - b2.astype(F32)).astype(BF16)
    return _dg(a01, b1, dims) + (_dg(a01, b2, dims) + _dg(a01, b3, dims))


def _dot2_rhs01(a, b01):
    ah, al = _split(a)
    return _dg(ah, b01, _NN) + _dg(al, b01, _NN)


def _iota(shape, dim):
    return lax.broadcasted_iota(jnp.int32, shape, dim)


def _seg_ones(n, seg):
    r = _iota((n, n), 0) // seg
    c = _iota((n, n), 1) // seg
    return (r == c).astype(BF16)


def _segsum(x, ones_bd):
    return _dot_exact_rhs(x, ones_bd)


def _softplus(x):
    return jnp.maximum(x, 0.0) + jnp.log(1.0 + jnp.exp(-jnp.abs(x)))


def _sigmoid(x):
    return 1.0 / (1.0 + jnp.exp(-x))


def _silu(x):
    return x * _sigmoid(x)


def _rmsnorm_mod(x, g, shift, scale):
    y = x * lax.rsqrt(jnp.mean(x * x, axis=-1, keepdims=True) + NORM_EPS)
    return (y * g) * (1.0 + scale) + shift


def _chunk_index(i, n_ctx_chunks, n_chunks, rev):
    if not rev:
        return i
    return jnp.where(i < n_ctx_chunks, n_ctx_chunks - 1 - i,
                     n_chunks - 1 - (i - n_ctx_chunks))


def _incl_mask(n, rev):
    r = _iota((n, n), 0)
    c = _iota((n, n), 1)
    return (c >= r) if rev else (c <= r)


def _mod_kernel(c_ref, w_ref, b_ref, o_ref):
    o_ref[0] = _dot3(_silu(c_ref[...]), w_ref[0]) + b_ref[0]


def _modulation(cvec, mod_w, mod_b):
    depth, d, n = mod_w.shape
    tn = 1536
    rows = cvec.shape[0]
    return pl.pallas_call(
        _mod_kernel,
        grid=(depth, n // tn),
        in_specs=[
            pl.BlockSpec((rows, d), lambda l, j: (0, 0)),
            pl.BlockSpec((1, d, tn), lambda l, j: (l, 0, j)),
            pl.BlockSpec((1, 1, tn), lambda l, j: (l, 0, j)),
        ],
        out_specs=pl.BlockSpec((1, rows, tn), lambda l, j: (l, 0, j)),
        out_shape=jax.ShapeDtypeStruct((depth, rows, n), F32),
        compiler_params=pltpu.CompilerParams(
            dimension_semantics=("arbitrary", "arbitrary"), vmem_limit_bytes=VMEM_LIMIT),
        name="adaln_mod",
    )(cvec, mod_w, mod_b.reshape(depth, 1, n))


def _inproj_kernel(x_ref, xp_ref, xn_ref, mod_ref, g_ref, w_ref, mu_ref, cw_ref, cb_ref,
                   za_ref, zm_ref, zb_ref, *, tm, n_ctx_tiles, n_tiles):
    j = pl.program_id(1)
    g = g_ref[...]
    shift = mod_ref[0, 0, 0:1, :]
    scale = mod_ref[0, 0, 1:2, :]
    h = _rmsnorm_mod(x_ref[0], g, shift, scale).astype(BF16)
    z = _dg(h, w_ref[...], _NN)
    halo = jnp.concatenate([xp_ref[0], xn_ref[0]], axis=0)
    hh = _rmsnorm_mod(halo, g, shift, scale).astype(BF16)
    zh = _dg(hh, w_ref[:, :SHIFT_COLS], _NN)
    prev_ok = jnp.logical_and(j != 0, j != n_ctx_tiles).astype(F32)
    next_ok = jnp.logical_and(j != n_ctx_tiles - 1, j != n_tiles - 1).astype(F32)
    zs = z[:, :SHIFT_COLS]
    row = _iota((tm, SHIFT_COLS), 0)
    prev = jnp.where(row == 0, zh[SUBLANES - 1:SUBLANES, :] * prev_ok, pltpu.roll(zs, 1, 0))
    nxt = jnp.where(row == tm - 1, zh[SUBLANES:SUBLANES + 1, :] * next_ok,
                    pltpu.roll(zs, tm - 1, 0))
    za = zs[:, :RWKV_COLS]
    za_ref[0] = za + mu_ref[...] * (0.5 * (prev[:, :RWKV_COLS] + nxt[:, :RWKV_COLS]) - za)
    qk = (prev[:, RWKV_COLS:] * cw_ref[0:1, :] + zs[:, RWKV_COLS:] * cw_ref[1:2, :]
          + nxt[:, RWKV_COLS:] * cw_ref[2:3, :] + cb_ref[...])
    zm_ref[0, :, :MLSTM_QKW] = _silu(qk)
    zm_ref[0, :, MLSTM_QKW:] = z[:, SHIFT_COLS:RWKV_COLS + ZM_COLS]
    zb_ref[0] = z[:, RWKV_COLS + ZM_COLS:]


def _in_proj(x, modsel, norm_g, w_r, mu, conv_w, conv_b, *, tm, ctx_len):
    b, l, d = x.shape
    n_tiles = l // tm
    n_ctx_tiles = ctx_len // tm
    tm8 = tm // SUBLANES
    kern = functools.partial(_inproj_kernel, tm=tm, n_ctx_tiles=n_ctx_tiles, n_tiles=n_tiles)
    seg = lambda j: jnp.where(j >= n_ctx_tiles, 1, 0)
    return pl.pallas_call(
        kern,
        grid=(b, n_tiles),
        in_specs=[
            pl.BlockSpec((1, tm, d), lambda i, j: (i, j, 0)),
            pl.BlockSpec((1, SUBLANES, d), lambda i, j: (i, jnp.maximum(j * tm8 - 1, 0), 0)),
            pl.BlockSpec((1, SUBLANES, d),
                         lambda i, j: (i, jnp.minimum((j + 1) * tm8, l // SUBLANES - 1), 0)),
            pl.BlockSpec((1, 1, 2, d), lambda i, j: (i, seg(j), 0, 0)),
            pl.BlockSpec((1, d), lambda i, j: (0, 0)),
            pl.BlockSpec((d, IN_COLS_PAD), lambda i, j: (0, 0)),
            pl.BlockSpec((1, RWKV_COLS), lambda i, j: (0, 0)),
            pl.BlockSpec((3, MLSTM_QKW), lambda i, j: (0, 0)),
            pl.BlockSpec((1, MLSTM_QKW), lambda i, j: (0, 0)),
        ],
        out_specs=[
            pl.BlockSpec((1, tm, RWKV_COLS), lambda i, j: (i, j, 0)),
            pl.BlockSpec((1, tm, ZM_COLS), lambda i, j: (i, j, 0)),
            pl.BlockSpec((1, tm, ZB_COLS), lambda i, j: (i, j, 0)),
        ],
        out_shape=[
            jax.ShapeDtypeStruct((b, l, RWKV_COLS), F32),
            jax.ShapeDtypeStruct((b, l, ZM_COLS), F32),
            jax.ShapeDtypeStruct((b, l, ZB_COLS), F32),
        ],
        compiler_params=pltpu.CompilerParams(
            dimension_semantics=("arbitrary", "arbitrary"), vmem_limit_bytes=VMEM_LIMIT),
        name="in_proj",
    )(x, x, x, modsel, norm_g.reshape(1, d), w_r, mu.reshape(1, RWKV_COLS), conv_w,
      conv_b.reshape(1, MLSTM_QKW))


def _rwkv_local_kernel(za_ref, kk_ref, ka_ref, rk_ref, w0_ref, w2_ref, a0_ref, a2_ref, g2_ref,
                       g_ref, yl_ref, at_ref, bs_ref, bonus_ref, gate_ref):
    c = CHUNK
    nd = RWKV_DIM
    hd = HEAD_DIM
    nb = za_ref.shape[0]
    ones_bd = _seg_ones(nd, hd)
    eye = _iota((hd, hd), 0) == _iota((hd, hd), 1)
    eye_f = eye.astype(F32)
    row2 = _iota((2 * c, 2 * c), 0)
    rr = row2 % c
    cc = _iota((2 * c, 2 * c), 1) % c
    diag_ok = jnp.logical_and(cc == rr, row2 >= c)
    pr, qk, vh, pcol, rcol, qe, ke, gam, gmask = [], [], [], [], [], [], [], [], []
    for rb in range(nb):
        za = za_ref[rb]
        r = za[:, 0:nd]
        k = za[:, nd:2 * nd]
        v = za[:, 2 * nd:3 * nd]
        kkr = k * kk_ref[...]
        kk = kkr / jnp.maximum(jnp.sqrt(_segsum(kkr * kkr, ones_bd)), 1e-12)
        bonus_ref[rb] = _segsum(r * k * rk_ref[...], ones_bd) * v
        gd = za[:, 3 * nd + 2 * RWKV_W_LORA + 2 * RWKV_A_LORA:]
        gate_ref[rb] = _dot1(_sigmoid(gd), g2_ref[...])
        for d in range(2):
            rev = d == 1
            wo = 3 * nd + d * RWKV_W_LORA
            ao = 3 * nd + 2 * RWKV_W_LORA + d * RWKV_A_LORA
            w_lo = _dot1(jnp.tanh(za[:, wo:wo + RWKV_W_LORA]), w2_ref[d])
            log_w = -_softplus(-(w0_ref[d] + w_lo)) - 0.5
            logdec = -jnp.exp(log_w)
            a = _sigmoid(a0_ref[d] + _dot1(za[:, ao:ao + RWKV_A_LORA], a2_ref[d]))
            kd = k * (1.0 + (a - 1.0) * ka_ref[...])
            bvec = kk * a
            cum = _dot_exact_lhs(_incl_mask(c, rev).astype(BF16), logdec)
            total = cum[0:1, :] if rev else cum[c - 1:c, :]
            e_neg = jnp.exp(-cum)
            e_end = jnp.exp(total - cum)
            pm = kk * jnp.exp(cum - logdec)
            qm = bvec * e_neg
            km = kd * e_neg
            rm = r * jnp.exp(cum)
            qe_d = (bvec * e_end).astype(BF16)
            ke_d = (kd * e_end).astype(BF16)
            gam_d = jnp.exp(total)
            gmask_d = jnp.logical_or(cc > rr if rev else cc < rr, diag_ok)
            for h in range(RWKV_HEADS):
                sl = slice(h * hd, (h + 1) * hd)
                pr.append(jnp.concatenate([pm[:, sl], rm[:, sl]], axis=0).astype(BF16))
                qk.append(jnp.concatenate([qm[:, sl], km[:, sl]], axis=0).astype(BF16))
                vh.append(v[:, sl].astype(BF16))
                pcol.append(pm[:, sl])
                rcol.append(rm[:, sl])
                qe.append(qe_d[:, sl])
                ke.append(ke_d[:, sl])
                gam.append(gam_d[:, sl])
                gmask.append(gmask_d)
    n = len(pr)
    a4 = [jnp.where(gmask[i], _dg(pr[i], qk[i], _NT), 0.0) for i in range(n)]
    a4b = [x.astype(BF16) for x in a4]
    lpq = [x[:c, :c] for x in a4b]
    pw = [_dg(x, x, _NN) for x in lpq]
    lv = [_dg(a4b[i][:, c:], vh[i], _NN) for i in range(n)]
    kv = [_dg(ke[i], vh[i], _TN) for i in range(n)]
    tinv = [eye_f - a4[i][:c, :c] for i in range(n)]
    covered = 2
    while True:
        tinv = [tinv[i] + _dot1(tinv[i], pw[i]) for i in range(n)]
        covered *= 2
        if covered >= c:
            break
        pw = [_dot1(x, x) for x in pw]
    wz = [_dot1(tinv[i], jnp.concatenate([pcol[i], lv[i][:c]], axis=1)).astype(BF16)
          for i in range(n)]
    awz = [_dg(a4b[i][c:, :c], wz[i], _NN) for i in range(n)]
    qwz = [_dg(qe[i], wz[i], _TN) for i in range(n)]
    for rb in range(nb):
        for d in range(2):
            first = (rb * 2 + d) * RWKV_HEADS
            ids = range(first, first + RWKV_HEADS)
            g_ref[d, rb] = jnp.concatenate([rcol[i] - awz[i][:, :hd] for i in ids], axis=1)
            yl_ref[d, rb] = jnp.concatenate([lv[i][c:] - awz[i][:, hd:] for i in ids], axis=1)
            at_ref[d, rb] = jnp.concatenate(
                [jnp.where(eye, jnp.broadcast_to(gam[i], (hd, hd)), 0.0) - qwz[i][:, :hd]
                 for i in ids], axis=1)
            bs_ref[d, rb] = jnp.concatenate([kv[i] - qwz[i][:, hd:] for i in ids], axis=1)


def _rwkv_local(za, k_k, k_a, r_k, w0, w2, a0, a2, g2, *, nb):
    b, l, _ = za.shape
    c = CHUNK
    nd = RWKV_DIM
    full = lambda shape: pl.BlockSpec(shape, lambda i, j: (0,) * len(shape))
    dir_spec = pl.BlockSpec((2, nb, c, nd), lambda i, j: (0, i, j, 0))
    dir_shape = jax.ShapeDtypeStruct((2, b, l, nd), F32)
    row_spec = pl.BlockSpec((nb, c, nd), lambda i, j: (i, j, 0))
    row_shape = jax.ShapeDtypeStruct((b, l, nd), F32)
    return pl.pallas_call(
        _rwkv_local_kernel,
        grid=(b // nb, l // c),
        in_specs=[
            pl.BlockSpec((nb, c, RWKV_COLS), lambda i, j: (i, j, 0)),
            full((1, nd)), full((1, nd)), full((1, nd)),
            full((2, 1, nd)), full((2, RWKV_W_LORA, nd)),
            full((2, 1, nd)), full((2, RWKV_A_LORA, nd)),
            full((RWKV_G_LORA, nd)),
        ],
        out_specs=[dir_spec] * 4 + [row_spec] * 2,
        out_shape=[dir_shape] * 4 + [row_shape] * 2,
        compiler_params=pltpu.CompilerParams(
            dimension_semantics=("arbitrary", "arbitrary"), vmem_limit_bytes=VMEM_LIMIT),
        name="rwkv_local",
    )(za, k_k.reshape(1, -1), k_a.reshape(1, -1), r_k.reshape(1, -1), w0.reshape(2, 1, nd),
      w2, a0.reshape(2, 1, nd), a2, g2)


def _rwkv_seq_kernel(gf_ref, ylf_ref, atf_ref, bsf_ref, gr_ref, ylr_ref, atr_ref, bsr_ref,
                     yf_ref, yr_ref, m_ref):
    nb = yf_ref.shape[0]

    @pl.when(pl.program_id(1) == 0)
    def _():
        m_ref[...] = jnp.zeros_like(m_ref)

    dirs = ((gf_ref, ylf_ref, atf_ref, bsf_ref, yf_ref), (gr_ref, ylr_ref, atr_ref, bsr_ref, yr_ref))
    for rb in range(nb):
        for d, (g_ref, yl_ref, at_ref, bs_ref, y_ref) in enumerate(dirs):
            g = g_ref[0, rb]
            at = at_ref[0, rb]
            ys = []
            for h in range(RWKV_HEADS):
                sl = slice(h * HEAD_DIM, (h + 1) * HEAD_DIM)
                i = (rb * 2 + d) * RWKV_HEADS + h
                m0 = m_ref[i].astype(BF16)
                ys.append(_dg(g[:, sl].astype(BF16), m0, _NN))
                m_ref[i] = _dg(at[:, sl].astype(BF16), m0, _NN) + bs_ref[0, rb, :, sl]
            y_ref[rb] = jnp.concatenate(ys, axis=1) + yl_ref[0, rb]


def _rwkv_seq(g, yl, at, bs, *, ctx_len, nb):
    _, b, l, nd = g.shape
    c = CHUNK
    n_chunks = l // c
    n_ctx_chunks = ctx_len // c
    rev_idx = lambda j: _chunk_index(j, n_ctx_chunks, n_chunks, True)
    fwd = pl.BlockSpec((1, nb, c, nd), lambda i, j: (0, i, j, 0))
    bwd = pl.BlockSpec((1, nb, c, nd), lambda i, j: (1, i, rev_idx(j), 0))
    y_shape = jax.ShapeDtypeStruct((b, l, nd), F32)
    return pl.pallas_call(
        _rwkv_seq_kernel,
        grid=(b // nb, n_chunks),
        in_specs=[fwd] * 4 + [bwd] * 4,
        out_specs=[pl.BlockSpec((nb, c, nd), lambda i, j: (i, j, 0)),
                   pl.BlockSpec((nb, c, nd), lambda i, j: (i, rev_idx(j), 0))],
        out_shape=[y_shape, y_shape],
        scratch_shapes=[pltpu.VMEM((nb * 2 * RWKV_HEADS, HEAD_DIM, HEAD_DIM), F32)],
        compiler_params=pltpu.CompilerParams(
            dimension_semantics=("arbitrary", "arbitrary"), vmem_limit_bytes=VMEM_LIMIT),
        name="rwkv_seq",
    )(g, yl, at, bs, g, yl, at, bs)


def _cummax_rows(x, rev):
    n = x.shape[0]
    row = _iota(x.shape, 0)
    shift = 1
    while shift < n:
        if rev:
            moved = jnp.where(row >= n - shift, -jnp.inf, pltpu.roll(x, n - shift, 0))
        else:
            moved = jnp.where(row < shift, -jnp.inf, pltpu.roll(x, shift, 0))
        x = jnp.maximum(x, moved)
        shift *= 2
    return x


def _head_lane(d):
    return d * 2 * MLSTM_HEADS + MLSTM_HEADS


def _sel_expand(d, width):
    n = MLSTM_HEADS * width
    return (_iota((LANES, n), 0) == _head_lane(d) + _iota((LANES, n), 1) // width).astype(BF16)


def _sel_reduce(d, width):
    n = MLSTM_HEADS * width
    return (_iota((n, LANES), 1) == _head_lane(d) + _iota((n, LANES), 0) // width).astype(BF16)


def _block_diag(x, reps, row_block, col_block):
    t = jnp.concatenate([x] * reps, axis=0)
    keep = (_iota(t.shape, 0) // row_block) == (_iota(t.shape, 1) // col_block)
    return jnp.where(keep, t, jnp.zeros_like(t))


def _mlstm_local_kernel(zm_ref, gb_ref, num_ref, den_ref, mi_ref, b_ref, cl_ref, rows_ref):
    c = CHUNK
    nh = MLSTM_HEADS
    dk = MLSTM_QK
    dv = MLSTM_V
    nb = zm_ref.shape[0]
    lane_j = _iota((c, nh * c), 1) % c
    row_s = _iota((c, nh * c), 0)
    cl_keep = (_iota((nh * dk, MLSTM_DIM), 0) // dk) == (_iota((nh * dk, MLSTM_DIM), 1) // dv)
    sel_c = [_sel_expand(d, c) for d in range(2)]
    sel_k = [_sel_expand(d, dk) for d in range(2)]
    sel_r = [_sel_reduce(d, c) for d in range(2)]
    k, vb, log_f, li, qk, vbd = [], [], [], [], [], []
    for rb in range(nb):
        zm = zm_ref[rb]
        qb = (zm[:, :nh * dk] * (dk ** -0.5)).astype(BF16)
        k.append(zm[:, nh * dk:MLSTM_QKW])
        vb.append(zm[:, MLSTM_QKW:MLSTM_QKW + MLSTM_DIM].astype(BF16))
        gates = zm[:, MLSTM_QKW + 2 * MLSTM_DIM:] + gb_ref[...]
        capped = GATE_SOFTCAP * jnp.tanh(gates * (1.0 / GATE_SOFTCAP))
        log_f.append(-_softplus(-capped))
        li.append(pltpu.roll(capped, nh, 1))
        qk.append(_dg(qb, _block_diag(k[rb].astype(BF16), nh, c, dk), _NT))
        vbd.append(_block_diag(vb[rb], nh, c, dv))
    units = [(rb, d) for rb in range(nb) for d in range(2)]
    b_all = [_dot_exact_lhs(_incl_mask(c, d == 1).astype(BF16), log_f[rb]) for rb, d in units]
    x = [li[rb] - b_all[u] for u, (rb, d) in enumerate(units)]
    cm = [_cummax_rows(x[u], d == 1) for u, (rb, d) in enumerate(units)]
    e1 = [_dot_exact_rhs(-cm[u], sel_c[d]) for u, (rb, d) in enumerate(units)]
    xe = [_dot_exact_rhs(x[u], sel_c[d]) for u, (rb, d) in enumerate(units)]
    g_row, m_loc, e32 = [], [], []
    for u, (rb, d) in enumerate(units):
        last = 0 if d == 1 else c - 1
        g_row.append(b_all[u][last:last + 1, :])
        w_end = g_row[u] - b_all[u] + li[rb]
        m_loc.append(jnp.max(w_end, axis=0, keepdims=True))
        e32.append(_dot2_rhs01(jnp.exp(w_end - m_loc[u]), sel_k[d]))
    wi = []
    for u, (rb, d) in enumerate(units):
        x_row = jnp.sum(jnp.where(row_s == lane_j, xe[u], 0.0), axis=0, keepdims=True)
        earlier = (lane_j >= row_s) if d == 1 else (lane_j <= row_s)
        wi.append((qk[rb] * jnp.exp(jnp.where(earlier, e1[u] + x_row, -jnp.inf))).astype(BF16))
    nd = [_dg(wi[u], jnp.concatenate([vbd[rb], sel_r[d]], axis=1), _NN)
          for u, (rb, d) in enumerate(units)]
    ke = [k[rb] * e32[u] for u, (rb, d) in enumerate(units)]
    full = [_dg(ke[u].astype(BF16), vb[rb], _TN) for u, (rb, d) in enumerate(units)]
    for u, (rb, d) in enumerate(units):
        num_ref[d, rb] = nd[u][:, :MLSTM_DIM]
        den_ref[d, rb] = nd[u][:, MLSTM_DIM:]
        mi_ref[d, rb] = b_all[u] + cm[u]
        b_ref[d, rb] = b_all[u]
        f = jnp.where(cl_keep, full[u], 0.0)
        cl_ref[d, rb] = (f[0:dk] + f[dk:2 * dk]) + (f[2 * dk:3 * dk] + f[3 * dk:4 * dk])
        rows_ref[d, rb] = jnp.concatenate(
            [jnp.sum(ke[u], axis=0, keepdims=True), m_loc[u], g_row[u],
             jnp.zeros((SUBLANES - 3, LANES), F32)], axis=0)


def _mlstm_local(zm, gate_bias, *, nb):
    b, l, _ = zm.shape
    c = CHUNK
    n_chunks = l // c
    spec = lambda rows, w: pl.BlockSpec((2, nb, rows, w), lambda i, j: (0, i, j, 0))
    shape = lambda rows, w: jax.ShapeDtypeStruct((2, b, n_chunks * rows, w), F32)
    outs = [(c, MLSTM_DIM), (c, LANES), (c, LANES), (c, LANES), (MLSTM_QK, MLSTM_DIM),
            (SUBLANES, LANES)]
    return pl.pallas_call(
        _mlstm_local_kernel,
        grid=(b // nb, n_chunks),
        in_specs=[
            pl.BlockSpec((nb, c, ZM_COLS), lambda i, j: (i, j, 0)),
            pl.BlockSpec((1, LANES), lambda i, j: (0, 0)),
        ],
        out_specs=[spec(*o) for o in outs],
        out_shape=[shape(*o) for o in outs],
        compiler_params=pltpu.CompilerParams(
            dimension_semantics=("arbitrary", "arbitrary"), vmem_limit_bytes=VMEM_LIMIT),
        name="mlstm_local",
    )(zm, gate_bias)


def _mlstm_seq_kernel(qf_ref, numf_ref, denf_ref, mif_ref, bf_ref, clf_ref, rowsf_ref,
                      qr_ref, numr_ref, denr_ref, mir_ref, br_ref, clr_ref, rowsr_ref,
                      hf_ref, hr_ref, cbd_ref, n_ref, m_ref):
    c = CHUNK
    nh = MLSTM_HEADS
    dk = MLSTM_QK
    dv = MLSTM_V
    nb = hf_ref.shape[0]

    @pl.when(pl.program_id(1) == 0)
    def _():
        cbd_ref[...] = jnp.zeros_like(cbd_ref)
        n_ref[...] = jnp.zeros_like(n_ref)
        m_ref[...] = jnp.zeros_like(m_ref)

    dirs = ((qf_ref, numf_ref, denf_ref, mif_ref, bf_ref, clf_ref, rowsf_ref, hf_ref),
            (qr_ref, numr_ref, denr_ref, mir_ref, br_ref, clr_ref, rowsr_ref, hr_ref))
    units = [(rb, d) for rb in range(nb) for d in range(2)]
    nu = len(units)
    sel_v = [_sel_expand(d, dv) for d in range(2)]
    sel_k = [_sel_expand(d, dk) for d in range(2)]
    sel_r = [_sel_reduce(d, dk) for d in range(2)]
    q = [dirs[d][0][rb] * (dk ** -0.5) for rb, d in units]
    rows = [dirs[d][6][0, rb] for rb, d in units]
    m_row = [m_ref[u] for u in range(nu)]
    n_row = [n_ref[u] for u in range(nu)]
    cbd = [cbd_ref[u] for u in range(nu)]
    qn = [_dot2_rhs01(q[u] * n_row[u], sel_r[d]) for u, (rb, d) in enumerate(units)]
    qc = [_dg(q[u].astype(BF16), cbd[u].astype(BF16), _NN) for u in range(nu)]
    a12, srow = [], []
    for u, (rb, d) in enumerate(units):
        mi = dirs[d][3][0, rb]
        log_inter = dirs[d][4][0, rb] + m_row[u]
        m_out = jnp.maximum(log_inter, mi)
        s_intra = jnp.exp(mi - m_out)
        s_inter = jnp.exp(log_inter - m_out)
        den = s_intra * dirs[d][2][0, rb] + s_inter * qn[u]
        dinv = 1.0 / jnp.maximum(jnp.abs(den), jnp.exp(-m_out))
        a12.append(jnp.concatenate([s_intra * dinv, s_inter * dinv], axis=0))
        m_loc, g_row = rows[u][1:2], rows[u][2:3]
        m_new = jnp.maximum(g_row + m_row[u], m_loc)
        srow.append(jnp.concatenate(
            [jnp.exp(g_row + m_row[u] - m_new), jnp.exp(m_loc - m_new),
             jnp.zeros((SUBLANES - 2, LANES), F32)], axis=0))
        m_ref[u] = m_new
    a12e = [_dot2_rhs01(a12[u], sel_v[d]) for u, (rb, d) in enumerate(units)]
    s_v = [_dot2_rhs01(srow[u], sel_v[d]) for u, (rb, d) in enumerate(units)]
    s_k = [_dot2_rhs01(srow[u], sel_k[d]) for u, (rb, d) in enumerate(units)]
    for u, (rb, d) in enumerate(units):
        dirs[d][7][rb] = a12e[u][:c] * dirs[d][1][0, rb] + a12e[u][c:] * qc[u]
        cl_full = _block_diag(dirs[d][5][0, rb], nh, dk, dv)
        cbd_ref[u] = s_v[u][0:1] * cbd[u] + s_v[u][1:2] * cl_full
        n_ref[u] = s_k[u][0:1] * n_row[u] + s_k[u][1:2] * rows[u][0:1]


def _mlstm_seq(zm, num, den, mi, bsum, cl, rows, *, ctx_len, nb):
    b, l, _ = zm.shape
    c = CHUNK
    n_chunks = l // c
    n_ctx_chunks = ctx_len // c
    rev_idx = lambda j: _chunk_index(j, n_ctx_chunks, n_chunks, True)

    def specs(d):
        cidx = (lambda j: j) if d == 0 else rev_idx
        blk = lambda r, w: pl.BlockSpec((1, nb, r, w), lambda i, j: (d, i, cidx(j), 0))
        return [pl.BlockSpec((nb, c, MLSTM_HEADS * MLSTM_QK), lambda i, j: (i, cidx(j), 0)),
                blk(c, MLSTM_DIM), blk(c, LANES), blk(c, LANES), blk(c, LANES),
                blk(MLSTM_QK, MLSTM_DIM), blk(SUBLANES, LANES)]

    h_shape = jax.ShapeDtypeStruct((b, l, MLSTM_DIM), F32)
    args = (zm, num, den, mi, bsum, cl, rows)
    return pl.pallas_call(
        _mlstm_seq_kernel,
        grid=(b // nb, n_chunks),
        in_specs=specs(0) + specs(1),
        out_specs=[pl.BlockSpec((nb, c, MLSTM_DIM), lambda i, j: (i, j, 0)),
                   pl.BlockSpec((nb, c, MLSTM_DIM), lambda i, j: (i, rev_idx(j), 0))],
        out_shape=[h_shape, h_shape],
        scratch_shapes=[
            pltpu.VMEM((2 * nb, MLSTM_HEADS * MLSTM_QK, MLSTM_DIM), F32),
            pltpu.VMEM((2 * nb, 1, LANES), F32),
            pltpu.VMEM((2 * nb, 1, LANES), F32),
        ],
        compiler_params=pltpu.CompilerParams(
            dimension_semantics=("arbitrary", "arbitrary"), vmem_limit_bytes=VMEM_LIMIT),
        name="mlstm_seq",
    )(*args, *args)


def _rope_rot(x):
    lane = _iota(x.shape, 1)
    half = MLA_ROPE // 2
    return jnp.where(lane < MLA_NOPE + half, -pltpu.roll(x, LANES - half, 1),
                     pltpu.roll(x, half, 1))


def _mla_proj_kernel(zb_ref, cos_ref, sin_ref, qg_ref, kvg_ref, gq_ref, gk_ref,
                     wq_ref, wk_ref, wv_ref, q_ref, k_ref, v_ref):
    zb = zb_ref[0]
    cq = zb[:, :MLA_Q_RANK]
    ckv = zb[:, MLA_Q_RANK:MLA_Q_RANK + MLA_KV_RANK]
    kr = zb[:, MLA_Q_RANK + MLA_KV_RANK:]
    cos = cos_ref[...]
    sin = sin_ref[...]
    cqn = cq * lax.rsqrt(jnp.mean(cq * cq, axis=-1, keepdims=True) + NORM_EPS) * qg_ref[...]
    ckn = ckv * lax.rsqrt(jnp.mean(ckv * ckv, axis=-1, keepdims=True) + NORM_EPS) * kvg_ref[...]
    q_all = _dot1(cqn, wq_ref[...])
    k_all = _dot1(ckn, wk_ref[...])
    v_ref[0] = _dot1(ckn, wv_ref[...]).astype(BF16)
    scale = MLA_QK ** -0.5
    for h in range(MLA_HEADS):
        sl = slice(h * QPAD, (h + 1) * QPAD)
        qh = q_all[:, sl]
        qn = qh * lax.rsqrt(jnp.sum(qh * qh, axis=-1, keepdims=True) * (1.0 / MLA_QK) + NORM_EPS)
        qn = qn * gq_ref[...]
        q_ref[0, :, sl] = ((qn * cos + _rope_rot(qn) * sin) * scale).astype(BF16)
        kh = k_all[:, sl] + kr
        kn = kh * lax.rsqrt(jnp.sum(kh * kh, axis=-1, keepdims=True) * (1.0 / MLA_QK) + NORM_EPS)
        kn = kn * gk_ref[...]
        k_ref[0, :, sl] = (kn * cos + _rope_rot(kn) * sin).astype(BF16)


def _mla_proj(zb, cos_f, sin_f, q_norm_g, kv_norm_g, gq, gk, wq_r, wk_r, wv_r, *, tm):
    b, l, _ = zb.shape
    hq = MLA_HEADS * QPAD
    full = lambda shape: pl.BlockSpec(shape, lambda i, j: (0,) * len(shape))
    return pl.pallas_call(
        _mla_proj_kernel,
        grid=(b, l // tm),
        in_specs=[
            pl.BlockSpec((1, tm, ZB_COLS), lambda i, j: (i, j, 0)),
            pl.BlockSpec((tm, QPAD), lambda i, j: (j, 0)),
            pl.BlockSpec((tm, QPAD), lambda i, j: (j, 0)),
            full((1, MLA_Q_RANK)), full((1, MLA_KV_RANK)), full((1, QPAD)), full((1, QPAD)),
            full((MLA_Q_RANK, hq)), full((MLA_KV_RANK, hq)), full((MLA_KV_RANK, MLA_DIM)),
        ],
        out_specs=[
            pl.BlockSpec((1, tm, hq), lambda i, j: (i, j, 0)),
            pl.BlockSpec((1, tm, hq), lambda i, j: (i, j, 0)),
            pl.BlockSpec((1, tm, MLA_DIM), lambda i, j: (i, j, 0)),
        ],
        out_shape=[
            jax.ShapeDtypeStruct((b, l, hq), BF16),
            jax.ShapeDtypeStruct((b, l, hq), BF16),
            jax.ShapeDtypeStruct((b, l, MLA_DIM), BF16),
        ],
        compiler_params=pltpu.CompilerParams(
            dimension_semantics=("arbitrary", "arbitrary"), vmem_limit_bytes=VMEM_LIMIT),
        name="mla_proj",
    )(zb, cos_f, sin_f, q_norm_g.reshape(1, -1), kv_norm_g.reshape(1, -1), gq, gk,
      wq_r, wk_r, wv_r)


def _attend_tile(q_ref, k_ref, v_ref, o_ref, n_blocks):
    tq = q_ref.shape[1]
    bk = KV_BLOCK
    lane_v = _iota((bk, 2 * MLA_V), 1)
    lane_o = _iota((tq, 2 * MLA_V), 1)
    sum_lane = [(1 - h) * MLA_V for h in range(2)]
    q = [q_ref[0, :, h * QPAD:(h + 1) * QPAD] for h in range(2)]

    def scores(j):
        return [_dg(q[h], k_ref[0, j * bk:(j + 1) * bk, h * QPAD:(h + 1) * QPAD], _NT)
                for h in range(2)]

    m = [jnp.full((tq, 1), -jnp.inf, F32) for _ in range(2)]
    acc = [jnp.zeros((tq, 2 * MLA_V), F32) for _ in range(2)]
    s = scores(0)
    for j in range(n_blocks):
        s_next = scores(j + 1) if j + 1 < n_blocks else None
        v = v_ref[0, j * bk:(j + 1) * bk, :]
        for h in range(2):
            ones = (lane_v == sum_lane[h]).astype(BF16)
            vh = jnp.where((lane_v // MLA_V) == h, v, ones)
            m_new = jnp.maximum(m[h], jnp.max(s[h], axis=-1, keepdims=True))
            p = jnp.exp(s[h] - m_new).astype(BF16)
            acc[h] = jnp.exp(m[h] - m_new) * acc[h] + _dg(p, vh, _NN)
            m[h] = m_new
        s = s_next
    out = jnp.zeros((tq, 2 * MLA_V), F32)
    for h in range(2):
        denom = jnp.sum(jnp.where(lane_o == sum_lane[h], acc[h], 0.0), axis=-1, keepdims=True)
        out = out + jnp.where((lane_o // MLA_V) == h, acc[h], 0.0) / denom
    o_ref[0] = out


def _mla_attn_kernel(q_ref, k_ref, v_ref, o_ref, *, ctx_len, first_tile, n_ctx_tiles):
    n_all = k_ref.shape[1] // KV_BLOCK
    if first_tile >= n_ctx_tiles:
        _attend_tile(q_ref, k_ref, v_ref, o_ref, n_all)
        return
    is_ctx = (pl.program_id(2) + first_tile) < n_ctx_tiles

    @pl.when(is_ctx)
    def _():
        _attend_tile(q_ref, k_ref, v_ref, o_ref, ctx_len // KV_BLOCK)

    @pl.when(jnp.logical_not(is_ctx))
    def _():
        _attend_tile(q_ref, k_ref, v_ref, o_ref, n_all)


def _mla_attn(q, k, v, *, tq, ctx_len, first_tile):
    b, l, _ = q.shape
    assert l % KV_BLOCK == 0 and ctx_len % KV_BLOCK == 0
    n_q = l // tq - first_tile
    pairs = MLA_HEADS // 2
    kern = functools.partial(_mla_attn_kernel, ctx_len=ctx_len, first_tile=first_tile,
                             n_ctx_tiles=ctx_len // tq)
    return pl.pallas_call(
        kern,
        grid=(b, pairs, n_q),
        in_specs=[
            pl.BlockSpec((1, tq, 2 * QPAD), lambda i, p, j: (i, j + first_tile, p)),
            pl.BlockSpec((1, l, 2 * QPAD), lambda i, p, j: (i, 0, p)),
            pl.BlockSpec((1, l, 2 * MLA_V), lambda i, p, j: (i, 0, p)),
        ],
        out_specs=pl.BlockSpec((1, tq, 2 * MLA_V), lambda i, p, j: (i, j + first_tile, p)),
        out_shape=jax.ShapeDtypeStruct((b, l, MLA_DIM), F32),
        compiler_params=pltpu.CompilerParams(
            dimension_semantics=("arbitrary", "arbitrary", "arbitrary"),
            vmem_limit_bytes=VMEM_LIMIT),
        name="mla_attn",
    )(q, k, v)


def _outproj_kernel(x_ref, yf_ref, yr_ref, bonus_ref, gate_ref, attn_ref, hf_ref, hr_ref,
                    o_ref, mod_ref, lng_ref, lnb_ref, mng_ref, wa_ref, wb_ref, wm_ref, out_ref):
    ones_bd = _seg_ones(RWKV_DIM, HEAD_DIM)
    inv = 1.0 / HEAD_DIM
    y = yf_ref[0] + yr_ref[0]
    mean = _segsum(y, ones_bd) * inv
    yc = y - mean
    var = _segsum(yc * yc, ones_bd) * inv
    yn = yc * lax.rsqrt(var + RWKV_GN_EPS) * lng_ref[...] + lnb_ref[...]
    ya = (yn + bonus_ref[0]) * gate_ref[0]
    hm = hf_ref[0] + hr_ref[0]
    hn = hm * lax.rsqrt(_segsum(hm * hm, ones_bd) * inv + NORM_EPS)
    ym = hn * mng_ref[...] * _sigmoid(o_ref[0])
    mix = (_dot1(ya, wa_ref[...]) + _dot1(attn_ref[0], wb_ref[...]) + _dot1(ym, wm_ref[...]))
    out_ref[0] = x_ref[0] + mod_ref[0, 0] * mix


def _out_proj(x, yf, yr, bonus, gate, attn, hf, hr, zm, modsel, ln_g, ln_b, mnorm_g,
              wa, wb, wm, *, tm, ctx_len, first_tile):
    b, l, d = x.shape
    n_ctx_tiles = ctx_len // tm
    n_tiles = l // tm - first_tile
    row = lambda w: pl.BlockSpec((1, tm, w), lambda i, j: (i, j + first_tile, 0))
    full = lambda shape: pl.BlockSpec(shape, lambda i, j: (0,) * len(shape))
    seg = lambda j: jnp.where(j + first_tile >= n_ctx_tiles, 1, 0)
    o_block = (MLSTM_QKW + MLSTM_DIM) // MLSTM_DIM
    return pl.pallas_call(
        _outproj_kernel,
        grid=(b, n_tiles),
        in_specs=[
            row(d), row(RWKV_DIM), row(RWKV_DIM), row(RWKV_DIM), row(RWKV_DIM), row(MLA_DIM),
            row(MLSTM_DIM), row(MLSTM_DIM),
            pl.BlockSpec((1, tm, MLSTM_DIM), lambda i, j: (i, j + first_tile, o_block)),
            pl.BlockSpec((1, 1, 1, d), lambda i, j: (i, seg(j), 0, 0)),
            full((1, RWKV_DIM)), full((1, RWKV_DIM)), full((1, MLSTM_DIM)),
            full((RWKV_DIM, d)), full((MLA_DIM, d)), full((MLSTM_DIM, d)),
        ],
        out_specs=pl.BlockSpec((1, tm, d), lambda i, j: (i, j, 0)),
        out_shape=jax.ShapeDtypeStruct((b, n_tiles * tm, d), F32),
        compiler_params=pltpu.CompilerParams(
            dimension_semantics=("arbitrary", "arbitrary"), vmem_limit_bytes=VMEM_LIMIT),
        name="out_proj",
    )(x, yf, yr, bonus, gate, attn, hf, hr, zm, modsel, ln_g.reshape(1, -1),
      ln_b.reshape(1, -1), mnorm_g.reshape(1, -1), wa, wb, wm)


def _ffn_kernel(x_ref, mod_ref, g_ref, w1_ref, w2_ref, out_ref, *, hidden):
    x = x_ref[0]
    h = _rmsnorm_mod(x, g_ref[...], mod_ref[0, 0, 0:1, :], mod_ref[0, 0, 1:2, :]).astype(BF16)
    gu = _dg(h, w1_ref[...], _NN)
    act = (_silu(gu[:, :hidden]) * gu[:, hidden:]).astype(BF16)
    out_ref[0] = x + mod_ref[0, 0, 2:3, :] * _dg(act, w2_ref[...], _NN)


def _ffn(x, modsel, norm_g, w1, w2, *, tm, n_ctx_tiles_here):
    b, l, d = x.shape
    hidden = w2.shape[0]
    seg = lambda j: jnp.where(j >= n_ctx_tiles_here, 1, 0)
    return pl.pallas_call(
        functools.partial(_ffn_kernel, hidden=hidden),
        grid=(b, l // tm),
        in_specs=[
            pl.BlockSpec((1, tm, d), lambda i, j: (i, j, 0)),
            pl.BlockSpec((1, 1, 3, d), lambda i, j: (i, seg(j), 0, 0)),
            pl.BlockSpec((1, d), lambda i, j: (0, 0)),
            pl.BlockSpec((d, 2 * hidden), lambda i, j: (0, 0)),
            pl.BlockSpec((hidden, d), lambda i, j: (0, 0)),
        ],
        out_specs=pl.BlockSpec((1, tm, d), lambda i, j: (i, j, 0)),
        out_shape=jax.ShapeDtypeStruct((b, l, d), F32),
        compiler_params=pltpu.CompilerParams(
            dimension_semantics=("arbitrary", "arbitrary"), vmem_limit_bytes=VMEM_LIMIT),
        name="ffn",
    )(x, modsel, norm_g.reshape(1, d), w1, w2)


def _arrange_w_in(w_in):
    d = w_in.shape[0]
    wa = w_in[:, :RWKV_COLS]
    wb = w_in[:, RWKV_COLS:RWKV_COLS + MLA_COLS]
    wm = w_in[:, RWKV_COLS + MLA_COLS:]
    gates = wm[:, MLSTM_QKW + 2 * MLSTM_DIM:]
    gate_blk = jnp.pad(gates, ((0, 0), (0, LANES - gates.shape[1])))
    kr = wb[:, MLA_Q_RANK + MLA_KV_RANK:]
    kr_blk = jnp.pad(kr, ((0, 0), (MLA_NOPE, LANES - MLA_NOPE - MLA_ROPE)))
    out = jnp.concatenate([wa, wm[:, :MLSTM_QKW + 2 * MLSTM_DIM], gate_blk,
                           wb[:, :MLA_Q_RANK + MLA_KV_RANK], kr_blk], axis=1)
    assert out.shape == (d, IN_COLS_PAD)
    return out.astype(BF16)


def _arrange_mla_weights(w_uq, w_ukv):
    rq = w_uq.shape[0]
    wq = w_uq.reshape(rq, MLA_HEADS, MLA_QK)
    wq = jnp.pad(wq, ((0, 0), (0, 0), (0, QPAD - MLA_QK))).reshape(rq, MLA_HEADS * QPAD)
    rk = w_ukv.shape[0]
    wkv = w_ukv.reshape(rk, MLA_HEADS, MLA_NOPE + MLA_V)
    wk = jnp.pad(wkv[:, :, :MLA_NOPE], ((0, 0), (0, 0), (0, QPAD - MLA_NOPE)))
    wk = wk.reshape(rk, MLA_HEADS * QPAD)
    wv = wkv[:, :, MLA_NOPE:].reshape(rk, MLA_DIM)
    return wq.astype(BF16), wk.astype(BF16), wv.astype(BF16)


def _rope_tables(seq_len, ctx_len):
    rows = seq_len // GRID_W
    row = jnp.repeat(jnp.arange(rows, dtype=F32), GRID_W)
    col = jnp.tile(jnp.arange(GRID_W, dtype=F32), rows)
    n_freq = MLA_ROPE // 4
    inv = jnp.power(ROPE_BASE, -jnp.arange(n_freq, dtype=F32) / n_freq)
    ang = jnp.concatenate([row[:, None] * inv, col[:, None] * inv], axis=-1)
    cos, sin = jnp.cos(ang), jnp.sin(ang)
    pad_l, pad_r = MLA_NOPE, QPAD - MLA_QK
    cos_f = jnp.concatenate([jnp.ones((seq_len, pad_l), F32), cos, cos,
                             jnp.ones((seq_len, pad_r), F32)], axis=1)
    sin_f = jnp.concatenate([jnp.zeros((seq_len, pad_l), F32), sin, sin,
                             jnp.zeros((seq_len, pad_r), F32)], axis=1)
    cos_f = jnp.concatenate([jnp.ones((ctx_len, QPAD), F32), cos_f], axis=0)
    sin_f = jnp.concatenate([jnp.zeros((ctx_len, QPAD), F32), sin_f], axis=0)
    return cos_f, sin_f


def _pad_gain(g):
    return jnp.pad(g, (0, QPAD - g.shape[0])).reshape(1, QPAD)


def _gate_bias(i_b, f_b):
    gb = jnp.stack([i_b, f_b], axis=1).reshape(-1)
    return jnp.pad(gb, (0, LANES - gb.shape[0])).reshape(1, LANES)


def kernel(x, c, ctx, c_ctx, mod_w, mod_b, norm1_g, norm2_g, w_in, w_out, ffn_w_in, ffn_w_out, rwkv_mu, rwkv_w0, rwkv_w2, rwkv_a0, rwkv_a2, rwkv_g2, rwkv_k_k, rwkv_k_a, rwkv_r_k, rwkv_ln_g, rwkv_ln_b, mla_q_norm_g, mla_w_uq, mla_kv_norm_g, mla_w_ukv, mla_q_qknorm_g, mla_k_qknorm_g, mlstm_conv_w, mlstm_conv_b, mlstm_i_b, mlstm_f_b, mlstm_norm_g):
    bsz, seq, d = x.shape
    ctx_len = ctx.shape[1]
    depth = mod_w.shape[0]
    tm = min(256, ctx_len)
    assert ctx_len % tm == 0 and seq % tm == 0 and ctx_len % CHUNK == 0 and seq % CHUNK == 0
    nb = ROWS_PER_STEP if bsz % ROWS_PER_STEP == 0 else 1

    xu = jnp.concatenate([ctx, x], axis=1)
    n_ctx_tiles = ctx_len // tm

    rows = -(-(bsz + 1) // SUBLANES) * SUBLANES
    cvec = jnp.zeros((rows, d), F32).at[:bsz].set(c).at[bsz].set(c_ctx)
    mod = _modulation(cvec, mod_w, mod_b).reshape(depth, rows, 6, d)
    mod_lat = mod[:, :bsz]
    mod_ctx = jnp.broadcast_to(mod[:, bsz][:, None], mod_lat.shape)
    modsel = jnp.stack([mod_ctx, mod_lat], axis=2)

    cos_f, sin_f = _rope_tables(seq, ctx_len)

    for i in range(depth):
        last = i == depth - 1
        first_tile = n_ctx_tiles if last else 0
        w_r = _arrange_w_in(w_in[i])
        za, zm, zb = _in_proj(xu, modsel[i, :, :, 0:2], norm1_g[i], w_r, rwkv_mu[i],
                              mlstm_conv_w[i], mlstm_conv_b[i], tm=tm, ctx_len=ctx_len)
        g_loc, y_loc, a_tr, b_st, bonus, gate = _rwkv_local(
            za, rwkv_k_k[i], rwkv_k_a[i], rwkv_r_k[i], rwkv_w0[i], rwkv_w2[i], rwkv_a0[i],
            rwkv_a2[i], rwkv_g2[i], nb=nb)
        yf, yr = _rwkv_seq(g_loc, y_loc, a_tr, b_st, ctx_len=ctx_len, nb=nb)
        wq_r, wk_r, wv_r = _arrange_mla_weights(mla_w_uq[i], mla_w_ukv[i])
        q, k, v = _mla_proj(zb, cos_f, sin_f, mla_q_norm_g[i], mla_kv_norm_g[i],
                            _pad_gain(mla_q_qknorm_g[i]), _pad_gain(mla_k_qknorm_g[i]),
                            wq_r, wk_r, wv_r, tm=tm)
        attn = _mla_attn(q, k, v, tq=tm, ctx_len=ctx_len, first_tile=first_tile)
        gb = _gate_bias(mlstm_i_b[i], mlstm_f_b[i])
        hf, hr = _mlstm_seq(zm, *_mlstm_local(zm, gb, nb=nb), ctx_len=ctx_len, nb=nb)
        wo = w_out[i].astype(BF16)
        xu = _out_proj(xu, yf, yr, bonus, gate, attn, hf, hr, zm, modsel[i, :, :, 2:3],
                       rwkv_ln_g[i], rwkv_ln_b[i], mlstm_norm_g[i],
                       wo[:RWKV_DIM], wo[RWKV_DIM:RWKV_DIM + MLA_DIM], wo[RWKV_DIM + MLA_DIM:],
                       tm=tm, ctx_len=ctx_len, first_tile=first_tile)
        xu = _ffn(xu, modsel[i, :, :, 3:6], norm2_g[i], ffn_w_in[i].astype(BF16),
                  ffn_w_out[i].astype(BF16), tm=tm,
                  n_ctx_tiles_here=0 if last else n_ctx_tiles)
    return xu
```

```python
import functools

import jax
import jax.numpy as jnp
from jax import lax
from jax.experimental import pallas as pl
from jax.experimental.pallas import tpu as pltpu

F32 = jnp.float32
BF16 = jnp.bfloat16

HEAD_DIM = 64
NORM_EPS = 1e-6
GRID_W = 64
ROPE_BASE = 10000.0
RWKV_HEADS = 4
RWKV_DIM = RWKV_HEADS * HEAD_DIM
RWKV_W_LORA = 64
RWKV_A_LORA = 64
RWKV_G_LORA = 128
RWKV_COLS = 3 * RWKV_DIM + 2 * RWKV_W_LORA + 2 * RWKV_A_LORA + RWKV_G_LORA
RWKV_GN_EPS = 64e-5
MLA_HEADS = 8
MLA_NOPE = 64
MLA_ROPE = 32
MLA_V = HEAD_DIM
MLA_QK = MLA_NOPE + MLA_ROPE
MLA_Q_RANK = 512
MLA_KV_RANK = 256
MLA_DIM = MLA_HEADS * MLA_V
MLA_COLS = MLA_Q_RANK + MLA_KV_RANK + MLA_ROPE
MLSTM_HEADS = 4
MLSTM_QK = 32
MLSTM_V = HEAD_DIM
MLSTM_DIM = MLSTM_HEADS * MLSTM_V
MLSTM_QKW = 2 * MLSTM_HEADS * MLSTM_QK
GATE_SOFTCAP = 15.0
MLSTM_COLS = MLSTM_QKW + 2 * MLSTM_DIM + 4 * MLSTM_HEADS

LANES = 128
SUBLANES = 8
CHUNK = HEAD_DIM
ROWS_PER_STEP = 4
KV_BLOCK = 256
QPAD = LANES
ZB_COLS = MLA_Q_RANK + MLA_KV_RANK + LANES
ZM_COLS = MLSTM_QKW + 2 * MLSTM_DIM
SHIFT_COLS = RWKV_COLS + MLSTM_QKW
IN_COLS_PAD = RWKV_COLS + ZM_COLS + LANES + ZB_COLS
ACT = BF16
VMEM_LIMIT = 56 * 1024 * 1024


def _split(a):
    hi = a.astype(BF16)
    lo = (a - hi.astype(F32)).astype(BF16)
    return hi, lo


_NN = (((1,), (0,)), ((), ()))
_NT = (((1,), (1,)), ((), ()))
_TN = (((0,), (0,)), ((), ()))


def _dg(a, b, dims):
    return lax.dot_general(a, b, dims, preferred_element_type=F32)


def _dot1(a, b, dims=_NN):
    return _dg(a.astype(BF16), b.astype(BF16), dims)


def _dot3(a, b, dims=_NN):
    ah, al = _split(a)
    bh, bl = _split(b)
    return _dg(ah, bh, dims) + (_dg(ah, bl, dims) + _dg(al, bh, dims))


def _dot_exact_rhs(a, b01, dims=_NN):
    a1 = a.astype(BF16)
    r1 = a - a1.astype(F32)
    a2 = r1.astype(BF16)
    a3 = (r1 - a2.astype(F32)).astype(BF16)
    return _dg(a1, b01, dims) + (_dg(a2, b01, dims) + _dg(a3, b01, dims))


def _dot_exact_lhs(a01, b, dims=_NN):
    b1 = b.astype(BF16)
    r1 = b - b1.astype(F32)
    b2 = r1.astype(BF16)
    b3 = (r1 - b2.astype(F32)).astype(BF16)
    return _dg(a01, b1, dims) + (_dg(a01, b2, dims) + _dg(a01, b3, dims))


def _dot2_rhs01(a, b01):
    ah, al = _split(a)
    return _dg(ah, b01, _NN) + _dg(al, b01, _NN)


def _iota(shape, dim):
    return lax.broadcasted_iota(jnp.int32, shape, dim)


def _seg_ones(n, seg):
    r = _iota((n, n), 0) // seg
    c = _iota((n, n), 1) // seg
    return (r == c).astype(BF16)


def _segsum(x, ones_bd):
    return _dot_exact_rhs(x, ones_bd)


def _softplus(x):
    return jnp.maximum(x, 0.0) + jnp.log(1.0 + jnp.exp(-jnp.abs(x)))


def _sigmoid(x):
    return 1.0 / (1.0 + jnp.exp(-x))


def _silu(x):
    return x * _sigmoid(x)


def _rmsnorm_mod(x, g, shift, scale):
    y = x * lax.rsqrt(jnp.mean(x * x, axis=-1, keepdims=True) + NORM_EPS)
    return (y * g) * (1.0 + scale) + shift


def _chunk_index(i, n_ctx_chunks, n_chunks, rev):
    if not rev:
        return i
    return jnp.where(i < n_ctx_chunks, n_ctx_chunks - 1 - i,
                     n_chunks - 1 - (i - n_ctx_chunks))


def _incl_mask(n, rev):
    r = _iota((n, n), 0)
    c = _iota((n, n), 1)
    return (c >= r) if rev else (c <= r)


def _mod_kernel(c_ref, w_ref, b_ref, o_ref):
    o_ref[0] = _dot3(_silu(c_ref[...]), w_ref[0]) + b_ref[0]


def _modulation(cvec, mod_w, mod_b):
    depth, d, n = mod_w.shape
    tn = 1536
    rows = cvec.shape[0]
    return pl.pallas_call(
        _mod_kernel,
        grid=(depth, n // tn),
        in_specs=[
            pl.BlockSpec((rows, d), lambda l, j: (0, 0)),
            pl.BlockSpec((1, d, tn), lambda l, j: (l, 0, j)),
            pl.BlockSpec((1, 1, tn), lambda l, j: (l, 0, j)),
        ],
        out_specs=pl.BlockSpec((1, rows, tn), lambda l, j: (l, 0, j)),
        out_shape=jax.ShapeDtypeStruct((depth, rows, n), F32),
        compiler_params=pltpu.CompilerParams(
            dimension_semantics=("arbitrary", "arbitrary"), vmem_limit_bytes=VMEM_LIMIT),
        name="adaln_mod",
    )(cvec, mod_w, mod_b.reshape(depth, 1, n))


def _inproj_kernel(x_ref, xp_ref, xn_ref, mod_ref, g_ref, w_ref, mu_ref, cw_ref, cb_ref,
                   za_ref, zm_ref, zg_ref, zb_ref, *, tm, n_ctx_tiles, n_tiles):
    j = pl.program_id(1)
    g = g_ref[...]
    shift = mod_ref[0, 0, 0:1, :]
    scale = mod_ref[0, 0, 1:2, :]
    h = _rmsnorm_mod(x_ref[0], g, shift, scale).astype(BF16)
    z = _dg(h, w_ref[...], _NN)
    halo = jnp.concatenate([xp_ref[0], xn_ref[0]], axis=0)
    hh = _rmsnorm_mod(halo, g, shift, scale).astype(BF16)
    zh = _dg(hh, w_ref[:, :SHIFT_COLS], _NN)
    prev_ok = jnp.logical_and(j != 0, j != n_ctx_tiles).astype(F32)
    next_ok = jnp.logical_and(j != n_ctx_tiles - 1, j != n_tiles - 1).astype(F32)
    zs = z[:, :SHIFT_COLS]
    row = _iota((tm, SHIFT_COLS), 0)
    prev = jnp.where(row == 0, zh[SUBLANES - 1:SUBLANES, :] * prev_ok, pltpu.roll(zs, 1, 0))
    nxt = jnp.where(row == tm - 1, zh[SUBLANES:SUBLANES + 1, :] * next_ok,
                    pltpu.roll(zs, tm - 1, 0))
    za = zs[:, :RWKV_COLS]
    shifted = za + mu_ref[...] * (0.5 * (prev[:, :RWKV_COLS] + nxt[:, :RWKV_COLS]) - za)
    za_ref[0] = shifted.astype(za_ref.dtype)
    qk = (prev[:, RWKV_COLS:] * cw_ref[0:1, :] + zs[:, RWKV_COLS:] * cw_ref[1:2, :]
          + nxt[:, RWKV_COLS:] * cw_ref[2:3, :] + cb_ref[...])
    zm_ref[0, :, :MLSTM_QKW] = _silu(qk).astype(zm_ref.dtype)
    zm_ref[0, :, MLSTM_QKW:] = z[:, SHIFT_COLS:RWKV_COLS + ZM_COLS].astype(zm_ref.dtype)
    gate_end = RWKV_COLS + ZM_COLS + LANES
    zg_ref[0] = z[:, RWKV_COLS + ZM_COLS:gate_end]
    zb_ref[0] = z[:, gate_end:].astype(zb_ref.dtype)


def _in_proj(x, modsel, norm_g, w_r, mu, conv_w, conv_b, *, tm, ctx_len):
    b, l, d = x.shape
    n_tiles = l // tm
    n_ctx_tiles = ctx_len // tm
    tm8 = tm // SUBLANES
    kern = functools.partial(_inproj_kernel, tm=tm, n_ctx_tiles=n_ctx_tiles, n_tiles=n_tiles)
    seg = lambda j: jnp.where(j >= n_ctx_tiles, 1, 0)
    return pl.pallas_call(
        kern,
        grid=(b, n_tiles),
        in_specs=[
            pl.BlockSpec((1, tm, d), lambda i, j: (i, j, 0)),
            pl.BlockSpec((1, SUBLANES, d), lambda i, j: (i, jnp.maximum(j * tm8 - 1, 0), 0)),
            pl.BlockSpec((1, SUBLANES, d),
                         lambda i, j: (i, jnp.minimum((j + 1) * tm8, l // SUBLANES - 1), 0)),
            pl.BlockSpec((1, 1, 2, d), lambda i, j: (i, seg(j), 0, 0)),
            pl.BlockSpec((1, d), lambda i, j: (0, 0)),
            pl.BlockSpec((d, IN_COLS_PAD), lambda i, j: (0, 0)),
            pl.BlockSpec((1, RWKV_COLS), lambda i, j: (0, 0)),
            pl.BlockSpec((3, MLSTM_QKW), lambda i, j: (0, 0)),
            pl.BlockSpec((1, MLSTM_QKW), lambda i, j: (0, 0)),
        ],
        out_specs=[
            pl.BlockSpec((1, tm, RWKV_COLS), lambda i, j: (i, j, 0)),
            pl.BlockSpec((1, tm, ZM_COLS), lambda i, j: (i, j, 0)),
            pl.BlockSpec((1, tm, LANES), lambda i, j: (i, j, 0)),
            pl.BlockSpec((1, tm, ZB_COLS), lambda i, j: (i, j, 0)),
        ],
        out_shape=[
            jax.ShapeDtypeStruct((b, l, RWKV_COLS), ACT),
            jax.ShapeDtypeStruct((b, l, ZM_COLS), ACT),
            jax.ShapeDtypeStruct((b, l, LANES), F32),
            jax.ShapeDtypeStruct((b, l, ZB_COLS), ACT),
        ],
        compiler_params=pltpu.CompilerParams(
            dimension_semantics=("arbitrary", "arbitrary"), vmem_limit_bytes=VMEM_LIMIT),
        name="in_proj",
    )(x, x, x, modsel, norm_g.reshape(1, d), w_r, mu.reshape(1, RWKV_COLS), conv_w,
      conv_b.reshape(1, MLSTM_QKW))


def _rwkv_local_kernel(za_ref, kk_ref, ka_ref, rk_ref, w0_ref, w2_ref, a0_ref, a2_ref, g2_ref,
                       g_ref, yl_ref, at_ref, bs_ref, bonus_ref, gate_ref):
    c = CHUNK
    nd = RWKV_DIM
    hd = HEAD_DIM
    nh = RWKV_HEADS
    nb = za_ref.shape[0]
    ones_bd = _seg_ones(nd, hd)
    eye4 = _iota((hd, nd), 0) == (_iota((hd, nd), 1) % hd)
    eye4_f = eye4.astype(F32)
    row2 = _iota((2 * c, 2 * nd), 0)
    rr = row2 % c
    cc = _iota((2 * c, 2 * nd), 1) % c
    diag_ok = jnp.logical_and(cc == rr, row2 >= c)
    bd = lambda x: _block_diag(x, nh, hd, hd)
    units, pr, rhs, vb, pm, rm, qe, ke, gam, gmask = ([] for _ in range(10))
    for rb in range(nb):
        za = za_ref[rb].astype(F32)
        r = za[:, 0:nd]
        k = za[:, nd:2 * nd]
        v = za[:, 2 * nd:3 * nd]
        kkr = k * kk_ref[...]
        kk = kkr / jnp.maximum(jnp.sqrt(_segsum(kkr * kkr, ones_bd)), 1e-12)
        bonus_ref[rb] = (_segsum(r * k * rk_ref[...], ones_bd) * v).astype(bonus_ref.dtype)
        gd = za[:, 3 * nd + 2 * RWKV_W_LORA + 2 * RWKV_A_LORA:]
        gate_ref[rb] = _dot1(_sigmoid(gd), g2_ref[...]).astype(gate_ref.dtype)
        v_b = v.astype(BF16)
        for d in range(2):
            rev = d == 1
            wo = 3 * nd + d * RWKV_W_LORA
            ao = 3 * nd + 2 * RWKV_W_LORA + d * RWKV_A_LORA
            w_lo = _dot1(jnp.tanh(za[:, wo:wo + RWKV_W_LORA]), w2_ref[d])
            log_w = -_softplus(-(w0_ref[d] + w_lo)) - 0.5
            logdec = -jnp.exp(log_w)
            a = _sigmoid(a0_ref[d] + _dot1(za[:, ao:ao + RWKV_A_LORA], a2_ref[d]))
            kd = k * (1.0 + (a - 1.0) * ka_ref[...])
            bvec = kk * a
            cum = _dot_exact_lhs(_incl_mask(c, rev).astype(BF16), logdec)
            total = cum[0:1, :] if rev else cum[c - 1:c, :]
            e_neg = jnp.exp(-cum)
            e_end = jnp.exp(total - cum)
            pm_d = kk * jnp.exp(cum - logdec)
            rm_d = r * jnp.exp(cum)
            units.append((rb, d))
            pm.append(pm_d)
            rm.append(rm_d)
            pr.append(jnp.concatenate([pm_d, rm_d], axis=0).astype(BF16))
            rhs.append(jnp.concatenate([bd((bvec * e_neg).astype(BF16)),
                                        bd((kd * e_neg).astype(BF16))], axis=0))
            vb.append(v_b)
            qe.append((bvec * e_end).astype(BF16))
            ke.append((kd * e_end).astype(BF16))
            gam.append(jnp.exp(total))
            gmask.append(jnp.logical_or(cc > rr if rev else cc < rr, diag_ok))
    n = len(units)
    a4 = [jnp.where(gmask[i], _dg(pr[i], rhs[i], _NT), 0.0) for i in range(n)]
    a4b = [x.astype(BF16) for x in a4]
    pw = [_dg(a4b[i][:c, :nd], bd(a4b[i][:c, :nd]), _NN) for i in range(n)]
    lv = [_dg(a4b[i][:, nd:], bd(vb[i]), _NN) for i in range(n)]
    tinv = [eye4_f - a4[i][:c, :nd] for i in range(n)]
    covered = 2
    while True:
        pwd = [bd(x.astype(BF16)) for x in pw]
        tinv = [tinv[i] + _dg(tinv[i].astype(BF16), pwd[i], _NN) for i in range(n)]
        covered *= 2
        if covered >= c:
            break
        pw = [_dg(pw[i].astype(BF16), pwd[i], _NN) for i in range(n)]
    wz = [_dg(tinv[i].astype(BF16),
              jnp.concatenate([bd(pm[i].astype(BF16)), bd(lv[i][:c].astype(BF16))], axis=1),
              _NN).astype(BF16) for i in range(n)]
    awz = [_dg(a4b[i][c:, :nd],
               jnp.concatenate([bd(wz[i][:, :nd]), bd(wz[i][:, nd:])], axis=1), _NN)
           for i in range(n)]
    eye = _iota((hd, hd), 0) == _iota((hd, hd), 1)
    for i, (rb, d) in enumerate(units):
        g_ref[d, rb] = (rm[i] - awz[i][:, :nd]).astype(g_ref.dtype)
        yl_ref[d, rb] = (lv[i][c:] - awz[i][:, nd:]).astype(yl_ref.dtype)
        ats, bss = [], []
        for h in range(nh):
            sl = slice(h * hd, (h + 1) * hd)
            wz_h = jnp.concatenate([wz[i][:, sl], wz[i][:, nd + h * hd:nd + (h + 1) * hd]], axis=1)
            qwz = _dg(qe[i][:, sl], wz_h, _TN)
            ats.append(jnp.where(eye, jnp.broadcast_to(gam[i][:, sl], (hd, hd)), 0.0)
                       - qwz[:, :hd])
            bss.append(_dg(ke[i][:, sl], vb[i][:, sl], _TN) - qwz[:, hd:])
        at_ref[d, rb] = jnp.concatenate(ats, axis=1).astype(at_ref.dtype)
        bs_ref[d, rb] = jnp.concatenate(bss, axis=1).astype(bs_ref.dtype)


def _rwkv_local(za, k_k, k_a, r_k, w0, w2, a0, a2, g2, *, nb):
    b, l, _ = za.shape
    c = CHUNK
    nd = RWKV_DIM
    full = lambda shape: pl.BlockSpec(shape, lambda i, j: (0,) * len(shape))
    dir_spec = pl.BlockSpec((2, nb, c, nd), lambda i, j: (0, i, j, 0))
    dir_shape = jax.ShapeDtypeStruct((2, b, l, nd), ACT)
    row_spec = pl.BlockSpec((nb, c, nd), lambda i, j: (i, j, 0))
    row_shape = jax.ShapeDtypeStruct((b, l, nd), ACT)
    return pl.pallas_call(
        _rwkv_local_kernel,
        grid=(b // nb, l // c),
        in_specs=[
            pl.BlockSpec((nb, c, RWKV_COLS), lambda i, j: (i, j, 0)),
            full((1, nd)), full((1, nd)), full((1, nd)),
            full((2, 1, nd)), full((2, RWKV_W_LORA, nd)),
            full((2, 1, nd)), full((2, RWKV_A_LORA, nd)),
            full((RWKV_G_LORA, nd)),
        ],
        out_specs=[dir_spec] * 4 + [row_spec] * 2,
        out_shape=[dir_shape] * 4 + [row_shape] * 2,
        compiler_params=pltpu.CompilerParams(
            dimension_semantics=("arbitrary", "arbitrary"), vmem_limit_bytes=VMEM_LIMIT),
        name="rwkv_local",
    )(za, k_k.reshape(1, -1), k_a.reshape(1, -1), r_k.reshape(1, -1), w0.reshape(2, 1, nd),
      w2, a0.reshape(2, 1, nd), a2, g2)


def _rwkv_seq_kernel(gf_ref, ylf_ref, atf_ref, bsf_ref, gr_ref, ylr_ref, atr_ref, bsr_ref,
                     yf_ref, yr_ref, m_ref):
    nb = yf_ref.shape[0]

    @pl.when(pl.program_id(1) == 0)
    def _():
        m_ref[...] = jnp.zeros_like(m_ref)

    dirs = ((gf_ref, ylf_ref, atf_ref, bsf_ref, yf_ref), (gr_ref, ylr_ref, atr_ref, bsr_ref, yr_ref))
    units = [(rb, d) for rb in range(nb) for d in range(2)]
    m_bd = [_block_diag(m_ref[u].astype(BF16), RWKV_HEADS, HEAD_DIM, HEAD_DIM)
            for u in range(len(units))]
    y = [_dg(dirs[d][0][0, rb], m_bd[u], _NN) for u, (rb, d) in enumerate(units)]
    m_new = [_dg(dirs[d][2][0, rb], m_bd[u], _NN) for u, (rb, d) in enumerate(units)]
    for u, (rb, d) in enumerate(units):
        dirs[d][4][rb] = (y[u] + dirs[d][1][0, rb].astype(F32)).astype(dirs[d][4].dtype)
        m_ref[u] = m_new[u] + dirs[d][3][0, rb].astype(F32)


def _rwkv_seq(g, yl, at, bs, *, ctx_len, nb):
    _, b, l, nd = g.shape
    c = CHUNK
    n_chunks = l // c
    n_ctx_chunks = ctx_len // c
    rev_idx = lambda j: _chunk_index(j, n_ctx_chunks, n_chunks, True)
    fwd = pl.BlockSpec((1, nb, c, nd), lambda i, j: (0, i, j, 0))
    bwd = pl.BlockSpec((1, nb, c, nd), lambda i, j: (1, i, rev_idx(j), 0))
    y_shape = jax.ShapeDtypeStruct((b, l, nd), ACT)
    return pl.pallas_call(
        _rwkv_seq_kernel,
        grid=(b // nb, n_chunks),
        in_specs=[fwd] * 4 + [bwd] * 4,
        out_specs=[pl.BlockSpec((nb, c, nd), lambda i, j: (i, j, 0)),
                   pl.BlockSpec((nb, c, nd), lambda i, j: (i, rev_idx(j), 0))],
        out_shape=[y_shape, y_shape],
        scratch_shapes=[pltpu.VMEM((nb * 2, HEAD_DIM, RWKV_DIM), F32)],
        compiler_params=pltpu.CompilerParams(
            dimension_semantics=("arbitrary", "arbitrary"), vmem_limit_bytes=VMEM_LIMIT),
        name="rwkv_seq",
    )(g, yl, at, bs, g, yl, at, bs)


def _cummax_rows(x, rev):
    n = x.shape[0]
    row = _iota(x.shape, 0)
    shift = 1
    while shift < n:
        if rev:
            moved = jnp.where(row >= n - shift, -jnp.inf, pltpu.roll(x, n - shift, 0))
        else:
            moved = jnp.where(row < shift, -jnp.inf, pltpu.roll(x, shift, 0))
        x = jnp.maximum(x, moved)
        shift *= 2
    return x


def _head_lane(d):
    return d * 2 * MLSTM_HEADS + MLSTM_HEADS


def _sel_expand(d, width):
    n = MLSTM_HEADS * width
    return (_iota((LANES, n), 0) == _head_lane(d) + _iota((LANES, n), 1) // width).astype(BF16)


def _sel_reduce(d, width):
    n = MLSTM_HEADS * width
    return (_iota((n, LANES), 1) == _head_lane(d) + _iota((n, LANES), 0) // width).astype(BF16)


def _block_diag(x, reps, row_block, col_block):
    t = jnp.concatenate([x] * reps, axis=0)
    keep = (_iota(t.shape, 0) // row_block) == (_iota(t.shape, 1) // col_block)
    return jnp.where(keep, t, jnp.zeros_like(t))


def _mlstm_local_kernel(zm_ref, zg_ref, gb_ref, num_ref, den_ref, mi_ref, b_ref, cl_ref,
                        rows_ref):
    c = CHUNK
    nh = MLSTM_HEADS
    dk = MLSTM_QK
    dv = MLSTM_V
    nb = zm_ref.shape[0]
    lane_j = _iota((c, nh * c), 1) % c
    row_s = _iota((c, nh * c), 0)
    cl_keep = (_iota((nh * dk, MLSTM_DIM), 0) // dk) == (_iota((nh * dk, MLSTM_DIM), 1) // dv)
    sel_c = [_sel_expand(d, c) for d in range(2)]
    sel_k = [_sel_expand(d, dk) for d in range(2)]
    sel_r = [_sel_reduce(d, c) for d in range(2)]
    k, vb, log_f, li, qk, vbd = [], [], [], [], [], []
    for rb in range(nb):
        zm = zm_ref[rb]
        qb = (zm[:, :nh * dk].astype(F32) * (dk ** -0.5)).astype(BF16)
        k.append(zm[:, nh * dk:MLSTM_QKW].astype(F32))
        vb.append(zm[:, MLSTM_QKW:MLSTM_QKW + MLSTM_DIM].astype(BF16))
        gates = zg_ref[rb] + gb_ref[...]
        capped = GATE_SOFTCAP * jnp.tanh(gates * (1.0 / GATE_SOFTCAP))
        log_f.append(-_softplus(-capped))
        li.append(pltpu.roll(capped, nh, 1))
        qk.append(_dg(qb, _block_diag(k[rb].astype(BF16), nh, c, dk), _NT))
        vbd.append(_block_diag(vb[rb], nh, c, dv))
    units = [(rb, d) for rb in range(nb) for d in range(2)]
    b_all = [_dot_exact_lhs(_incl_mask(c, d == 1).astype(BF16), log_f[rb]) for rb, d in units]
    x = [li[rb] - b_all[u] for u, (rb, d) in enumerate(units)]
    cm = [_cummax_rows(x[u], d == 1) for u, (rb, d) in enumerate(units)]
    e1 = [_dot_exact_rhs(-cm[u], sel_c[d]) for u, (rb, d) in enumerate(units)]
    xe = [_dot_exact_rhs(x[u], sel_c[d]) for u, (rb, d) in enumerate(units)]
    g_row, m_loc, e32 = [], [], []
    for u, (rb, d) in enumerate(units):
        last = 0 if d == 1 else c - 1
        g_row.append(b_all[u][last:last + 1, :])
        w_end = g_row[u] - b_all[u] + li[rb]
        m_loc.append(jnp.max(w_end, axis=0, keepdims=True))
        e32.append(_dot2_rhs01(jnp.exp(w_end - m_loc[u]), sel_k[d]))
    wi = []
    for u, (rb, d) in enumerate(units):
        x_row = jnp.sum(jnp.where(row_s == lane_j, xe[u], 0.0), axis=0, keepdims=True)
        earlier = (lane_j >= row_s) if d == 1 else (lane_j <= row_s)
        wi.append((qk[rb] * jnp.exp(jnp.where(earlier, e1[u] + x_row, -jnp.inf))).astype(BF16))
    nd = [_dg(wi[u], jnp.concatenate([vbd[rb], sel_r[d]], axis=1), _NN)
          for u, (rb, d) in enumerate(units)]
    ke = [k[rb] * e32[u] for u, (rb, d) in enumerate(units)]
    full = [_dg(ke[u].astype(BF16), vb[rb], _TN) for u, (rb, d) in enumerate(units)]
    for u, (rb, d) in enumerate(units):
        num_ref[d, rb] = nd[u][:, :MLSTM_DIM].astype(num_ref.dtype)
        den_ref[d, rb] = nd[u][:, MLSTM_DIM:]
        mi_ref[d, rb] = b_all[u] + cm[u]
        b_ref[d, rb] = b_all[u]
        f = jnp.where(cl_keep, full[u], 0.0)
        cl_ref[d, rb] = ((f[0:dk] + f[dk:2 * dk])
                         + (f[2 * dk:3 * dk] + f[3 * dk:4 * dk])).astype(cl_ref.dtype)
        rows_ref[d, rb] = jnp.concatenate(
            [jnp.sum(ke[u], axis=0, keepdims=True), m_loc[u], g_row[u],
             jnp.zeros((SUBLANES - 3, LANES), F32)], axis=0)


def _mlstm_local(zm, zg, gate_bias, *, nb):
    b, l, _ = zm.shape
    c = CHUNK
    n_chunks = l // c
    spec = lambda rows, w, dt: pl.BlockSpec((2, nb, rows, w), lambda i, j: (0, i, j, 0))
    shape = lambda rows, w, dt: jax.ShapeDtypeStruct((2, b, n_chunks * rows, w), dt)
    outs = [(c, MLSTM_DIM, ACT), (c, LANES, F32), (c, LANES, F32), (c, LANES, F32),
            (MLSTM_QK, MLSTM_DIM, ACT), (SUBLANES, LANES, F32)]
    return pl.pallas_call(
        _mlstm_local_kernel,
        grid=(b // nb, n_chunks),
        in_specs=[
            pl.BlockSpec((nb, c, ZM_COLS), lambda i, j: (i, j, 0)),
            pl.BlockSpec((nb, c, LANES), lambda i, j: (i, j, 0)),
            pl.BlockSpec((1, LANES), lambda i, j: (0, 0)),
        ],
        out_specs=[spec(*o) for o in outs],
        out_shape=[shape(*o) for o in outs],
        compiler_params=pltpu.CompilerParams(
            dimension_semantics=("arbitrary", "arbitrary"), vmem_limit_bytes=VMEM_LIMIT),
        name="mlstm_local",
    )(zm, zg, gate_bias)


def _mlstm_seq_kernel(qf_ref, numf_ref, denf_ref, mif_ref, bf_ref, clf_ref, rowsf_ref,
                      qr_ref, numr_ref, denr_ref, mir_ref, br_ref, clr_ref, rowsr_ref,
                      hf_ref, hr_ref, cbd_ref, n_ref, m_ref):
    c = CHUNK
    nh = MLSTM_HEADS
    dk = MLSTM_QK
    dv = MLSTM_V
    nb = hf_ref.shape[0]

    @pl.when(pl.program_id(1) == 0)
    def _():
        cbd_ref[...] = jnp.zeros_like(cbd_ref)
        n_ref[...] = jnp.zeros_like(n_ref)
        m_ref[...] = jnp.zeros_like(m_ref)

    dirs = ((qf_ref, numf_ref, denf_ref, mif_ref, bf_ref, clf_ref, rowsf_ref, hf_ref),
            (qr_ref, numr_ref, denr_ref, mir_ref, br_ref, clr_ref, rowsr_ref, hr_ref))
    units = [(rb, d) for rb in range(nb) for d in range(2)]
    nu = len(units)
    sel_v = [_sel_expand(d, dv) for d in range(2)]
    sel_k = [_sel_expand(d, dk) for d in range(2)]
    sel_r = [_sel_reduce(d, dk) for d in range(2)]
    q = [dirs[d][0][rb].astype(F32) * (dk ** -0.5) for rb, d in units]
    rows = [dirs[d][6][0, rb] for rb, d in units]
    m_row = [m_ref[u] for u in range(nu)]
    n_row = [n_ref[u] for u in range(nu)]
    cbd = [cbd_ref[u] for u in range(nu)]
    qn = [_dot2_rhs01(q[u] * n_row[u], sel_r[d]) for u, (rb, d) in enumerate(units)]
    qc = [_dg(q[u].astype(BF16), cbd[u].astype(BF16), _NN) for u in range(nu)]
    a12, srow = [], []
    for u, (rb, d) in enumerate(units):
        mi = dirs[d][3][0, rb]
        log_inter = dirs[d][4][0, rb] + m_row[u]
        m_out = jnp.maximum(log_inter, mi)
        s_intra = jnp.exp(mi - m_out)
        s_inter = jnp.exp(log_inter - m_out)
        den = s_intra * dirs[d][2][0, rb] + s_inter * qn[u]
        dinv = 1.0 / jnp.maximum(jnp.abs(den), jnp.exp(-m_out))
        a12.append(jnp.concatenate([s_intra * dinv, s_inter * dinv], axis=0))
        m_loc, g_row = rows[u][1:2], rows[u][2:3]
        m_new = jnp.maximum(g_row + m_row[u], m_loc)
        srow.append(jnp.concatenate(
            [jnp.exp(g_row + m_row[u] - m_new), jnp.exp(m_loc - m_new),
             jnp.zeros((SUBLANES - 2, LANES), F32)], axis=0))
        m_ref[u] = m_new
    a12e = [_dot2_rhs01(a12[u], sel_v[d]) for u, (rb, d) in enumerate(units)]
    s_v = [_dot2_rhs01(srow[u], sel_v[d]) for u, (rb, d) in enumerate(units)]
    s_k = [_dot2_rhs01(srow[u], sel_k[d]) for u, (rb, d) in enumerate(units)]
    for u, (rb, d) in enumerate(units):
        h_out = a12e[u][:c] * dirs[d][1][0, rb].astype(F32) + a12e[u][c:] * qc[u]
        dirs[d][7][rb] = h_out.astype(dirs[d][7].dtype)
        cl_full = _block_diag(dirs[d][5][0, rb].astype(F32), nh, dk, dv)
        cbd_ref[u] = s_v[u][0:1] * cbd[u] + s_v[u][1:2] * cl_full
        n_ref[u] = s_k[u][0:1] * n_row[u] + s_k[u][1:2] * rows[u][0:1]


def _mlstm_seq(zm, num, den, mi, bsum, cl, rows, *, ctx_len, nb):
    b, l, _ = zm.shape
    c = CHUNK
    n_chunks = l // c
    n_ctx_chunks = ctx_len // c
    rev_idx = lambda j: _chunk_index(j, n_ctx_chunks, n_chunks, True)

    def specs(d):
        cidx = (lambda j: j) if d == 0 else rev_idx
        blk = lambda r, w: pl.BlockSpec((1, nb, r, w), lambda i, j: (d, i, cidx(j), 0))
        return [pl.BlockSpec((nb, c, MLSTM_HEADS * MLSTM_QK), lambda i, j: (i, cidx(j), 0)),
                blk(c, MLSTM_DIM), blk(c, LANES), blk(c, LANES), blk(c, LANES),
                blk(MLSTM_QK, MLSTM_DIM), blk(SUBLANES, LANES)]

    h_shape = jax.ShapeDtypeStruct((b, l, MLSTM_DIM), ACT)
    args = (zm, num, den, mi, bsum, cl, rows)
    return pl.pallas_call(
        _mlstm_seq_kernel,
        grid=(b // nb, n_chunks),
        in_specs=specs(0) + specs(1),
        out_specs=[pl.BlockSpec((nb, c, MLSTM_DIM), lambda i, j: (i, j, 0)),
                   pl.BlockSpec((nb, c, MLSTM_DIM), lambda i, j: (i, rev_idx(j), 0))],
        out_shape=[h_shape, h_shape],
        scratch_shapes=[
            pltpu.VMEM((2 * nb, MLSTM_HEADS * MLSTM_QK, MLSTM_DIM), F32),
            pltpu.VMEM((2 * nb, 1, LANES), F32),
            pltpu.VMEM((2 * nb, 1, LANES), F32),
        ],
        compiler_params=pltpu.CompilerParams(
            dimension_semantics=("arbitrary", "arbitrary"), vmem_limit_bytes=VMEM_LIMIT),
        name="mlstm_seq",
    )(*args, *args)


def _rope_rot(x):
    lane = _iota(x.shape, 1)
    half = MLA_ROPE // 2
    return jnp.where(lane < MLA_NOPE + half, -pltpu.roll(x, LANES - half, 1),
                     pltpu.roll(x, half, 1))


def _mla_proj_kernel(zb_ref, cos_ref, sin_ref, qg_ref, kvg_ref, gq_ref, gk_ref,
                     wq_ref, wk_ref, wv_ref, q_ref, k_ref, v_ref):
    zb = zb_ref[0].astype(F32)
    cq = zb[:, :MLA_Q_RANK]
    ckv = zb[:, MLA_Q_RANK:MLA_Q_RANK + MLA_KV_RANK]
    kr = zb[:, MLA_Q_RANK + MLA_KV_RANK:]
    cos = cos_ref[...]
    sin = sin_ref[...]
    cqn = cq * lax.rsqrt(jnp.mean(cq * cq, axis=-1, keepdims=True) + NORM_EPS) * qg_ref[...]
    ckn = ckv * lax.rsqrt(jnp.mean(ckv * ckv, axis=-1, keepdims=True) + NORM_EPS) * kvg_ref[...]
    q_all = _dot1(cqn, wq_ref[...])
    k_all = _dot1(ckn, wk_ref[...])
    v_ref[0] = _dot1(ckn, wv_ref[...]).astype(BF16)
    scale = MLA_QK ** -0.5
    for h in range(MLA_HEADS):
        sl = slice(h * QPAD, (h + 1) * QPAD)
        qh = q_all[:, sl]
        qn = qh * lax.rsqrt(jnp.sum(qh * qh, axis=-1, keepdims=True) * (1.0 / MLA_QK) + NORM_EPS)
        qn = qn * gq_ref[...]
        q_ref[0, :, sl] = ((qn * cos + _rope_rot(qn) * sin) * scale).astype(BF16)
        kh = k_all[:, sl] + kr
        kn = kh * lax.rsqrt(jnp.sum(kh * kh, axis=-1, keepdims=True) * (1.0 / MLA_QK) + NORM_EPS)
        kn = kn * gk_ref[...]
        k_ref[0, :, sl] = (kn * cos + _rope_rot(kn) * sin).astype(BF16)


def _mla_proj(zb, cos_f, sin_f, q_norm_g, kv_norm_g, gq, gk, wq_r, wk_r, wv_r, *, tm):
    b, l, _ = zb.shape
    hq = MLA_HEADS * QPAD
    full = lambda shape: pl.BlockSpec(shape, lambda i, j: (0,) * len(shape))
    return pl.pallas_call(
        _mla_proj_kernel,
        grid=(b, l // tm),
        in_specs=[
            pl.BlockSpec((1, tm, ZB_COLS), lambda i, j: (i, j, 0)),
            pl.BlockSpec((tm, QPAD), lambda i, j: (j, 0)),
            pl.BlockSpec((tm, QPAD), lambda i, j: (j, 0)),
            full((1, MLA_Q_RANK)), full((1, MLA_KV_RANK)), full((1, QPAD)), full((1, QPAD)),
            full((MLA_Q_RANK, hq)), full((MLA_KV_RANK, hq)), full((MLA_KV_RANK, MLA_DIM)),
        ],
        out_specs=[
            pl.BlockSpec((1, tm, hq), lambda i, j: (i, j, 0)),
            pl.BlockSpec((1, tm, hq), lambda i, j: (i, j, 0)),
            pl.BlockSpec((1, tm, MLA_DIM), lambda i, j: (i, j, 0)),
        ],
        out_shape=[
            jax.ShapeDtypeStruct((b, l, hq), BF16),
            jax.ShapeDtypeStruct((b, l, hq), BF16),
            jax.ShapeDtypeStruct((b, l, MLA_DIM), BF16),
        ],
        compiler_params=pltpu.CompilerParams(
            dimension_semantics=("arbitrary", "arbitrary"), vmem_limit_bytes=VMEM_LIMIT),
        name="mla_proj",
    )(zb, cos_f, sin_f, q_norm_g.reshape(1, -1), kv_norm_g.reshape(1, -1), gq, gk,
      wq_r, wk_r, wv_r)


def _attend_tile(q_ref, k_ref, v_ref, o_ref, n_blocks):
    tq = q_ref.shape[1]
    bk = KV_BLOCK
    lane_v = _iota((bk, 2 * MLA_V), 1)
    lane_o = _iota((tq, 2 * MLA_V), 1)
    sum_lane = [(1 - h) * MLA_V for h in range(2)]
    q = [q_ref[0, :, h * QPAD:(h + 1) * QPAD] for h in range(2)]

    def scores(j):
        return [_dg(q[h], k_ref[0, j * bk:(j + 1) * bk, h * QPAD:(h + 1) * QPAD], _NT)
                for h in range(2)]

    m = [jnp.full((tq, 1), -jnp.inf, F32) for _ in range(2)]
    acc = [jnp.zeros((tq, 2 * MLA_V), F32) for _ in range(2)]
    s = scores(0)
    for j in range(n_blocks):
        s_next = scores(j + 1) if j + 1 < n_blocks else None
        v = v_ref[0, j * bk:(j + 1) * bk, :]
        for h in range(2):
            ones = (lane_v == sum_lane[h]).astype(BF16)
            vh = jnp.where((lane_v // MLA_V) == h, v, ones)
            m_new = jnp.maximum(m[h], jnp.max(s[h], axis=-1, keepdims=True))
            p = jnp.exp(s[h] - m_new).astype(BF16)
            acc[h] = jnp.exp(m[h] - m_new) * acc[h] + _dg(p, vh, _NN)
            m[h] = m_new
        s = s_next
    out = jnp.zeros((tq, 2 * MLA_V), F32)
    for h in range(2):
        denom = jnp.sum(jnp.where(lane_o == sum_lane[h], acc[h], 0.0), axis=-1, keepdims=True)
        out = out + jnp.where((lane_o // MLA_V) == h, acc[h], 0.0) / denom
    o_ref[0] = out.astype(o_ref.dtype)


def _mla_attn_kernel(q_ref, k_ref, v_ref, o_ref, *, ctx_len, first_tile, n_ctx_tiles):
    n_all = k_ref.shape[1] // KV_BLOCK
    if first_tile >= n_ctx_tiles:
        _attend_tile(q_ref, k_ref, v_ref, o_ref, n_all)
        return
    is_ctx = (pl.program_id(2) + first_tile) < n_ctx_tiles

    @pl.when(is_ctx)
    def _():
        _attend_tile(q_ref, k_ref, v_ref, o_ref, ctx_len // KV_BLOCK)

    @pl.when(jnp.logical_not(is_ctx))
    def _():
        _attend_tile(q_ref, k_ref, v_ref, o_ref, n_all)


def _mla_attn(q, k, v, *, tq, ctx_len, first_tile):
    b, l, _ = q.shape
    assert l % KV_BLOCK == 0 and ctx_len % KV_BLOCK == 0
    n_q = l // tq - first_tile
    pairs = MLA_HEADS // 2
    kern = functools.partial(_mla_attn_kernel, ctx_len=ctx_len, first_tile=first_tile,
                             n_ctx_tiles=ctx_len // tq)
    return pl.pallas_call(
        kern,
        grid=(b, pairs, n_q),
        in_specs=[
            pl.BlockSpec((1, tq, 2 * QPAD), lambda i, p, j: (i, j + first_tile, p)),
            pl.BlockSpec((1, l, 2 * QPAD), lambda i, p, j: (i, 0, p)),
            pl.BlockSpec((1, l, 2 * MLA_V), lambda i, p, j: (i, 0, p)),
        ],
        out_specs=pl.BlockSpec((1, tq, 2 * MLA_V), lambda i, p, j: (i, j + first_tile, p)),
        out_shape=jax.ShapeDtypeStruct((b, l, MLA_DIM), ACT),
        compiler_params=pltpu.CompilerParams(
            dimension_semantics=("arbitrary", "arbitrary", "arbitrary"),
            vmem_limit_bytes=VMEM_LIMIT),
        name="mla_attn",
    )(q, k, v)


def _outproj_kernel(x_ref, yf_ref, yr_ref, bonus_ref, gate_ref, attn_ref, hf_ref, hr_ref,
                    o_ref, mod_ref, lng_ref, lnb_ref, mng_ref, wa_ref, wb_ref, wm_ref, out_ref):
    ones_bd = _seg_ones(RWKV_DIM, HEAD_DIM)
    inv = 1.0 / HEAD_DIM
    y = yf_ref[0].astype(F32) + yr_ref[0].astype(F32)
    mean = _segsum(y, ones_bd) * inv
    yc = y - mean
    var = _segsum(yc * yc, ones_bd) * inv
    yn = yc * lax.rsqrt(var + RWKV_GN_EPS) * lng_ref[...] + lnb_ref[...]
    ya = (yn + bonus_ref[0].astype(F32)) * gate_ref[0].astype(F32)
    hm = hf_ref[0].astype(F32) + hr_ref[0].astype(F32)
    hn = hm * lax.rsqrt(_segsum(hm * hm, ones_bd) * inv + NORM_EPS)
    ym = hn * mng_ref[...] * _sigmoid(o_ref[0].astype(F32))
    mix = (_dot1(ya, wa_ref[...]) + _dot1(attn_ref[0], wb_ref[...]) + _dot1(ym, wm_ref[...]))
    out_ref[0] = x_ref[0] + mod_ref[0, 0] * mix


def _out_proj(x, yf, yr, bonus, gate, attn, hf, hr, zm, modsel, ln_g, ln_b, mnorm_g,
              wa, wb, wm, *, tm, ctx_len, first_tile):
    b, l, d = x.shape
    n_ctx_tiles = ctx_len // tm
    n_tiles = l // tm - first_tile
    row = lambda w: pl.BlockSpec((1, tm, w), lambda i, j: (i, j + first_tile, 0))
    full = lambda shape: pl.BlockSpec(shape, lambda i, j: (0,) * len(shape))
    seg = lambda j: jnp.where(j + first_tile >= n_ctx_tiles, 1, 0)
    o_block = (MLSTM_QKW + MLSTM_DIM) // MLSTM_DIM
    return pl.pallas_call(
        _outproj_kernel,
        grid=(b, n_tiles),
        in_specs=[
            row(d), row(RWKV_DIM), row(RWKV_DIM), row(RWKV_DIM), row(RWKV_DIM), row(MLA_DIM),
            row(MLSTM_DIM), row(MLSTM_DIM),
            pl.BlockSpec((1, tm, MLSTM_DIM), lambda i, j: (i, j + first_tile, o_block)),
            pl.BlockSpec((1, 1, 1, d), lambda i, j: (i, seg(j), 0, 0)),
            full((1, RWKV_DIM)), full((1, RWKV_DIM)), full((1, MLSTM_DIM)),
            full((RWKV_DIM, d)), full((MLA_DIM, d)), full((MLSTM_DIM, d)),
        ],
        out_specs=pl.BlockSpec((1, tm, d), lambda i, j: (i, j, 0)),
        out_shape=jax.ShapeDtypeStruct((b, n_tiles * tm, d), F32),
        compiler_params=pltpu.CompilerParams(
            dimension_semantics=("arbitrary", "arbitrary"), vmem_limit_bytes=VMEM_LIMIT),
        name="out_proj",
    )(x, yf, yr, bonus, gate, attn, hf, hr, zm, modsel, ln_g.reshape(1, -1),
      ln_b.reshape(1, -1), mnorm_g.reshape(1, -1), wa, wb, wm)


def _ffn_kernel(x_ref, mod_ref, g_ref, w1_ref, w2_ref, out_ref, *, hidden):
    x = x_ref[0]
    h = _rmsnorm_mod(x, g_ref[...], mod_ref[0, 0, 0:1, :], mod_ref[0, 0, 1:2, :]).astype(BF16)
    gu = _dg(h, w1_ref[...], _NN)
    act = (_silu(gu[:, :hidden]) * gu[:, hidden:]).astype(BF16)
    out_ref[0] = x + mod_ref[0, 0, 2:3, :] * _dg(act, w2_ref[...], _NN)


def _ffn(x, modsel, norm_g, w1, w2, *, tm, n_ctx_tiles_here):
    b, l, d = x.shape
    hidden = w2.shape[0]
    seg = lambda j: jnp.where(j >= n_ctx_tiles_here, 1, 0)
    return pl.pallas_call(
        functools.partial(_ffn_kernel, hidden=hidden),
        grid=(b, l // tm),
        in_specs=[
            pl.BlockSpec((1, tm, d), lambda i, j: (i, j, 0)),
            pl.BlockSpec((1, 1, 3, d), lambda i, j: (i, seg(j), 0, 0)),
            pl.BlockSpec((1, d), lambda i, j: (0, 0)),
            pl.BlockSpec((d, 2 * hidden), lambda i, j: (0, 0)),
            pl.BlockSpec((hidden, d), lambda i, j: (0, 0)),
        ],
        out_specs=pl.BlockSpec((1, tm, d), lambda i, j: (i, j, 0)),
        out_shape=jax.ShapeDtypeStruct((b, l, d), F32),
        compiler_params=pltpu.CompilerParams(
            dimension_semantics=("arbitrary", "arbitrary"), vmem_limit_bytes=VMEM_LIMIT),
        name="ffn",
    )(x, modsel, norm_g.reshape(1, d), w1, w2)


def _arrange_w_in(w_in):
    d = w_in.shape[0]
    wa = w_in[:, :RWKV_COLS]
    wb = w_in[:, RWKV_COLS:RWKV_COLS + MLA_COLS]
    wm = w_in[:, RWKV_COLS + MLA_COLS:]
    gates = wm[:, MLSTM_QKW + 2 * MLSTM_DIM:]
    gate_blk = jnp.pad(gates, ((0, 0), (0, LANES - gates.shape[1])))
    kr = wb[:, MLA_Q_RANK + MLA_KV_RANK:]
    kr_blk = jnp.pad(kr, ((0, 0), (MLA_NOPE, LANES - MLA_NOPE - MLA_ROPE)))
    out = jnp.concatenate([wa, wm[:, :MLSTM_QKW + 2 * MLSTM_DIM], gate_blk,
                           wb[:, :MLA_Q_RANK + MLA_KV_RANK], kr_blk], axis=1)
    assert out.shape == (d, IN_COLS_PAD)
    return out.astype(BF16)


def _arrange_mla_weights(w_uq, w_ukv):
    rq = w_uq.shape[0]
    wq = w_uq.reshape(rq, MLA_HEADS, MLA_QK)
    wq = jnp.pad(wq, ((0, 0), (0, 0), (0, QPAD - MLA_QK))).reshape(rq, MLA_HEADS * QPAD)
    rk = w_ukv.shape[0]
    wkv = w_ukv.reshape(rk, MLA_HEADS, MLA_NOPE + MLA_V)
    wk = jnp.pad(wkv[:, :, :MLA_NOPE], ((0, 0), (0, 0), (0, QPAD - MLA_NOPE)))
    wk = wk.reshape(rk, MLA_HEADS * QPAD)
    wv = wkv[:, :, MLA_NOPE:].reshape(rk, MLA_DIM)
    return wq.astype(BF16), wk.astype(BF16), wv.astype(BF16)


def _rope_tables(seq_len, ctx_len):
    rows = seq_len // GRID_W
    row = jnp.repeat(jnp.arange(rows, dtype=F32), GRID_W)
    col = jnp.tile(jnp.arange(GRID_W, dtype=F32), rows)
    n_freq = MLA_ROPE // 4
    inv = jnp.power(ROPE_BASE, -jnp.arange(n_freq, dtype=F32) / n_freq)
    ang = jnp.concatenate([row[:, None] * inv, col[:, None] * inv], axis=-1)
    cos, sin = jnp.cos(ang), jnp.sin(ang)
    pad_l, pad_r = MLA_NOPE, QPAD - MLA_QK
    cos_f = jnp.concatenate([jnp.ones((seq_len, pad_l), F32), cos, cos,
                             jnp.ones((seq_len, pad_r), F32)], axis=1)
    sin_f = jnp.concatenate([jnp.zeros((seq_len, pad_l), F32), sin, sin,
                             jnp.zeros((seq_len, pad_r), F32)], axis=1)
    cos_f = jnp.concatenate([jnp.ones((ctx_len, QPAD), F32), cos_f], axis=0)
    sin_f = jnp.concatenate([jnp.zeros((ctx_len, QPAD), F32), sin_f], axis=0)
    return cos_f, sin_f


def _pad_gain(g):
    return jnp.pad(g, (0, QPAD - g.shape[0])).reshape(1, QPAD)


def _gate_bias(i_b, f_b):
    gb = jnp.stack([i_b, f_b], axis=1).reshape(-1)
    return jnp.pad(gb, (0, LANES - gb.shape[0])).reshape(1, LANES)


def kernel(x, c, ctx, c_ctx, mod_w, mod_b, norm1_g, norm2_g, w_in, w_out, ffn_w_in, ffn_w_out, rwkv_mu, rwkv_w0, rwkv_w2, rwkv_a0, rwkv_a2, rwkv_g2, rwkv_k_k, rwkv_k_a, rwkv_r_k, rwkv_ln_g, rwkv_ln_b, mla_q_norm_g, mla_w_uq, mla_kv_norm_g, mla_w_ukv, mla_q_qknorm_g, mla_k_qknorm_g, mlstm_conv_w, mlstm_conv_b, mlstm_i_b, mlstm_f_b, mlstm_norm_g):
    bsz, seq, d = x.shape
    ctx_len = ctx.shape[1]
    depth = mod_w.shape[0]
    tm = min(256, ctx_len)
    assert ctx_len % tm == 0 and seq % tm == 0 and ctx_len % CHUNK == 0 and seq % CHUNK == 0
    nb = ROWS_PER_STEP if bsz % ROWS_PER_STEP == 0 else 1

    xu = jnp.concatenate([ctx, x], axis=1)
    n_ctx_tiles = ctx_len // tm

    rows = -(-(bsz + 1) // SUBLANES) * SUBLANES
    cvec = jnp.zeros((rows, d), F32).at[:bsz].set(c).at[bsz].set(c_ctx)
    mod = _modulation(cvec, mod_w, mod_b).reshape(depth, rows, 6, d)
    mod_lat = mod[:, :bsz]
    mod_ctx = jnp.broadcast_to(mod[:, bsz][:, None], mod_lat.shape)
    modsel = jnp.stack([mod_ctx, mod_lat], axis=2)

    cos_f, sin_f = _rope_tables(seq, ctx_len)

    for i in range(depth):
        last = i == depth - 1
        first_tile = n_ctx_tiles if last else 0
        w_r = _arrange_w_in(w_in[i])
        za, zm, zg, zb = _in_proj(xu, modsel[i, :, :, 0:2], norm1_g[i], w_r, rwkv_mu[i],
                              mlstm_conv_w[i], mlstm_conv_b[i], tm=tm, ctx_len=ctx_len)
        g_loc, y_loc, a_tr, b_st, bonus, gate = _rwkv_local(
            za, rwkv_k_k[i], rwkv_k_a[i], rwkv_r_k[i], rwkv_w0[i], rwkv_w2[i], rwkv_a0[i],
            rwkv_a2[i], rwkv_g2[i], nb=nb)
        yf, yr = _rwkv_seq(g_loc, y_loc, a_tr, b_st, ctx_len=ctx_len, nb=nb)
        wq_r, wk_r, wv_r = _arrange_mla_weights(mla_w_uq[i], mla_w_ukv[i])
        q, k, v = _mla_proj(zb, cos_f, sin_f, mla_q_norm_g[i], mla_kv_norm_g[i],
                            _pad_gain(mla_q_qknorm_g[i]), _pad_gain(mla_k_qknorm_g[i]),
                            wq_r, wk_r, wv_r, tm=tm)
        attn = _mla_attn(q, k, v, tq=tm, ctx_len=ctx_len, first_tile=first_tile)
        gb = _gate_bias(mlstm_i_b[i], mlstm_f_b[i])
        hf, hr = _mlstm_seq(zm, *_mlstm_local(zm, zg, gb, nb=nb), ctx_len=ctx_len, nb=nb)
        wo = w_out[i].astype(BF16)
        xu = _out_proj(xu, yf, yr, bonus, gate, attn, hf, hr, zm, modsel[i, :, :, 2:3],
                       rwkv_ln_g[i], rwkv_ln_b[i], mlstm_norm_g[i],
                       wo[:RWKV_DIM], wo[RWKV_DIM:RWKV_DIM + MLA_DIM], wo[RWKV_DIM + MLA_DIM:],
                       tm=tm, ctx_len=ctx_len, first_tile=first_tile)
        xu = _ffn(xu, modsel[i, :, :, 3:6], norm2_g[i], ffn_w_in[i].astype(BF16),
                  ffn_w_out[i].astype(BF16), tm=tm,
                  n_ctx_tiles_here=0 if last else n_ctx_tiles)
    return xu
```

```python
import functools

import jax
import jax.numpy as jnp
from jax import lax
from jax.experimental import pallas as pl
from jax.experimental.pallas import tpu as pltpu

F32 = jnp.float32
BF16 = jnp.bfloat16

HEAD_DIM = 64
NORM_EPS = 1e-6
GRID_W = 64
ROPE_BASE = 10000.0
LOG2_E = 1.4426950408889634
RWKV_HEADS = 4
RWKV_DIM = RWKV_HEADS * HEAD_DIM
RWKV_W_LORA = 64
RWKV_A_LORA = 64
RWKV_G_LORA = 128
RWKV_COLS = 3 * RWKV_DIM + 2 * RWKV_W_LORA + 2 * RWKV_A_LORA + RWKV_G_LORA
RWKV_GN_EPS = 64e-5
MLA_HEADS = 8
MLA_NOPE = 64
MLA_ROPE = 32
MLA_V = HEAD_DIM
MLA_QK = MLA_NOPE + MLA_ROPE
MLA_Q_RANK = 512
MLA_KV_RANK = 256
MLA_DIM = MLA_HEADS * MLA_V
MLA_COLS = MLA_Q_RANK + MLA_KV_RANK + MLA_ROPE
MLSTM_HEADS = 4
MLSTM_QK = 32
MLSTM_V = HEAD_DIM
MLSTM_DIM = MLSTM_HEADS * MLSTM_V
MLSTM_QKW = 2 * MLSTM_HEADS * MLSTM_QK
GATE_SOFTCAP = 15.0
MLSTM_COLS = MLSTM_QKW + 2 * MLSTM_DIM + 4 * MLSTM_HEADS

LANES = 128
SUBLANES = 8
CHUNK = HEAD_DIM
ROWS_PER_STEP = 4
SEQ_ROWS_PER_STEP = 8
KV_BLOCK = 256
QPAD = LANES
ZB_COLS = MLA_Q_RANK + MLA_KV_RANK + LANES
ZM_COLS = MLSTM_QKW + 2 * MLSTM_DIM
SHIFT_COLS = RWKV_COLS + MLSTM_QKW
IN_COLS_PAD = RWKV_COLS + ZM_COLS + LANES + ZB_COLS
ACT = BF16
VMEM_LIMIT = 56 * 1024 * 1024


def _split(a):
    hi = a.astype(BF16)
    lo = (a - hi.astype(F32)).astype(BF16)
    return hi, lo


_NN = (((1,), (0,)), ((), ()))
_NT = (((1,), (1,)), ((), ()))
_TN = (((0,), (0,)), ((), ()))


def _dg(a, b, dims):
    return lax.dot_general(a, b, dims, preferred_element_type=F32)


def _dot1(a, b, dims=_NN):
    return _dg(a.astype(BF16), b.astype(BF16), dims)


def _dot3(a, b, dims=_NN):
    ah, al = _split(a)
    bh, bl = _split(b)
    return _dg(ah, bh, dims) + (_dg(ah, bl, dims) + _dg(al, bh, dims))


def _dot_exact_rhs(a, b01, dims=_NN):
    a1 = a.astype(BF16)
    r1 = a - a1.astype(F32)
    a2 = r1.astype(BF16)
    a3 = (r1 - a2.astype(F32)).astype(BF16)
    return _dg(a1, b01, dims) + (_dg(a2, b01, dims) + _dg(a3, b01, dims))


def _dot_exact_lhs(a01, b, dims=_NN):
    b1 = b.astype(BF16)
    r1 = b - b1.astype(F32)
    b2 = r1.astype(BF16)
    b3 = (r1 - b2.astype(F32)).astype(BF16)
    return _dg(a01, b1, dims) + (_dg(a01, b2, dims) + _dg(a01, b3, dims))


def _dot2_rhs01(a, b01):
    ah, al = _split(a)
    return _dg(ah, b01, _NN) + _dg(al, b01, _NN)


def _iota(shape, dim):
    return lax.broadcasted_iota(jnp.int32, shape, dim)


def _seg_ones(n, seg):
    r = _iota((n, n), 0) // seg
    c = _iota((n, n), 1) // seg
    return (r == c).astype(BF16)


def _segsum(x, ones_bd):
    return _dot_exact_rhs(x, ones_bd)


def _softplus(x):
    return jnp.maximum(x, 0.0) + jnp.log(1.0 + jnp.exp(-jnp.abs(x)))


def _sigmoid(x):
    return 1.0 / (1.0 + jnp.exp(-x))


def _silu(x):
    return x * _sigmoid(x)


def _rmsnorm_mod(x, g, shift, scale):
    y = x * lax.rsqrt(jnp.mean(x * x, axis=-1, keepdims=True) + NORM_EPS)
    return (y * g) * (1.0 + scale) + shift


def _chunk_index(i, n_ctx_chunks, n_chunks, rev):
    if not rev:
        return i
    return jnp.where(i < n_ctx_chunks, n_ctx_chunks - 1 - i,
                     n_chunks - 1 - (i - n_ctx_chunks))


def _incl_mask(n, rev):
    r = _iota((n, n), 0)
    c = _iota((n, n), 1)
    return (c >= r) if rev else (c <= r)


def _mod_kernel(c_ref, w_ref, b_ref, o_ref):
    o_ref[0] = _dot3(_silu(c_ref[...]), w_ref[0]) + b_ref[0]


def _modulation(cvec, mod_w, mod_b):
    depth, d, n = mod_w.shape
    tn = 1536
    rows = cvec.shape[0]
    return pl.pallas_call(
        _mod_kernel,
        grid=(depth, n // tn),
        in_specs=[
            pl.BlockSpec((rows, d), lambda l, j: (0, 0)),
            pl.BlockSpec((1, d, tn), lambda l, j: (l, 0, j)),
            pl.BlockSpec((1, 1, tn), lambda l, j: (l, 0, j)),
        ],
        out_specs=pl.BlockSpec((1, rows, tn), lambda l, j: (l, 0, j)),
        out_shape=jax.ShapeDtypeStruct((depth, rows, n), F32),
        compiler_params=pltpu.CompilerParams(
            dimension_semantics=("arbitrary", "arbitrary"), vmem_limit_bytes=VMEM_LIMIT),
        name="adaln_mod",
    )(cvec, mod_w, mod_b.reshape(depth, 1, n))


def _inproj_kernel(x_ref, xp_ref, xn_ref, mod_ref, g_ref, w_ref, mu_ref, cw_ref, cb_ref,
                   za_ref, zm_ref, zg_ref, zb_ref, *, tm, n_ctx_tiles, n_tiles):
    j = pl.program_id(1)
    g = g_ref[...]
    shift = mod_ref[0, 0, 0:1, :]
    scale = mod_ref[0, 0, 1:2, :]
    h = _rmsnorm_mod(x_ref[0], g, shift, scale).astype(BF16)
    z = _dg(h, w_ref[...], _NN)
    halo = jnp.concatenate([xp_ref[0], xn_ref[0]], axis=0)
    hh = _rmsnorm_mod(halo, g, shift, scale).astype(BF16)
    zh = _dg(hh, w_ref[:, :SHIFT_COLS], _NN)
    prev_ok = jnp.logical_and(j != 0, j != n_ctx_tiles).astype(F32)
    next_ok = jnp.logical_and(j != n_ctx_tiles - 1, j != n_tiles - 1).astype(F32)
    zs = z[:, :SHIFT_COLS]
    row = _iota((tm, SHIFT_COLS), 0)
    prev = jnp.where(row == 0, zh[SUBLANES - 1:SUBLANES, :] * prev_ok, pltpu.roll(zs, 1, 0))
    nxt = jnp.where(row == tm - 1, zh[SUBLANES:SUBLANES + 1, :] * next_ok,
                    pltpu.roll(zs, tm - 1, 0))
    za = zs[:, :RWKV_COLS]
    shifted = za + mu_ref[...] * (0.5 * (prev[:, :RWKV_COLS] + nxt[:, :RWKV_COLS]) - za)
    za_ref[0] = shifted.astype(za_ref.dtype)
    qk = (prev[:, RWKV_COLS:] * cw_ref[0:1, :] + zs[:, RWKV_COLS:] * cw_ref[1:2, :]
          + nxt[:, RWKV_COLS:] * cw_ref[2:3, :] + cb_ref[...])
    zm_ref[0, :, :MLSTM_QKW] = _silu(qk).astype(zm_ref.dtype)
    zm_ref[0, :, MLSTM_QKW:] = z[:, SHIFT_COLS:RWKV_COLS + ZM_COLS].astype(zm_ref.dtype)
    gate_end = RWKV_COLS + ZM_COLS + LANES
    zg_ref[0] = z[:, RWKV_COLS + ZM_COLS:gate_end]
    zb_ref[0] = z[:, gate_end:].astype(zb_ref.dtype)


def _in_proj(x, modsel, norm_g, w_r, mu, conv_w, conv_b, *, tm, ctx_len):
    b, l, d = x.shape
    n_tiles = l // tm
    n_ctx_tiles = ctx_len // tm
    tm8 = tm // SUBLANES
    kern = functools.partial(_inproj_kernel, tm=tm, n_ctx_tiles=n_ctx_tiles, n_tiles=n_tiles)
    seg = lambda j: jnp.where(j >= n_ctx_tiles, 1, 0)
    return pl.pallas_call(
        kern,
        grid=(b, n_tiles),
        in_specs=[
            pl.BlockSpec((1, tm, d), lambda i, j: (i, j, 0)),
            pl.BlockSpec((1, SUBLANES, d), lambda i, j: (i, jnp.maximum(j * tm8 - 1, 0), 0)),
            pl.BlockSpec((1, SUBLANES, d),
                         lambda i, j: (i, jnp.minimum((j + 1) * tm8, l // SUBLANES - 1), 0)),
            pl.BlockSpec((1, 1, 2, d), lambda i, j: (i, seg(j), 0, 0)),
            pl.BlockSpec((1, d), lambda i, j: (0, 0)),
            pl.BlockSpec((d, IN_COLS_PAD), lambda i, j: (0, 0)),
            pl.BlockSpec((1, RWKV_COLS), lambda i, j: (0, 0)),
            pl.BlockSpec((3, MLSTM_QKW), lambda i, j: (0, 0)),
            pl.BlockSpec((1, MLSTM_QKW), lambda i, j: (0, 0)),
        ],
        out_specs=[
            pl.BlockSpec((1, tm, RWKV_COLS), lambda i, j: (i, j, 0)),
            pl.BlockSpec((1, tm, ZM_COLS), lambda i, j: (i, j, 0)),
            pl.BlockSpec((1, tm, LANES), lambda i, j: (i, j, 0)),
            pl.BlockSpec((1, tm, ZB_COLS), lambda i, j: (i, j, 0)),
        ],
        out_shape=[
            jax.ShapeDtypeStruct((b, l, RWKV_COLS), ACT),
            jax.ShapeDtypeStruct((b, l, ZM_COLS), ACT),
            jax.ShapeDtypeStruct((b, l, LANES), F32),
            jax.ShapeDtypeStruct((b, l, ZB_COLS), ACT),
        ],
        compiler_params=pltpu.CompilerParams(
            dimension_semantics=("arbitrary", "arbitrary"), vmem_limit_bytes=VMEM_LIMIT),
        name="in_proj",
    )(x, x, x, modsel, norm_g.reshape(1, d), w_r, mu.reshape(1, RWKV_COLS), conv_w,
      conv_b.reshape(1, MLSTM_QKW))


def _rwkv_local_kernel(za_ref, kk_ref, ka_ref, rk_ref, w0_ref, w2_ref, a0_ref, a2_ref, g2_ref,
                       g_ref, yl_ref, at_ref, bs_ref, bonus_ref, gate_ref):
    c = CHUNK
    nd = RWKV_DIM
    hd = HEAD_DIM
    nh = RWKV_HEADS
    nb = za_ref.shape[0]
    ones_bd = _seg_ones(nd, hd)
    eye4 = _iota((hd, nd), 0) == (_iota((hd, nd), 1) % hd)
    eye4_f = eye4.astype(F32)
    row2 = _iota((2 * c, 2 * nd), 0)
    rr = row2 % c
    cc = _iota((2 * c, 2 * nd), 1) % c
    diag_ok = jnp.logical_and(cc == rr, row2 >= c)
    bd = lambda x: _block_diag(x, nh, hd, hd)
    units, pr, rhs, vb, pm, rm, qe, ke, gam, gmask = ([] for _ in range(10))
    for rb in range(nb):
        za = za_ref[rb].astype(F32)
        r = za[:, 0:nd]
        k = za[:, nd:2 * nd]
        v = za[:, 2 * nd:3 * nd]
        kkr = k * kk_ref[...]
        kk = kkr / jnp.maximum(jnp.sqrt(_segsum(kkr * kkr, ones_bd)), 1e-12)
        bonus_ref[rb] = (_segsum(r * k * rk_ref[...], ones_bd) * v).astype(bonus_ref.dtype)
        gd = za[:, 3 * nd + 2 * RWKV_W_LORA + 2 * RWKV_A_LORA:]
        gate_ref[rb] = _dot1(_sigmoid(gd), g2_ref[...]).astype(gate_ref.dtype)
        v_b = v.astype(BF16)
        for d in range(2):
            rev = d == 1
            wo = 3 * nd + d * RWKV_W_LORA
            ao = 3 * nd + 2 * RWKV_W_LORA + d * RWKV_A_LORA
            w_lo = _dot1(jnp.tanh(za[:, wo:wo + RWKV_W_LORA]), w2_ref[d])
            log_w = -_softplus(-(w0_ref[d] + w_lo)) - 0.5
            logdec = -jnp.exp(log_w)
            a = _sigmoid(a0_ref[d] + _dot1(za[:, ao:ao + RWKV_A_LORA], a2_ref[d]))
            kd = k * (1.0 + (a - 1.0) * ka_ref[...])
            bvec = kk * a
            cum = _dot_exact_lhs(_incl_mask(c, rev).astype(BF16), logdec)
            total = cum[0:1, :] if rev else cum[c - 1:c, :]
            e_neg = jnp.exp(-cum)
            e_end = jnp.exp(total - cum)
            pm_d = kk * jnp.exp(cum - logdec)
            rm_d = r * jnp.exp(cum)
            units.append((rb, d))
            pm.append(pm_d)
            rm.append(rm_d)
            pr.append(jnp.concatenate([pm_d, rm_d], axis=0).astype(BF16))
            rhs.append(jnp.concatenate([bd((bvec * e_neg).astype(BF16)),
                                        bd((kd * e_neg).astype(BF16))], axis=0))
            vb.append(v_b)
            qe.append((bvec * e_end).astype(BF16))
            ke.append((kd * e_end).astype(BF16))
            gam.append(jnp.exp(total))
            gmask.append(jnp.logical_or(cc > rr if rev else cc < rr, diag_ok))
    n = len(units)
    a4 = [jnp.where(gmask[i], _dg(pr[i], rhs[i], _NT), 0.0) for i in range(n)]
    a4b = [x.astype(BF16) for x in a4]
    pw = [_dg(a4b[i][:c, :nd], bd(a4b[i][:c, :nd]), _NN) for i in range(n)]
    lv = [_dg(a4b[i][:, nd:], bd(vb[i]), _NN) for i in range(n)]
    tinv = [eye4_f - a4[i][:c, :nd] for i in range(n)]
    covered = 2
    while True:
        pwd = [bd(x.astype(BF16)) for x in pw]
        tinv = [tinv[i] + _dg(tinv[i].astype(BF16), pwd[i], _NN) for i in range(n)]
        covered *= 2
        if covered >= c:
            break
        pw = [_dg(pw[i].astype(BF16), pwd[i], _NN) for i in range(n)]
    wz = [_dg(tinv[i].astype(BF16),
              jnp.concatenate([bd(pm[i].astype(BF16)), bd(lv[i][:c].astype(BF16))], axis=1),
              _NN).astype(BF16) for i in range(n)]
    awz = [_dg(a4b[i][c:, :nd],
               jnp.concatenate([bd(wz[i][:, :nd]), bd(wz[i][:, nd:])], axis=1), _NN)
           for i in range(n)]
    eye = _iota((hd, hd), 0) == _iota((hd, hd), 1)
    for i, (rb, d) in enumerate(units):
        g_ref[d, rb] = (rm[i] - awz[i][:, :nd]).astype(g_ref.dtype)
        yl_ref[d, rb] = (lv[i][c:] - awz[i][:, nd:]).astype(yl_ref.dtype)
        ats, bss = [], []
        for h in range(nh):
            sl = slice(h * hd, (h + 1) * hd)
            wz_h = jnp.concatenate([wz[i][:, sl], wz[i][:, nd + h * hd:nd + (h + 1) * hd]], axis=1)
            qwz = _dg(qe[i][:, sl], wz_h, _TN)
            ats.append(jnp.where(eye, jnp.broadcast_to(gam[i][:, sl], (hd, hd)), 0.0)
                       - qwz[:, :hd])
            bss.append(_dg(ke[i][:, sl], vb[i][:, sl], _TN) - qwz[:, hd:])
        at_ref[d, rb] = jnp.concatenate(ats, axis=1).astype(at_ref.dtype)
        bs_ref[d, rb] = jnp.concatenate(bss, axis=1).astype(bs_ref.dtype)


def _rwkv_local(za, k_k, k_a, r_k, w0, w2, a0, a2, g2, *, nb):
    b, l, _ = za.shape
    c = CHUNK
    nd = RWKV_DIM
    full = lambda shape: pl.BlockSpec(shape, lambda i, j: (0,) * len(shape))
    dir_spec = pl.BlockSpec((2, nb, c, nd), lambda i, j: (0, i, j, 0))
    dir_shape = jax.ShapeDtypeStruct((2, b, l, nd), ACT)
    row_spec = pl.BlockSpec((nb, c, nd), lambda i, j: (i, j, 0))
    row_shape = jax.ShapeDtypeStruct((b, l, nd), ACT)
    return pl.pallas_call(
        _rwkv_local_kernel,
        grid=(b // nb, l // c),
        in_specs=[
            pl.BlockSpec((nb, c, RWKV_COLS), lambda i, j: (i, j, 0)),
            full((1, nd)), full((1, nd)), full((1, nd)),
            full((2, 1, nd)), full((2, RWKV_W_LORA, nd)),
            full((2, 1, nd)), full((2, RWKV_A_LORA, nd)),
            full((RWKV_G_LORA, nd)),
        ],
        out_specs=[dir_spec] * 4 + [row_spec] * 2,
        out_shape=[dir_shape] * 4 + [row_shape] * 2,
        compiler_params=pltpu.CompilerParams(
            dimension_semantics=("arbitrary", "arbitrary"), vmem_limit_bytes=VMEM_LIMIT),
        name="rwkv_local",
    )(za, k_k.reshape(1, -1), k_a.reshape(1, -1), r_k.reshape(1, -1), w0.reshape(2, 1, nd),
      w2, a0.reshape(2, 1, nd), a2, g2)


def _rwkv_seq_kernel(gf_ref, ylf_ref, atf_ref, bsf_ref, gr_ref, ylr_ref, atr_ref, bsr_ref,
                     yf_ref, yr_ref, m_ref):
    nb = yf_ref.shape[0]

    @pl.when(pl.program_id(1) == 0)
    def _():
        m_ref[...] = jnp.zeros_like(m_ref)

    dirs = ((gf_ref, ylf_ref, atf_ref, bsf_ref, yf_ref), (gr_ref, ylr_ref, atr_ref, bsr_ref, yr_ref))
    units = [(rb, d) for rb in range(nb) for d in range(2)]
    m_bd = [_block_diag(m_ref[u].astype(BF16), RWKV_HEADS, HEAD_DIM, HEAD_DIM)
            for u in range(len(units))]
    y = [_dg(dirs[d][0][0, rb], m_bd[u], _NN) for u, (rb, d) in enumerate(units)]
    m_new = [_dg(dirs[d][2][0, rb], m_bd[u], _NN) for u, (rb, d) in enumerate(units)]
    for u, (rb, d) in enumerate(units):
        dirs[d][4][rb] = (y[u] + dirs[d][1][0, rb].astype(F32)).astype(dirs[d][4].dtype)
        m_ref[u] = m_new[u] + dirs[d][3][0, rb].astype(F32)


def _rwkv_seq(g, yl, at, bs, *, ctx_len, nb):
    _, b, l, nd = g.shape
    c = CHUNK
    n_chunks = l // c
    n_ctx_chunks = ctx_len // c
    rev_idx = lambda j: _chunk_index(j, n_ctx_chunks, n_chunks, True)
    fwd = pl.BlockSpec((1, nb, c, nd), lambda i, j: (0, i, j, 0))
    bwd = pl.BlockSpec((1, nb, c, nd), lambda i, j: (1, i, rev_idx(j), 0))
    y_shape = jax.ShapeDtypeStruct((b, l, nd), ACT)
    return pl.pallas_call(
        _rwkv_seq_kernel,
        grid=(b // nb, n_chunks),
        in_specs=[fwd] * 4 + [bwd] * 4,
        out_specs=[pl.BlockSpec((nb, c, nd), lambda i, j: (i, j, 0)),
                   pl.BlockSpec((nb, c, nd), lambda i, j: (i, rev_idx(j), 0))],
        out_shape=[y_shape, y_shape],
        scratch_shapes=[pltpu.VMEM((nb * 2, HEAD_DIM, RWKV_DIM), F32)],
        compiler_params=pltpu.CompilerParams(
            dimension_semantics=("arbitrary", "arbitrary"), vmem_limit_bytes=VMEM_LIMIT),
        name="rwkv_seq",
    )(g, yl, at, bs, g, yl, at, bs)


def _cummax_rows(x, rev):
    n = x.shape[0]
    row = _iota(x.shape, 0)
    shift = 1
    while shift < n:
        if rev:
            moved = jnp.where(row >= n - shift, -jnp.inf, pltpu.roll(x, n - shift, 0))
        else:
            moved = jnp.where(row < shift, -jnp.inf, pltpu.roll(x, shift, 0))
        x = jnp.maximum(x, moved)
        shift *= 2
    return x


def _head_lane(d):
    return d * 2 * MLSTM_HEADS + MLSTM_HEADS


def _sel_expand(d, width):
    n = MLSTM_HEADS * width
    return (_iota((LANES, n), 0) == _head_lane(d) + _iota((LANES, n), 1) // width).astype(BF16)


def _sel_reduce(d, width):
    n = MLSTM_HEADS * width
    return (_iota((n, LANES), 1) == _head_lane(d) + _iota((n, LANES), 0) // width).astype(BF16)


def _block_diag(x, reps, row_block, col_block):
    t = jnp.concatenate([x] * reps, axis=0)
    keep = (_iota(t.shape, 0) // row_block) == (_iota(t.shape, 1) // col_block)
    return jnp.where(keep, t, jnp.zeros_like(t))


def _mlstm_local_kernel(zm_ref, zg_ref, gb_ref, num_ref, den_ref, mi_ref, b_ref, cl_ref,
                        rows_ref):
    c = CHUNK
    nh = MLSTM_HEADS
    dk = MLSTM_QK
    dv = MLSTM_V
    nb = zm_ref.shape[0]
    lane_j = _iota((c, nh * c), 1) % c
    row_s = _iota((c, nh * c), 0)
    cl_keep = (_iota((nh * dk, MLSTM_DIM), 0) // dk) == (_iota((nh * dk, MLSTM_DIM), 1) // dv)
    sel_c = [_sel_expand(d, c) for d in range(2)]
    sel_k = [_sel_expand(d, dk) for d in range(2)]
    sel_r = [_sel_reduce(d, c) for d in range(2)]
    k, vb, log_f, li, qk, vbd = [], [], [], [], [], []
    for rb in range(nb):
        zm = zm_ref[rb]
        qb = (zm[:, :nh * dk].astype(F32) * (dk ** -0.5)).astype(BF16)
        k.append(zm[:, nh * dk:MLSTM_QKW].astype(F32))
        vb.append(zm[:, MLSTM_QKW:MLSTM_QKW + MLSTM_DIM].astype(BF16))
        gates = zg_ref[rb] + gb_ref[...]
        capped = GATE_SOFTCAP * jnp.tanh(gates * (1.0 / GATE_SOFTCAP))
        log_f.append(-_softplus(-capped))
        li.append(pltpu.roll(capped, nh, 1))
        qk.append(_dg(qb, _block_diag(k[rb].astype(BF16), nh, c, dk), _NT))
        vbd.append(_block_diag(vb[rb], nh, c, dv))
    units = [(rb, d) for rb in range(nb) for d in range(2)]
    b_all = [_dot_exact_lhs(_incl_mask(c, d == 1).astype(BF16), log_f[rb]) for rb, d in units]
    x = [li[rb] - b_all[u] for u, (rb, d) in enumerate(units)]
    cm = [_cummax_rows(x[u], d == 1) for u, (rb, d) in enumerate(units)]
    e1 = [_dot_exact_rhs(-cm[u], sel_c[d]) for u, (rb, d) in enumerate(units)]
    xe = [_dot_exact_rhs(x[u], sel_c[d]) for u, (rb, d) in enumerate(units)]
    g_row, m_loc, e32 = [], [], []
    for u, (rb, d) in enumerate(units):
        last = 0 if d == 1 else c - 1
        g_row.append(b_all[u][last:last + 1, :])
        w_end = g_row[u] - b_all[u] + li[rb]
        m_loc.append(jnp.max(w_end, axis=0, keepdims=True))
        e32.append(_dot2_rhs01(jnp.exp(w_end - m_loc[u]), sel_k[d]))
    wi = []
    for u, (rb, d) in enumerate(units):
        x_row = jnp.sum(jnp.where(row_s == lane_j, xe[u], 0.0), axis=0, keepdims=True)
        earlier = (lane_j >= row_s) if d == 1 else (lane_j <= row_s)
        wi.append((qk[rb] * jnp.exp(jnp.where(earlier, e1[u] + x_row, -jnp.inf))).astype(BF16))
    nd = [_dg(wi[u], jnp.concatenate([vbd[rb], sel_r[d]], axis=1), _NN)
          for u, (rb, d) in enumerate(units)]
    ke = [k[rb] * e32[u] for u, (rb, d) in enumerate(units)]
    full = [_dg(ke[u].astype(BF16), vb[rb], _TN) for u, (rb, d) in enumerate(units)]
    for u, (rb, d) in enumerate(units):
        num_ref[d, rb] = nd[u][:, :MLSTM_DIM].astype(num_ref.dtype)
        den_ref[d, rb] = nd[u][:, MLSTM_DIM:]
        mi_ref[d, rb] = b_all[u] + cm[u]
        b_ref[d, rb] = b_all[u]
        f = jnp.where(cl_keep, full[u], 0.0)
        cl_ref[d, rb] = ((f[0:dk] + f[dk:2 * dk])
                         + (f[2 * dk:3 * dk] + f[3 * dk:4 * dk])).astype(cl_ref.dtype)
        rows_ref[d, rb] = jnp.concatenate(
            [jnp.sum(ke[u], axis=0, keepdims=True), m_loc[u], g_row[u],
             jnp.zeros((SUBLANES - 3, LANES), F32)], axis=0)


def _mlstm_local(zm, zg, gate_bias, *, nb):
    b, l, _ = zm.shape
    c = CHUNK
    n_chunks = l // c
    spec = lambda rows, w, dt: pl.BlockSpec((2, nb, rows, w), lambda i, j: (0, i, j, 0))
    shape = lambda rows, w, dt: jax.ShapeDtypeStruct((2, b, n_chunks * rows, w), dt)
    outs = [(c, MLSTM_DIM, ACT), (c, LANES, F32), (c, LANES, F32), (c, LANES, F32),
            (MLSTM_QK, MLSTM_DIM, ACT), (SUBLANES, LANES, F32)]
    return pl.pallas_call(
        _mlstm_local_kernel,
        grid=(b // nb, n_chunks),
        in_specs=[
            pl.BlockSpec((nb, c, ZM_COLS), lambda i, j: (i, j, 0)),
            pl.BlockSpec((nb, c, LANES), lambda i, j: (i, j, 0)),
            pl.BlockSpec((1, LANES), lambda i, j: (0, 0)),
        ],
        out_specs=[spec(*o) for o in outs],
        out_shape=[shape(*o) for o in outs],
        compiler_params=pltpu.CompilerParams(
            dimension_semantics=("arbitrary", "arbitrary"), vmem_limit_bytes=VMEM_LIMIT),
        name="mlstm_local",
    )(zm, zg, gate_bias)


def _mlstm_seq_kernel(qf_ref, numf_ref, denf_ref, mif_ref, bf_ref, clf_ref, rowsf_ref,
                      qr_ref, numr_ref, denr_ref, mir_ref, br_ref, clr_ref, rowsr_ref,
                      hf_ref, hr_ref, cbd_ref, n_ref, m_ref):
    c = CHUNK
    nh = MLSTM_HEADS
    dk = MLSTM_QK
    dv = MLSTM_V
    nb = hf_ref.shape[0]

    @pl.when(pl.program_id(1) == 0)
    def _():
        cbd_ref[...] = jnp.zeros_like(cbd_ref)
        n_ref[...] = jnp.zeros_like(n_ref)
        m_ref[...] = jnp.zeros_like(m_ref)

    dirs = ((qf_ref, numf_ref, denf_ref, mif_ref, bf_ref, clf_ref, rowsf_ref, hf_ref),
            (qr_ref, numr_ref, denr_ref, mir_ref, br_ref, clr_ref, rowsr_ref, hr_ref))
    units = [(rb, d) for rb in range(nb) for d in range(2)]
    nu = len(units)
    sel_v = [_sel_expand(d, dv) for d in range(2)]
    sel_k = [_sel_expand(d, dk) for d in range(2)]
    sel_r = [_sel_reduce(d, dk) for d in range(2)]
    q = [dirs[d][0][rb].astype(F32) * (dk ** -0.5) for rb, d in units]
    rows = [dirs[d][6][0, rb] for rb, d in units]
    m_row = [m_ref[u] for u in range(nu)]
    n_row = [n_ref[u] for u in range(nu)]
    cbd = [cbd_ref[u] for u in range(nu)]
    qn = [_dot2_rhs01(q[u] * n_row[u], sel_r[d]) for u, (rb, d) in enumerate(units)]
    qc = [_dg(q[u].astype(BF16), cbd[u].astype(BF16), _NN) for u in range(nu)]
    a12, srow = [], []
    for u, (rb, d) in enumerate(units):
        mi = dirs[d][3][0, rb]
        log_inter = dirs[d][4][0, rb] + m_row[u]
        m_out = jnp.maximum(log_inter, mi)
        s_intra = jnp.exp(mi - m_out)
        s_inter = jnp.exp(log_inter - m_out)
        den = s_intra * dirs[d][2][0, rb] + s_inter * qn[u]
        dinv = 1.0 / jnp.maximum(jnp.abs(den), jnp.exp(-m_out))
        a12.append(jnp.concatenate([s_intra * dinv, s_inter * dinv], axis=0))
        m_loc, g_row = rows[u][1:2], rows[u][2:3]
        m_new = jnp.maximum(g_row + m_row[u], m_loc)
        srow.append(jnp.concatenate(
            [jnp.exp(g_row + m_row[u] - m_new), jnp.exp(m_loc - m_new),
             jnp.zeros((SUBLANES - 2, LANES), F32)], axis=0))
        m_ref[u] = m_new
    a12e = [_dot2_rhs01(a12[u], sel_v[d]) for u, (rb, d) in enumerate(units)]
    s_v = [_dot2_rhs01(srow[u], sel_v[d]) for u, (rb, d) in enumerate(units)]
    s_k = [_dot2_rhs01(srow[u], sel_k[d]) for u, (rb, d) in enumerate(units)]
    for u, (rb, d) in enumerate(units):
        h_out = a12e[u][:c] * dirs[d][1][0, rb].astype(F32) + a12e[u][c:] * qc[u]
        dirs[d][7][rb] = h_out.astype(dirs[d][7].dtype)
        cl_full = _block_diag(dirs[d][5][0, rb].astype(F32), nh, dk, dv)
        cbd_ref[u] = s_v[u][0:1] * cbd[u] + s_v[u][1:2] * cl_full
        n_ref[u] = s_k[u][0:1] * n_row[u] + s_k[u][1:2] * rows[u][0:1]


def _mlstm_seq(zm, num, den, mi, bsum, cl, rows, *, ctx_len, nb):
    b, l, _ = zm.shape
    c = CHUNK
    n_chunks = l // c
    n_ctx_chunks = ctx_len // c
    rev_idx = lambda j: _chunk_index(j, n_ctx_chunks, n_chunks, True)

    def specs(d):
        cidx = (lambda j: j) if d == 0 else rev_idx
        blk = lambda r, w: pl.BlockSpec((1, nb, r, w), lambda i, j: (d, i, cidx(j), 0))
        return [pl.BlockSpec((nb, c, MLSTM_HEADS * MLSTM_QK), lambda i, j: (i, cidx(j), 0)),
                blk(c, MLSTM_DIM), blk(c, LANES), blk(c, LANES), blk(c, LANES),
                blk(MLSTM_QK, MLSTM_DIM), blk(SUBLANES, LANES)]

    h_shape = jax.ShapeDtypeStruct((b, l, MLSTM_DIM), ACT)
    args = (zm, num, den, mi, bsum, cl, rows)
    return pl.pallas_call(
        _mlstm_seq_kernel,
        grid=(b // nb, n_chunks),
        in_specs=specs(0) + specs(1),
        out_specs=[pl.BlockSpec((nb, c, MLSTM_DIM), lambda i, j: (i, j, 0)),
                   pl.BlockSpec((nb, c, MLSTM_DIM), lambda i, j: (i, rev_idx(j), 0))],
        out_shape=[h_shape, h_shape],
        scratch_shapes=[
            pltpu.VMEM((2 * nb, MLSTM_HEADS * MLSTM_QK, MLSTM_DIM), F32),
            pltpu.VMEM((2 * nb, 1, LANES), F32),
            pltpu.VMEM((2 * nb, 1, LANES), F32),
        ],
        compiler_params=pltpu.CompilerParams(
            dimension_semantics=("arbitrary", "arbitrary"), vmem_limit_bytes=VMEM_LIMIT),
        name="mlstm_seq",
    )(*args, *args)


def _rope_swap(x):
    lane = _iota(x.shape, 1)
    half = MLA_ROPE // 2
    return jnp.where(lane < MLA_NOPE + half, pltpu.roll(x, LANES - half, 1),
                     pltpu.roll(x, half, 1))


def _mla_proj_kernel(zb_ref, cq_ref, sq_ref, ck_ref, sk_ref, qg_ref, kvg_ref,
                     wq_ref, wk_ref, wv_ref, q_ref, k_ref, v_ref):
    hq = MLA_HEADS * QPAD
    zb = zb_ref[0].astype(F32)
    cq = zb[:, :MLA_Q_RANK]
    ckv = zb[:, MLA_Q_RANK:MLA_Q_RANK + MLA_KV_RANK]
    kr = zb[:, MLA_Q_RANK + MLA_KV_RANK:]
    cqn = cq * lax.rsqrt(jnp.mean(cq * cq, axis=-1, keepdims=True) + NORM_EPS) * qg_ref[...]
    ckn = ckv * lax.rsqrt(jnp.mean(ckv * ckv, axis=-1, keepdims=True) + NORM_EPS) * kvg_ref[...]
    q_all = _dot1(cqn, wq_ref[...])
    k_all = _dot1(ckn, wk_ref[...])
    v_ref[0] = _dot1(ckn, wv_ref[...]).astype(BF16)
    cos_q, sin_q, cos_k = cq_ref[...], sq_ref[...], ck_ref[...]
    kr_rot = _rope_swap(kr) * sk_ref[...]
    inv_dim = 1.0 / MLA_QK
    for h in range(MLA_HEADS):
        sl = slice(h * QPAD, (h + 1) * QPAD)
        qh = q_all[:, sl]
        q_inv = lax.rsqrt(jnp.sum(qh * qh, axis=-1, keepdims=True) * inv_dim + NORM_EPS)
        q_ref[0, :, sl] = ((qh * cos_q + q_all[:, hq + h * QPAD:hq + (h + 1) * QPAD] * sin_q)
                           * q_inv).astype(BF16)
        kh = k_all[:, sl] + kr
        k_inv = lax.rsqrt(jnp.sum(kh * kh, axis=-1, keepdims=True) * inv_dim + NORM_EPS)
        k_ref[0, :, sl] = ((kh * cos_k + kr_rot) * k_inv).astype(BF16)


def _mla_proj(zb, tables, q_norm_g, kv_norm_g, wq_r, wk_r, wv_r, *, tm):
    b, l, _ = zb.shape
    hq = MLA_HEADS * QPAD
    full = lambda shape: pl.BlockSpec(shape, lambda i, j: (0,) * len(shape))
    table = pl.BlockSpec((tm, QPAD), lambda i, j: (j, 0))
    return pl.pallas_call(
        _mla_proj_kernel,
        grid=(b, l // tm),
        in_specs=[
            pl.BlockSpec((1, tm, ZB_COLS), lambda i, j: (i, j, 0)),
            table, table, table, table,
            full((1, MLA_Q_RANK)), full((1, MLA_KV_RANK)),
            full((MLA_Q_RANK, 2 * hq)), full((MLA_KV_RANK, hq)), full((MLA_KV_RANK, MLA_DIM)),
        ],
        out_specs=[
            pl.BlockSpec((1, tm, hq), lambda i, j: (i, j, 0)),
            pl.BlockSpec((1, tm, hq), lambda i, j: (i, j, 0)),
            pl.BlockSpec((1, tm, MLA_DIM), lambda i, j: (i, j, 0)),
        ],
        out_shape=[
            jax.ShapeDtypeStruct((b, l, hq), BF16),
            jax.ShapeDtypeStruct((b, l, hq), BF16),
            jax.ShapeDtypeStruct((b, l, MLA_DIM), BF16),
        ],
        compiler_params=pltpu.CompilerParams(
            dimension_semantics=("arbitrary", "arbitrary"), vmem_limit_bytes=VMEM_LIMIT),
        name="mla_proj",
    )(zb, *tables, q_norm_g.reshape(1, -1), kv_norm_g.reshape(1, -1), wq_r, wk_r, wv_r)


def _attend_tile(q_ref, k_ref, v_ref, o_ref, n_blocks):
    tq = q_ref.shape[1]
    bk = KV_BLOCK
    lane_v = _iota((bk, 2 * MLA_V), 1)
    lane_o = _iota((tq, 2 * MLA_V), 1)
    sum_lane = [(1 - h) * MLA_V for h in range(2)]
    q = [q_ref[0, :, h * QPAD:(h + 1) * QPAD] for h in range(2)]

    def scores(j):
        return [_dg(q[h], k_ref[0, j * bk:(j + 1) * bk, h * QPAD:(h + 1) * QPAD], _NT)
                for h in range(2)]

    m = [jnp.full((tq, 1), -jnp.inf, F32) for _ in range(2)]
    acc = [jnp.zeros((tq, 2 * MLA_V), F32) for _ in range(2)]
    s = scores(0)
    for j in range(n_blocks):
        s_next = scores(j + 1) if j + 1 < n_blocks else None
        v = v_ref[0, j * bk:(j + 1) * bk, :]
        for h in range(2):
            ones = (lane_v == sum_lane[h]).astype(BF16)
            vh = jnp.where((lane_v // MLA_V) == h, v, ones)
            m_new = jnp.maximum(m[h], jnp.max(s[h], axis=-1, keepdims=True))
            p = jnp.exp2(s[h] - m_new).astype(BF16)
            acc[h] = jnp.exp2(m[h] - m_new) * acc[h] + _dg(p, vh, _NN)
            m[h] = m_new
        s = s_next
    out = jnp.zeros((tq, 2 * MLA_V), F32)
    for h in range(2):
        denom = jnp.sum(jnp.where(lane_o == sum_lane[h], acc[h], 0.0), axis=-1, keepdims=True)
        out = out + jnp.where((lane_o // MLA_V) == h, acc[h], 0.0) / denom
    o_ref[0] = out.astype(o_ref.dtype)


def _mla_attn_kernel(q_ref, k_ref, v_ref, o_ref, *, ctx_len, first_tile, n_ctx_tiles):
    n_all = k_ref.shape[1] // KV_BLOCK
    if first_tile >= n_ctx_tiles:
        _attend_tile(q_ref, k_ref, v_ref, o_ref, n_all)
        return
    is_ctx = (pl.program_id(2) + first_tile) < n_ctx_tiles

    @pl.when(is_ctx)
    def _():
        _attend_tile(q_ref, k_ref, v_ref, o_ref, ctx_len // KV_BLOCK)

    @pl.when(jnp.logical_not(is_ctx))
    def _():
        _attend_tile(q_ref, k_ref, v_ref, o_ref, n_all)


def _mla_attn(q, k, v, *, tq, ctx_len, first_tile):
    b, l, _ = q.shape
    assert l % KV_BLOCK == 0 and ctx_len % KV_BLOCK == 0
    n_q = l // tq - first_tile
    pairs = MLA_HEADS // 2
    kern = functools.partial(_mla_attn_kernel, ctx_len=ctx_len, first_tile=first_tile,
                             n_ctx_tiles=ctx_len // tq)
    return pl.pallas_call(
        kern,
        grid=(b, pairs, n_q),
        in_specs=[
            pl.BlockSpec((1, tq, 2 * QPAD), lambda i, p, j: (i, j + first_tile, p)),
            pl.BlockSpec((1, l, 2 * QPAD), lambda i, p, j: (i, 0, p)),
            pl.BlockSpec((1, l, 2 * MLA_V), lambda i, p, j: (i, 0, p)),
        ],
        out_specs=pl.BlockSpec((1, tq, 2 * MLA_V), lambda i, p, j: (i, j + first_tile, p)),
        out_shape=jax.ShapeDtypeStruct((b, l, MLA_DIM), ACT),
        compiler_params=pltpu.CompilerParams(
            dimension_semantics=("arbitrary", "arbitrary", "arbitrary"),
            vmem_limit_bytes=VMEM_LIMIT),
        name="mla_attn",
    )(q, k, v)


def _outproj_kernel(x_ref, yf_ref, yr_ref, bonus_ref, gate_ref, attn_ref, hf_ref, hr_ref,
                    o_ref, mod_ref, lng_ref, lnb_ref, mng_ref, wa_ref, wb_ref, wm_ref, out_ref):
    ones_bd = _seg_ones(RWKV_DIM, HEAD_DIM)
    inv = 1.0 / HEAD_DIM
    y = yf_ref[0].astype(F32) + yr_ref[0].astype(F32)
    mean = _segsum(y, ones_bd) * inv
    yc = y - mean
    var = _segsum(yc * yc, ones_bd) * inv
    yn = yc * lax.rsqrt(var + RWKV_GN_EPS) * lng_ref[...] + lnb_ref[...]
    ya = (yn + bonus_ref[0].astype(F32)) * gate_ref[0].astype(F32)
    hm = hf_ref[0].astype(F32) + hr_ref[0].astype(F32)
    hn = hm * lax.rsqrt(_segsum(hm * hm, ones_bd) * inv + NORM_EPS)
    ym = hn * mng_ref[...] * _sigmoid(o_ref[0].astype(F32))
    mix = (_dot1(ya, wa_ref[...]) + _dot1(attn_ref[0], wb_ref[...]) + _dot1(ym, wm_ref[...]))
    out_ref[0] = x_ref[0] + mod_ref[0, 0] * mix


def _out_proj(x, yf, yr, bonus, gate, attn, hf, hr, zm, modsel, ln_g, ln_b, mnorm_g,
              wa, wb, wm, *, tm, ctx_len, first_tile):
    b, l, d = x.shape
    n_ctx_tiles = ctx_len // tm
    n_tiles = l // tm - first_tile
    row = lambda w: pl.BlockSpec((1, tm, w), lambda i, j: (i, j + first_tile, 0))
    full = lambda shape: pl.BlockSpec(shape, lambda i, j: (0,) * len(shape))
    seg = lambda j: jnp.where(j + first_tile >= n_ctx_tiles, 1, 0)
    o_block = (MLSTM_QKW + MLSTM_DIM) // MLSTM_DIM
    return pl.pallas_call(
        _outproj_kernel,
        grid=(b, n_tiles),
        in_specs=[
            row(d), row(RWKV_DIM), row(RWKV_DIM), row(RWKV_DIM), row(RWKV_DIM), row(MLA_DIM),
            row(MLSTM_DIM), row(MLSTM_DIM),
            pl.BlockSpec((1, tm, MLSTM_DIM), lambda i, j: (i, j + first_tile, o_block)),
            pl.BlockSpec((1, 1, 1, d), lambda i, j: (i, seg(j), 0, 0)),
            full((1, RWKV_DIM)), full((1, RWKV_DIM)), full((1, MLSTM_DIM)),
            full((RWKV_DIM, d)), full((MLA_DIM, d)), full((MLSTM_DIM, d)),
        ],
        out_specs=pl.BlockSpec((1, tm, d), lambda i, j: (i, j, 0)),
        out_shape=jax.ShapeDtypeStruct((b, n_tiles * tm, d), F32),
        compiler_params=pltpu.CompilerParams(
            dimension_semantics=("arbitrary", "arbitrary"), vmem_limit_bytes=VMEM_LIMIT),
        name="out_proj",
    )(x, yf, yr, bonus, gate, attn, hf, hr, zm, modsel, ln_g.reshape(1, -1),
      ln_b.reshape(1, -1), mnorm_g.reshape(1, -1), wa, wb, wm)


def _ffn_kernel(x_ref, mod_ref, g_ref, w1_ref, w2_ref, out_ref, *, hidden):
    x = x_ref[0]
    h = _rmsnorm_mod(x, g_ref[...], mod_ref[0, 0, 0:1, :], mod_ref[0, 0, 1:2, :]).astype(BF16)
    gu = _dg(h, w1_ref[...], _NN)
    act = (_silu(gu[:, :hidden]) * gu[:, hidden:]).astype(BF16)
    out_ref[0] = x + mod_ref[0, 0, 2:3, :] * _dg(act, w2_ref[...], _NN)


def _ffn(x, modsel, norm_g, w1, w2, *, tm, n_ctx_tiles_here):
    b, l, d = x.shape
    hidden = w2.shape[0]
    seg = lambda j: jnp.where(j >= n_ctx_tiles_here, 1, 0)
    return pl.pallas_call(
        functools.partial(_ffn_kernel, hidden=hidden),
        grid=(b, l // tm),
        in_specs=[
            pl.BlockSpec((1, tm, d), lambda i, j: (i, j, 0)),
            pl.BlockSpec((1, 1, 3, d), lambda i, j: (i, seg(j), 0, 0)),
            pl.BlockSpec((1, d), lambda i, j: (0, 0)),
            pl.BlockSpec((d, 2 * hidden), lambda i, j: (0, 0)),
            pl.BlockSpec((hidden, d), lambda i, j: (0, 0)),
        ],
        out_specs=pl.BlockSpec((1, tm, d), lambda i, j: (i, j, 0)),
        out_shape=jax.ShapeDtypeStruct((b, l, d), F32),
        compiler_params=pltpu.CompilerParams(
            dimension_semantics=("arbitrary", "arbitrary"), vmem_limit_bytes=VMEM_LIMIT),
        name="ffn",
    )(x, modsel, norm_g.reshape(1, d), w1, w2)


def _arrange_w_in(w_in):
    d = w_in.shape[0]
    wa = w_in[:, :RWKV_COLS]
    wb = w_in[:, RWKV_COLS:RWKV_COLS + MLA_COLS]
    wm = w_in[:, RWKV_COLS + MLA_COLS:]
    gates = wm[:, MLSTM_QKW + 2 * MLSTM_DIM:]
    gate_blk = jnp.pad(gates, ((0, 0), (0, LANES - gates.shape[1])))
    kr = wb[:, MLA_Q_RANK + MLA_KV_RANK:]
    kr_blk = jnp.pad(kr, ((0, 0), (MLA_NOPE, LANES - MLA_NOPE - MLA_ROPE)))
    out = jnp.concatenate([wa, wm[:, :MLSTM_QKW + 2 * MLSTM_DIM], gate_blk,
                           wb[:, :MLA_Q_RANK + MLA_KV_RANK], kr_blk], axis=1)
    assert out.shape == (d, IN_COLS_PAD)
    return out.astype(BF16)


def _arrange_mla_weights(w_uq, w_ukv):
    rq = w_uq.shape[0]
    wq = w_uq.reshape(rq, MLA_HEADS, MLA_QK)
    wq = jnp.pad(wq, ((0, 0), (0, 0), (0, QPAD - MLA_QK)))
    half = MLA_ROPE // 2
    wq_swap = jnp.concatenate(
        [jnp.zeros_like(wq[:, :, :MLA_NOPE]), wq[:, :, MLA_NOPE + half:MLA_QK],
         wq[:, :, MLA_NOPE:MLA_NOPE + half], jnp.zeros_like(wq[:, :, MLA_QK:])], axis=2)
    wq = jnp.concatenate([wq.reshape(rq, -1), wq_swap.reshape(rq, -1)], axis=1)
    rk = w_ukv.shape[0]
    wkv = w_ukv.reshape(rk, MLA_HEADS, MLA_NOPE + MLA_V)
    wk = jnp.pad(wkv[:, :, :MLA_NOPE], ((0, 0), (0, 0), (0, QPAD - MLA_NOPE)))
    wk = wk.reshape(rk, MLA_HEADS * QPAD)
    wv = wkv[:, :, MLA_NOPE:].reshape(rk, MLA_DIM)
    return wq.astype(BF16), wk.astype(BF16), wv.astype(BF16)


def _rope_tables(seq_len, ctx_len):
    rows = seq_len // GRID_W
    row = jnp.repeat(jnp.arange(rows, dtype=F32), GRID_W)
    col = jnp.tile(jnp.arange(GRID_W, dtype=F32), rows)
    n_freq = MLA_ROPE // 4
    inv = jnp.power(ROPE_BASE, -jnp.arange(n_freq, dtype=F32) / n_freq)
    ang = jnp.concatenate([row[:, None] * inv, col[:, None] * inv], axis=-1)
    cos, sin = jnp.cos(ang), jnp.sin(ang)
    pad_l, pad_r = MLA_NOPE, QPAD - MLA_QK
    cos_f = jnp.concatenate([jnp.ones((seq_len, pad_l), F32), cos, cos,
                             jnp.ones((seq_len, pad_r), F32)], axis=1)
    sin_f = jnp.concatenate([jnp.zeros((seq_len, pad_l), F32), sin, sin,
                             jnp.zeros((seq_len, pad_r), F32)], axis=1)
    cos_f = jnp.concatenate([jnp.ones((ctx_len, QPAD), F32), cos_f], axis=0)
    sin_f = jnp.concatenate([jnp.zeros((ctx_len, QPAD), F32), sin_f], axis=0)
    return cos_f, sin_f


def _gained_tables(cos_f, sin_f, gain, scale):
    g = jnp.pad(gain, (0, QPAD - gain.shape[0]))
    half = MLA_ROPE // 2
    lo, mid, hi = MLA_NOPE, MLA_NOPE + half, MLA_QK
    g_swap = jnp.concatenate([g[:lo], g[mid:hi], g[lo:mid], g[hi:]])
    sign = jnp.where(jnp.arange(QPAD) < mid, -1.0, 1.0).astype(F32)
    return cos_f * (g * scale), sin_f * (g_swap * sign * scale)


def _gate_bias(i_b, f_b):
    gb = jnp.stack([i_b, f_b], axis=1).reshape(-1)
    return jnp.pad(gb, (0, LANES - gb.shape[0])).reshape(1, LANES)


def kernel(x, c, ctx, c_ctx, mod_w, mod_b, norm1_g, norm2_g, w_in, w_out, ffn_w_in, ffn_w_out, rwkv_mu, rwkv_w0, rwkv_w2, rwkv_a0, rwkv_a2, rwkv_g2, rwkv_k_k, rwkv_k_a, rwkv_r_k, rwkv_ln_g, rwkv_ln_b, mla_q_norm_g, mla_w_uq, mla_kv_norm_g, mla_w_ukv, mla_q_qknorm_g, mla_k_qknorm_g, mlstm_conv_w, mlstm_conv_b, mlstm_i_b, mlstm_f_b, mlstm_norm_g):
    bsz, seq, d = x.shape
    ctx_len = ctx.shape[1]
    depth = mod_w.shape[0]
    tm = min(256, ctx_len)
    assert ctx_len % tm == 0 and seq % tm == 0 and ctx_len % CHUNK == 0 and seq % CHUNK == 0
    nb = ROWS_PER_STEP if bsz % ROWS_PER_STEP == 0 else 1
    nb_seq = SEQ_ROWS_PER_STEP if bsz % SEQ_ROWS_PER_STEP == 0 else nb

    xu = jnp.concatenate([ctx, x], axis=1)
    n_ctx_tiles = ctx_len // tm

    rows = -(-(bsz + 1) // SUBLANES) * SUBLANES
    cvec = jnp.zeros((rows, d), F32).at[:bsz].set(c).at[bsz].set(c_ctx)
    mod = _modulation(cvec, mod_w, mod_b).reshape(depth, rows, 6, d)
    mod_lat = mod[:, :bsz]
    mod_ctx = jnp.broadcast_to(mod[:, bsz][:, None], mod_lat.shape)
    modsel = jnp.stack([mod_ctx, mod_lat], axis=2)

    cos_f, sin_f = _rope_tables(seq, ctx_len)

    for i in range(depth):
        last = i == depth - 1
        first_tile = n_ctx_tiles if last else 0
        w_r = _arrange_w_in(w_in[i])
        za, zm, zg, zb = _in_proj(xu, modsel[i, :, :, 0:2], norm1_g[i], w_r, rwkv_mu[i],
                              mlstm_conv_w[i], mlstm_conv_b[i], tm=tm, ctx_len=ctx_len)
        g_loc, y_loc, a_tr, b_st, bonus, gate = _rwkv_local(
            za, rwkv_k_k[i], rwkv_k_a[i], rwkv_r_k[i], rwkv_w0[i], rwkv_w2[i], rwkv_a0[i],
            rwkv_a2[i], rwkv_g2[i], nb=nb)
        yf, yr = _rwkv_seq(g_loc, y_loc, a_tr, b_st, ctx_len=ctx_len, nb=nb_seq)
        wq_r, wk_r, wv_r = _arrange_mla_weights(mla_w_uq[i], mla_w_ukv[i])
        tables = (_gained_tables(cos_f, sin_f, mla_q_qknorm_g[i], MLA_QK ** -0.5 * LOG2_E)
                  + _gained_tables(cos_f, sin_f, mla_k_qknorm_g[i], 1.0))
        q, k, v = _mla_proj(zb, tables, mla_q_norm_g[i], mla_kv_norm_g[i],
                            wq_r, wk_r, wv_r, tm=tm)
        attn = _mla_attn(q, k, v, tq=tm, ctx_len=ctx_len, first_tile=first_tile)
        gb = _gate_bias(mlstm_i_b[i], mlstm_f_b[i])
        hf, hr = _mlstm_seq(zm, *_mlstm_local(zm, zg, gb, nb=nb), ctx_len=ctx_len, nb=nb_seq)
        wo = w_out[i].astype(BF16)
        xu = _out_proj(xu, yf, yr, bonus, gate, attn, hf, hr, zm, modsel[i, :, :, 2:3],
                       rwkv_ln_g[i], rwkv_ln_b[i], mlstm_norm_g[i],
                       wo[:RWKV_DIM], wo[RWKV_DIM:RWKV_DIM + MLA_DIM], wo[RWKV_DIM + MLA_DIM:],
                       tm=tm, ctx_len=ctx_len, first_tile=first_tile)
        xu = _ffn(xu, modsel[i, :, :, 3:6], norm2_g[i], ffn_w_in[i].astype(BF16),
                  ffn_w_out[i].astype(BF16), tm=tm,
                  n_ctx_tiles_here=0 if last else n_ctx_tiles)
    return xu
```

```python
import functools

import jax
import jax.numpy as jnp
from jax import lax
from jax.experimental import pallas as pl
from jax.experimental.pallas import tpu as pltpu

F32 = jnp.float32
BF16 = jnp.bfloat16

HEAD_DIM = 64
NORM_EPS = 1e-6
GRID_W = 64
ROPE_BASE = 10000.0
LOG2_E = 1.4426950408889634
RWKV_HEADS = 4
RWKV_DIM = RWKV_HEADS * HEAD_DIM
RWKV_W_LORA = 64
RWKV_A_LORA = 64
RWKV_G_LORA = 128
RWKV_COLS = 3 * RWKV_DIM + 2 * RWKV_W_LORA + 2 * RWKV_A_LORA + RWKV_G_LORA
RWKV_GN_EPS = 64e-5
MLA_HEADS = 8
MLA_NOPE = 64
MLA_ROPE = 32
MLA_V = HEAD_DIM
MLA_QK = MLA_NOPE + MLA_ROPE
MLA_Q_RANK = 512
MLA_KV_RANK = 256
MLA_DIM = MLA_HEADS * MLA_V
MLA_COLS = MLA_Q_RANK + MLA_KV_RANK + MLA_ROPE
MLSTM_HEADS = 4
MLSTM_QK = 32
MLSTM_V = HEAD_DIM
MLSTM_DIM = MLSTM_HEADS * MLSTM_V
MLSTM_QKW = 2 * MLSTM_HEADS * MLSTM_QK
GATE_SOFTCAP = 15.0
MLSTM_COLS = MLSTM_QKW + 2 * MLSTM_DIM + 4 * MLSTM_HEADS

LANES = 128
SUBLANES = 8
CHUNK = HEAD_DIM
ROWS_PER_STEP = 4
SEQ_ROWS_PER_STEP = 8
KV_BLOCK = 256
SCORE_LOOKAHEAD = 2
QPAD = LANES
ZB_COLS = MLA_Q_RANK + MLA_KV_RANK + LANES
ZM_COLS = MLSTM_QKW + 2 * MLSTM_DIM
SHIFT_COLS = RWKV_COLS + MLSTM_QKW
IN_COLS_PAD = RWKV_COLS + ZM_COLS + LANES + ZB_COLS
ACT = BF16
VMEM_LIMIT = 56 * 1024 * 1024


def _split(a):
    hi = a.astype(BF16)
    lo = (a - hi.astype(F32)).astype(BF16)
    return hi, lo


_NN = (((1,), (0,)), ((), ()))
_NT = (((1,), (1,)), ((), ()))
_TN = (((0,), (0,)), ((), ()))


def _dg(a, b, dims):
    return lax.dot_general(a, b, dims, preferred_element_type=F32)


def _dot1(a, b, dims=_NN):
    return _dg(a.astype(BF16), b.astype(BF16), dims)


def _dot3(a, b, dims=_NN):
    ah, al = _split(a)
    bh, bl = _split(b)
    return _dg(ah, bh, dims) + (_dg(ah, bl, dims) + _dg(al, bh, dims))


def _dot_exact_rhs(a, b01, dims=_NN):
    a1 = a.astype(BF16)
    r1 = a - a1.astype(F32)
    a2 = r1.astype(BF16)
    a3 = (r1 - a2.astype(F32)).astype(BF16)
    return _dg(a1, b01, dims) + (_dg(a2, b01, dims) + _dg(a3, b01, dims))


def _dot_exact_lhs(a01, b, dims=_NN):
    b1 = b.astype(BF16)
    r1 = b - b1.astype(F32)
    b2 = r1.astype(BF16)
    b3 = (r1 - b2.astype(F32)).astype(BF16)
    return _dg(a01, b1, dims) + (_dg(a01, b2, dims) + _dg(a01, b3, dims))


def _dot2_rhs01(a, b01):
    ah, al = _split(a)
    return _dg(ah, b01, _NN) + _dg(al, b01, _NN)


def _iota(shape, dim):
    return lax.broadcasted_iota(jnp.int32, shape, dim)


def _seg_ones(n, seg):
    r = _iota((n, n), 0) // seg
    c = _iota((n, n), 1) // seg
    return (r == c).astype(BF16)


def _segsum(x, ones_bd):
    return _dot_exact_rhs(x, ones_bd)


def _softplus(x):
    return jnp.maximum(x, 0.0) + jnp.log(1.0 + jnp.exp(-jnp.abs(x)))


def _sigmoid(x):
    return 1.0 / (1.0 + jnp.exp(-x))


def _silu(x):
    return x * _sigmoid(x)


def _rmsnorm_mod(x, g, shift, scale):
    y = x * lax.rsqrt(jnp.mean(x * x, axis=-1, keepdims=True) + NORM_EPS)
    return (y * g) * (1.0 + scale) + shift


def _chunk_index(i, n_ctx_chunks, n_chunks, rev):
    if not rev:
        return i
    return jnp.where(i < n_ctx_chunks, n_ctx_chunks - 1 - i,
                     n_chunks - 1 - (i - n_ctx_chunks))


def _incl_mask(n, rev):
    r = _iota((n, n), 0)
    c = _iota((n, n), 1)
    return (c >= r) if rev else (c <= r)


def _mod_kernel(c_ref, w_ref, b_ref, o_ref):
    o_ref[0] = _dot3(_silu(c_ref[...]), w_ref[0]) + b_ref[0]


def _modulation(cvec, mod_w, mod_b):
    depth, d, n = mod_w.shape
    tn = 1536
    rows = cvec.shape[0]
    return pl.pallas_call(
        _mod_kernel,
        grid=(depth, n // tn),
        in_specs=[
            pl.BlockSpec((rows, d), lambda l, j: (0, 0)),
            pl.BlockSpec((1, d, tn), lambda l, j: (l, 0, j)),
            pl.BlockSpec((1, 1, tn), lambda l, j: (l, 0, j)),
        ],
        out_specs=pl.BlockSpec((1, rows, tn), lambda l, j: (l, 0, j)),
        out_shape=jax.ShapeDtypeStruct((depth, rows, n), F32),
        compiler_params=pltpu.CompilerParams(
            dimension_semantics=("arbitrary", "arbitrary"), vmem_limit_bytes=VMEM_LIMIT),
        name="adaln_mod",
    )(cvec, mod_w, mod_b.reshape(depth, 1, n))


def _inproj_kernel(x_ref, xp_ref, xn_ref, mod_ref, g_ref, w_ref, mu_ref, cw_ref, cb_ref,
                   za_ref, zm_ref, zg_ref, zb_ref, *, tm, n_ctx_tiles, n_tiles):
    j = pl.program_id(1)
    g = g_ref[...]
    shift = mod_ref[0, 0, 0:1, :]
    scale = mod_ref[0, 0, 1:2, :]
    h = _rmsnorm_mod(x_ref[0], g, shift, scale).astype(BF16)
    z = _dg(h, w_ref[...], _NN)
    halo = jnp.concatenate([xp_ref[0], xn_ref[0]], axis=0)
    hh = _rmsnorm_mod(halo, g, shift, scale).astype(BF16)
    zh = _dg(hh, w_ref[:, :SHIFT_COLS], _NN)
    prev_ok = jnp.logical_and(j != 0, j != n_ctx_tiles).astype(F32)
    next_ok = jnp.logical_and(j != n_ctx_tiles - 1, j != n_tiles - 1).astype(F32)
    zs = z[:, :SHIFT_COLS]
    row = _iota((tm, SHIFT_COLS), 0)
    prev = jnp.where(row == 0, zh[SUBLANES - 1:SUBLANES, :] * prev_ok, pltpu.roll(zs, 1, 0))
    nxt = jnp.where(row == tm - 1, zh[SUBLANES:SUBLANES + 1, :] * next_ok,
                    pltpu.roll(zs, tm - 1, 0))
    za = zs[:, :RWKV_COLS]
    shifted = za + mu_ref[...] * (0.5 * (prev[:, :RWKV_COLS] + nxt[:, :RWKV_COLS]) - za)
    za_ref[0] = shifted.astype(za_ref.dtype)
    qk = (prev[:, RWKV_COLS:] * cw_ref[0:1, :] + zs[:, RWKV_COLS:] * cw_ref[1:2, :]
          + nxt[:, RWKV_COLS:] * cw_ref[2:3, :] + cb_ref[...])
    zm_ref[0, :, :MLSTM_QKW] = _silu(qk).astype(zm_ref.dtype)
    zm_ref[0, :, MLSTM_QKW:] = z[:, SHIFT_COLS:RWKV_COLS + ZM_COLS].astype(zm_ref.dtype)
    gate_end = RWKV_COLS + ZM_COLS + LANES
    zg_ref[0] = z[:, RWKV_COLS + ZM_COLS:gate_end]
    zb_ref[0] = z[:, gate_end:].astype(zb_ref.dtype)


def _in_proj(x, modsel, norm_g, w_r, mu, conv_w, conv_b, *, tm, ctx_len):
    b, l, d = x.shape
    n_tiles = l // tm
    n_ctx_tiles = ctx_len // tm
    tm8 = tm // SUBLANES
    kern = functools.partial(_inproj_kernel, tm=tm, n_ctx_tiles=n_ctx_tiles, n_tiles=n_tiles)
    seg = lambda j: jnp.where(j >= n_ctx_tiles, 1, 0)
    return pl.pallas_call(
        kern,
        grid=(b, n_tiles),
        in_specs=[
            pl.BlockSpec((1, tm, d), lambda i, j: (i, j, 0)),
            pl.BlockSpec((1, SUBLANES, d), lambda i, j: (i, jnp.maximum(j * tm8 - 1, 0), 0)),
            pl.BlockSpec((1, SUBLANES, d),
                         lambda i, j: (i, jnp.minimum((j + 1) * tm8, l // SUBLANES - 1), 0)),
            pl.BlockSpec((1, 1, 2, d), lambda i, j: (i, seg(j), 0, 0)),
            pl.BlockSpec((1, d), lambda i, j: (0, 0)),
            pl.BlockSpec((d, IN_COLS_PAD), lambda i, j: (0, 0)),
            pl.BlockSpec((1, RWKV_COLS), lambda i, j: (0, 0)),
            pl.BlockSpec((3, MLSTM_QKW), lambda i, j: (0, 0)),
            pl.BlockSpec((1, MLSTM_QKW), lambda i, j: (0, 0)),
        ],
        out_specs=[
            pl.BlockSpec((1, tm, RWKV_COLS), lambda i, j: (i, j, 0)),
            pl.BlockSpec((1, tm, ZM_COLS), lambda i, j: (i, j, 0)),
            pl.BlockSpec((1, tm, LANES), lambda i, j: (i, j, 0)),
            pl.BlockSpec((1, tm, ZB_COLS), lambda i, j: (i, j, 0)),
        ],
        out_shape=[
            jax.ShapeDtypeStruct((b, l, RWKV_COLS), ACT),
            jax.ShapeDtypeStruct((b, l, ZM_COLS), ACT),
            jax.ShapeDtypeStruct((b, l, LANES), F32),
            jax.ShapeDtypeStruct((b, l, ZB_COLS), ACT),
        ],
        compiler_params=pltpu.CompilerParams(
            dimension_semantics=("arbitrary", "arbitrary"), vmem_limit_bytes=VMEM_LIMIT),
        name="in_proj",
    )(x, x, x, modsel, norm_g.reshape(1, d), w_r, mu.reshape(1, RWKV_COLS), conv_w,
      conv_b.reshape(1, MLSTM_QKW))


def _rwkv_local_kernel(za_ref, kk_ref, ka_ref, rk_ref, w0_ref, w2_ref, a0_ref, a2_ref, g2_ref,
                       g_ref, yl_ref, at_ref, bs_ref, bonus_ref, gate_ref):
    c = CHUNK
    nd = RWKV_DIM
    hd = HEAD_DIM
    nh = RWKV_HEADS
    nb = za_ref.shape[0]
    ones_bd = _seg_ones(nd, hd)
    eye4 = _iota((hd, nd), 0) == (_iota((hd, nd), 1) % hd)
    eye4_f = eye4.astype(F32)
    row2 = _iota((2 * c, 2 * nd), 0)
    rr = row2 % c
    cc = _iota((2 * c, 2 * nd), 1) % c
    diag_ok = jnp.logical_and(cc == rr, row2 >= c)
    keep = _diag_mask(nd, nd, hd, hd)
    bd = lambda x: _block_diag(x, nh, hd, hd, keep)
    units, pr, rhs, vb, pm, rm, qe, ke, gam, gmask = ([] for _ in range(10))
    for rb in range(nb):
        za = za_ref[rb].astype(F32)
        r = za[:, 0:nd]
        k = za[:, nd:2 * nd]
        v = za[:, 2 * nd:3 * nd]
        kkr = k * kk_ref[...]
        kk = kkr / jnp.maximum(jnp.sqrt(_segsum(kkr * kkr, ones_bd)), 1e-12)
        bonus_ref[rb] = (_segsum(r * k * rk_ref[...], ones_bd) * v).astype(bonus_ref.dtype)
        gd = za[:, 3 * nd + 2 * RWKV_W_LORA + 2 * RWKV_A_LORA:]
        gate_ref[rb] = _dot1(_sigmoid(gd), g2_ref[...]).astype(gate_ref.dtype)
        v_b = v.astype(BF16)
        for d in range(2):
            rev = d == 1
            wo = 3 * nd + d * RWKV_W_LORA
            ao = 3 * nd + 2 * RWKV_W_LORA + d * RWKV_A_LORA
            w_lo = _dot1(jnp.tanh(za[:, wo:wo + RWKV_W_LORA]), w2_ref[d])
            log_w = -_softplus(-(w0_ref[d] + w_lo)) - 0.5
            logdec = -jnp.exp(log_w)
            a = _sigmoid(a0_ref[d] + _dot1(za[:, ao:ao + RWKV_A_LORA], a2_ref[d]))
            kd = k * (1.0 + (a - 1.0) * ka_ref[...])
            bvec = kk * a
            cum = _dot_exact_lhs(_incl_mask(c, rev).astype(BF16), logdec)
            total = cum[0:1, :] if rev else cum[c - 1:c, :]
            e_neg = jnp.exp(-cum)
            e_end = jnp.exp(total - cum)
            pm_d = kk * jnp.exp(cum - logdec)
            rm_d = r * jnp.exp(cum)
            units.append((rb, d))
            pm.append(pm_d)
            rm.append(rm_d)
            pr.append(jnp.concatenate([pm_d, rm_d], axis=0).astype(BF16))
            rhs.append(jnp.concatenate([bd((bvec * e_neg).astype(BF16)),
                                        bd((kd * e_neg).astype(BF16))], axis=0))
            vb.append(v_b)
            qe.append((bvec * e_end).astype(BF16))
            ke.append((kd * e_end).astype(BF16))
            gam.append(jnp.exp(total))
            gmask.append(jnp.logical_or(cc > rr if rev else cc < rr, diag_ok))
    n = len(units)
    a4 = [jnp.where(gmask[i], _dg(pr[i], rhs[i], _NT), 0.0) for i in range(n)]
    a4b = [x.astype(BF16) for x in a4]
    pw = [_dg(a4b[i][:c, :nd], bd(a4b[i][:c, :nd]), _NN) for i in range(n)]
    lv = [_dg(a4b[i][:, nd:], bd(vb[i]), _NN) for i in range(n)]
    tinv = [eye4_f - a4[i][:c, :nd] for i in range(n)]
    covered = 2
    while True:
        pwd = [bd(x.astype(BF16)) for x in pw]
        tinv = [tinv[i] + _dg(tinv[i].astype(BF16), pwd[i], _NN) for i in range(n)]
        covered *= 2
        if covered >= c:
            break
        pw = [_dg(pw[i].astype(BF16), pwd[i], _NN) for i in range(n)]
    wz = [_dg(tinv[i].astype(BF16),
              jnp.concatenate([bd(pm[i].astype(BF16)), bd(lv[i][:c].astype(BF16))], axis=1),
              _NN).astype(BF16) for i in range(n)]
    awz = [_dg(a4b[i][c:, :nd],
               jnp.concatenate([bd(wz[i][:, :nd]), bd(wz[i][:, nd:])], axis=1), _NN)
           for i in range(n)]
    eye = _iota((hd, hd), 0) == _iota((hd, hd), 1)
    for i, (rb, d) in enumerate(units):
        g_ref[d, rb] = (rm[i] - awz[i][:, :nd]).astype(g_ref.dtype)
        yl_ref[d, rb] = (lv[i][c:] - awz[i][:, nd:]).astype(yl_ref.dtype)
        ats, bss = [], []
        for h in range(nh):
            sl = slice(h * hd, (h + 1) * hd)
            wz_h = jnp.concatenate([wz[i][:, sl], wz[i][:, nd + h * hd:nd + (h + 1) * hd]], axis=1)
            qwz = _dg(qe[i][:, sl], wz_h, _TN)
            ats.append(jnp.where(eye, jnp.broadcast_to(gam[i][:, sl], (hd, hd)), 0.0)
                       - qwz[:, :hd])
            bss.append(_dg(ke[i][:, sl], vb[i][:, sl], _TN) - qwz[:, hd:])
        at_ref[d, rb] = jnp.concatenate(ats, axis=1).astype(at_ref.dtype)
        bs_ref[d, rb] = jnp.concatenate(bss, axis=1).astype(bs_ref.dtype)


def _rwkv_local(za, k_k, k_a, r_k, w0, w2, a0, a2, g2, *, nb):
    b, l, _ = za.shape
    c = CHUNK
    nd = RWKV_DIM
    full = lambda shape: pl.BlockSpec(shape, lambda i, j: (0,) * len(shape))
    dir_spec = pl.BlockSpec((2, nb, c, nd), lambda i, j: (0, i, j, 0))
    dir_shape = jax.ShapeDtypeStruct((2, b, l, nd), ACT)
    row_spec = pl.BlockSpec((nb, c, nd), lambda i, j: (i, j, 0))
    row_shape = jax.ShapeDtypeStruct((b, l, nd), ACT)
    return pl.pallas_call(
        _rwkv_local_kernel,
        grid=(b // nb, l // c),
        in_specs=[
            pl.BlockSpec((nb, c, RWKV_COLS), lambda i, j: (i, j, 0)),
            full((1, nd)), full((1, nd)), full((1, nd)),
            full((2, 1, nd)), full((2, RWKV_W_LORA, nd)),
            full((2, 1, nd)), full((2, RWKV_A_LORA, nd)),
            full((RWKV_G_LORA, nd)),
        ],
        out_specs=[dir_spec] * 4 + [row_spec] * 2,
        out_shape=[dir_shape] * 4 + [row_shape] * 2,
        compiler_params=pltpu.CompilerParams(
            dimension_semantics=("arbitrary", "arbitrary"), vmem_limit_bytes=VMEM_LIMIT),
        name="rwkv_local",
    )(za, k_k.reshape(1, -1), k_a.reshape(1, -1), r_k.reshape(1, -1), w0.reshape(2, 1, nd),
      w2, a0.reshape(2, 1, nd), a2, g2)


def _rwkv_seq_kernel(gf_ref, ylf_ref, atf_ref, bsf_ref, gr_ref, ylr_ref, atr_ref, bsr_ref,
                     yf_ref, yr_ref, m_ref):
    nb = yf_ref.shape[0]

    @pl.when(pl.program_id(1) == 0)
    def _():
        m_ref[...] = jnp.zeros_like(m_ref)

    dirs = ((gf_ref, ylf_ref, atf_ref, bsf_ref, yf_ref), (gr_ref, ylr_ref, atr_ref, bsr_ref, yr_ref))
    units = [(rb, d) for rb in range(nb) for d in range(2)]
    keep = _diag_mask(RWKV_DIM, RWKV_DIM, HEAD_DIM, HEAD_DIM)
    m_bd = [_block_diag(m_ref[u].astype(BF16), RWKV_HEADS, HEAD_DIM, HEAD_DIM, keep)
            for u in range(len(units))]
    y = [_dg(dirs[d][0][0, rb], m_bd[u], _NN) for u, (rb, d) in enumerate(units)]
    m_new = [_dg(dirs[d][2][0, rb], m_bd[u], _NN) for u, (rb, d) in enumerate(units)]
    for u, (rb, d) in enumerate(units):
        dirs[d][4][rb] = (y[u] + dirs[d][1][0, rb].astype(F32)).astype(dirs[d][4].dtype)
        m_ref[u] = m_new[u] + dirs[d][3][0, rb].astype(F32)


def _rwkv_seq(g, yl, at, bs, *, ctx_len, nb):
    _, b, l, nd = g.shape
    c = CHUNK
    n_chunks = l // c
    n_ctx_chunks = ctx_len // c
    rev_idx = lambda j: _chunk_index(j, n_ctx_chunks, n_chunks, True)
    fwd = pl.BlockSpec((1, nb, c, nd), lambda i, j: (0, i, j, 0))
    bwd = pl.BlockSpec((1, nb, c, nd), lambda i, j: (1, i, rev_idx(j), 0))
    y_shape = jax.ShapeDtypeStruct((b, l, nd), ACT)
    return pl.pallas_call(
        _rwkv_seq_kernel,
        grid=(b // nb, n_chunks),
        in_specs=[fwd] * 4 + [bwd] * 4,
        out_specs=[pl.BlockSpec((nb, c, nd), lambda i, j: (i, j, 0)),
                   pl.BlockSpec((nb, c, nd), lambda i, j: (i, rev_idx(j), 0))],
        out_shape=[y_shape, y_shape],
        scratch_shapes=[pltpu.VMEM((nb * 2, HEAD_DIM, RWKV_DIM), F32)],
        compiler_params=pltpu.CompilerParams(
            dimension_semantics=("arbitrary", "arbitrary"), vmem_limit_bytes=VMEM_LIMIT),
        name="rwkv_seq",
    )(g, yl, at, bs, g, yl, at, bs)


def _cummax_rows(x, rev):
    n = x.shape[0]
    row = _iota(x.shape, 0)
    shift = 1
    while shift < n:
        if rev:
            moved = jnp.where(row >= n - shift, -jnp.inf, pltpu.roll(x, n - shift, 0))
        else:
            moved = jnp.where(row < shift, -jnp.inf, pltpu.roll(x, shift, 0))
        x = jnp.maximum(x, moved)
        shift *= 2
    return x


def _head_lane(d):
    return d * 2 * MLSTM_HEADS + MLSTM_HEADS


def _sel_expand(d, width):
    n = MLSTM_HEADS * width
    return (_iota((LANES, n), 0) == _head_lane(d) + _iota((LANES, n), 1) // width).astype(BF16)


def _sel_reduce(d, width):
    n = MLSTM_HEADS * width
    return (_iota((n, LANES), 1) == _head_lane(d) + _iota((n, LANES), 0) // width).astype(BF16)


def _diag_mask(rows, cols, row_block, col_block):
    return (_iota((rows, cols), 0) // row_block) == (_iota((rows, cols), 1) // col_block)


def _block_diag(x, reps, row_block, col_block, keep=None):
    t = jnp.concatenate([x] * reps, axis=0)
    if keep is None:
        keep = _diag_mask(t.shape[0], t.shape[1], row_block, col_block)
    return jnp.where(keep, t, jnp.zeros_like(t))


def _mlstm_local_kernel(zm_ref, zg_ref, gb_ref, num_ref, den_ref, mi_ref, b_ref, cl_ref,
                        rows_ref):
    c = CHUNK
    nh = MLSTM_HEADS
    dk = MLSTM_QK
    dv = MLSTM_V
    nb = zm_ref.shape[0]
    lane_j = _iota((c, nh * c), 1) % c
    row_s = _iota((c, nh * c), 0)
    cl_keep = _diag_mask(nh * dk, MLSTM_DIM, dk, dv)
    k_keep = _diag_mask(nh * c, nh * dk, c, dk)
    v_keep = _diag_mask(nh * c, MLSTM_DIM, c, dv)
    sel_c = [_sel_expand(d, c) for d in range(2)]
    sel_k = [_sel_expand(d, dk) for d in range(2)]
    sel_r = [_sel_reduce(d, c) for d in range(2)]
    k, vb, log_f, li, qk, vbd = [], [], [], [], [], []
    for rb in range(nb):
        zm = zm_ref[rb]
        qb = (zm[:, :nh * dk].astype(F32) * (dk ** -0.5)).astype(BF16)
        k.append(zm[:, nh * dk:MLSTM_QKW].astype(F32))
        vb.append(zm[:, MLSTM_QKW:MLSTM_QKW + MLSTM_DIM].astype(BF16))
        gates = zg_ref[rb] + gb_ref[...]
        capped = GATE_SOFTCAP * jnp.tanh(gates * (1.0 / GATE_SOFTCAP))
        log_f.append(-_softplus(-capped))
        li.append(pltpu.roll(capped, nh, 1))
        qk.append(_dg(qb, _block_diag(k[rb].astype(BF16), nh, c, dk, k_keep), _NT))
        vbd.append(_block_diag(vb[rb], nh, c, dv, v_keep))
    units = [(rb, d) for rb in range(nb) for d in range(2)]
    b_all = [_dot_exact_lhs(_incl_mask(c, d == 1).astype(BF16), log_f[rb]) for rb, d in units]
    x = [li[rb] - b_all[u] for u, (rb, d) in enumerate(units)]
    cm = [_cummax_rows(x[u], d == 1) for u, (rb, d) in enumerate(units)]
    e1 = [_dot_exact_rhs(-cm[u], sel_c[d]) for u, (rb, d) in enumerate(units)]
    xe = [_dot_exact_rhs(x[u], sel_c[d]) for u, (rb, d) in enumerate(units)]
    g_row, m_loc, e32 = [], [], []
    for u, (rb, d) in enumerate(units):
        last = 0 if d == 1 else c - 1
        g_row.append(b_all[u][last:last + 1, :])
        w_end = g_row[u] - b_all[u] + li[rb]
        m_loc.append(jnp.max(w_end, axis=0, keepdims=True))
        e32.append(_dot2_rhs01(jnp.exp(w_end - m_loc[u]), sel_k[d]))
    wi = []
    for u, (rb, d) in enumerate(units):
        x_row = jnp.sum(jnp.where(row_s == lane_j, xe[u], 0.0), axis=0, keepdims=True)
        earlier = (lane_j >= row_s) if d == 1 else (lane_j <= row_s)
        wi.append((qk[rb] * jnp.exp(jnp.where(earlier, e1[u] + x_row, -jnp.inf))).astype(BF16))
    nd = [_dg(wi[u], jnp.concatenate([vbd[rb], sel_r[d]], axis=1), _NN)
          for u, (rb, d) in enumerate(units)]
    ke = [k[rb] * e32[u] for u, (rb, d) in enumerate(units)]
    full = [_dg(ke[u].astype(BF16), vb[rb], _TN) for u, (rb, d) in enumerate(units)]
    for u, (rb, d) in enumerate(units):
        num_ref[d, rb] = nd[u][:, :MLSTM_DIM].astype(num_ref.dtype)
        den_ref[d, rb] = nd[u][:, MLSTM_DIM:]
        mi_ref[d, rb] = b_all[u] + cm[u]
        b_ref[d, rb] = b_all[u]
        f = jnp.where(cl_keep, full[u], 0.0)
        cl_ref[d, rb] = ((f[0:dk] + f[dk:2 * dk])
                         + (f[2 * dk:3 * dk] + f[3 * dk:4 * dk])).astype(cl_ref.dtype)
        rows_ref[d, rb] = jnp.concatenate(
            [jnp.sum(ke[u], axis=0, keepdims=True), m_loc[u], g_row[u],
             jnp.zeros((SUBLANES - 3, LANES), F32)], axis=0)


def _mlstm_local(zm, zg, gate_bias, *, nb):
    b, l, _ = zm.shape
    c = CHUNK
    n_chunks = l // c
    spec = lambda rows, w, dt: pl.BlockSpec((2, nb, rows, w), lambda i, j: (0, i, j, 0))
    shape = lambda rows, w, dt: jax.ShapeDtypeStruct((2, b, n_chunks * rows, w), dt)
    outs = [(c, MLSTM_DIM, ACT), (c, LANES, F32), (c, LANES, F32), (c, LANES, F32),
            (MLSTM_QK, MLSTM_DIM, ACT), (SUBLANES, LANES, F32)]
    return pl.pallas_call(
        _mlstm_local_kernel,
        grid=(b // nb, n_chunks),
        in_specs=[
            pl.BlockSpec((nb, c, ZM_COLS), lambda i, j: (i, j, 0)),
            pl.BlockSpec((nb, c, LANES), lambda i, j: (i, j, 0)),
            pl.BlockSpec((1, LANES), lambda i, j: (0, 0)),
        ],
        out_specs=[spec(*o) for o in outs],
        out_shape=[shape(*o) for o in outs],
        compiler_params=pltpu.CompilerParams(
            dimension_semantics=("arbitrary", "arbitrary"), vmem_limit_bytes=VMEM_LIMIT),
        name="mlstm_local",
    )(zm, zg, gate_bias)


def _mlstm_seq_kernel(qf_ref, numf_ref, denf_ref, mif_ref, bf_ref, clf_ref, rowsf_ref,
                      qr_ref, numr_ref, denr_ref, mir_ref, br_ref, clr_ref, rowsr_ref,
                      hf_ref, hr_ref, cbd_ref, n_ref, m_ref):
    c = CHUNK
    nh = MLSTM_HEADS
    dk = MLSTM_QK
    dv = MLSTM_V
    nb = hf_ref.shape[0]

    @pl.when(pl.program_id(1) == 0)
    def _():
        cbd_ref[...] = jnp.zeros_like(cbd_ref)
        n_ref[...] = jnp.zeros_like(n_ref)
        m_ref[...] = jnp.zeros_like(m_ref)

    dirs = ((qf_ref, numf_ref, denf_ref, mif_ref, bf_ref, clf_ref, rowsf_ref, hf_ref),
            (qr_ref, numr_ref, denr_ref, mir_ref, br_ref, clr_ref, rowsr_ref, hr_ref))
    units = [(rb, d) for rb in range(nb) for d in range(2)]
    nu = len(units)
    cl_keep = _diag_mask(nh * dk, MLSTM_DIM, dk, dv)
    sel_v = [_sel_expand(d, dv) for d in range(2)]
    sel_k = [_sel_expand(d, dk) for d in range(2)]
    sel_r = [_sel_reduce(d, dk) for d in range(2)]
    q = [dirs[d][0][rb].astype(F32) * (dk ** -0.5) for rb, d in units]
    rows = [dirs[d][6][0, rb] for rb, d in units]
    m_row = [m_ref[u] for u in range(nu)]
    n_row = [n_ref[u] for u in range(nu)]
    cbd = [cbd_ref[u] for u in range(nu)]
    qn = [_dot2_rhs01(q[u] * n_row[u], sel_r[d]) for u, (rb, d) in enumerate(units)]
    qc = [_dg(q[u].astype(BF16), cbd[u].astype(BF16), _NN) for u in range(nu)]
    a12, srow = [], []
    for u, (rb, d) in enumerate(units):
        mi = dirs[d][3][0, rb]
        log_inter = dirs[d][4][0, rb] + m_row[u]
        m_out = jnp.maximum(log_inter, mi)
        s_intra = jnp.exp(mi - m_out)
        s_inter = jnp.exp(log_inter - m_out)
        den = s_intra * dirs[d][2][0, rb] + s_inter * qn[u]
        dinv = 1.0 / jnp.maximum(jnp.abs(den), jnp.exp(-m_out))
        a12.append(jnp.concatenate([s_intra * dinv, s_inter * dinv], axis=0))
        m_loc, g_row = rows[u][1:2], rows[u][2:3]
        m_new = jnp.maximum(g_row + m_row[u], m_loc)
        srow.append(jnp.concatenate(
            [jnp.exp(g_row + m_row[u] - m_new), jnp.exp(m_loc - m_new),
             jnp.zeros((SUBLANES - 2, LANES), F32)], axis=0))
        m_ref[u] = m_new
    a12e = [_dot2_rhs01(a12[u], sel_v[d]) for u, (rb, d) in enumerate(units)]
    s_v = [_dot2_rhs01(srow[u], sel_v[d]) for u, (rb, d) in enumerate(units)]
    s_k = [_dot2_rhs01(srow[u], sel_k[d]) for u, (rb, d) in enumerate(units)]
    for u, (rb, d) in enumerate(units):
        h_out = a12e[u][:c] * dirs[d][1][0, rb].astype(F32) + a12e[u][c:] * qc[u]
        dirs[d][7][rb] = h_out.astype(dirs[d][7].dtype)
        cl_full = _block_diag(dirs[d][5][0, rb].astype(F32), nh, dk, dv, cl_keep)
        cbd_ref[u] = s_v[u][0:1] * cbd[u] + s_v[u][1:2] * cl_full
        n_ref[u] = s_k[u][0:1] * n_row[u] + s_k[u][1:2] * rows[u][0:1]


def _mlstm_seq(zm, num, den, mi, bsum, cl, rows, *, ctx_len, nb):
    b, l, _ = zm.shape
    c = CHUNK
    n_chunks = l // c
    n_ctx_chunks = ctx_len // c
    rev_idx = lambda j: _chunk_index(j, n_ctx_chunks, n_chunks, True)

    def specs(d):
        cidx = (lambda j: j) if d == 0 else rev_idx
        blk = lambda r, w: pl.BlockSpec((1, nb, r, w), lambda i, j: (d, i, cidx(j), 0))
        return [pl.BlockSpec((nb, c, MLSTM_HEADS * MLSTM_QK), lambda i, j: (i, cidx(j), 0)),
                blk(c, MLSTM_DIM), blk(c, LANES), blk(c, LANES), blk(c, LANES),
                blk(MLSTM_QK, MLSTM_DIM), blk(SUBLANES, LANES)]

    h_shape = jax.ShapeDtypeStruct((b, l, MLSTM_DIM), ACT)
    args = (zm, num, den, mi, bsum, cl, rows)
    return pl.pallas_call(
        _mlstm_seq_kernel,
        grid=(b // nb, n_chunks),
        in_specs=specs(0) + specs(1),
        out_specs=[pl.BlockSpec((nb, c, MLSTM_DIM), lambda i, j: (i, j, 0)),
                   pl.BlockSpec((nb, c, MLSTM_DIM), lambda i, j: (i, rev_idx(j), 0))],
        out_shape=[h_shape, h_shape],
        scratch_shapes=[
            pltpu.VMEM((2 * nb, MLSTM_HEADS * MLSTM_QK, MLSTM_DIM), F32),
            pltpu.VMEM((2 * nb, 1, LANES), F32),
            pltpu.VMEM((2 * nb, 1, LANES), F32),
        ],
        compiler_params=pltpu.CompilerParams(
            dimension_semantics=("arbitrary", "arbitrary"), vmem_limit_bytes=VMEM_LIMIT),
        name="mlstm_seq",
    )(*args, *args)


def _rope_swap(x):
    lane = _iota(x.shape, 1)
    half = MLA_ROPE // 2
    return jnp.where(lane < MLA_NOPE + half, pltpu.roll(x, LANES - half, 1),
                     pltpu.roll(x, half, 1))


def _mla_proj_kernel(zb_ref, cq_ref, sq_ref, ck_ref, sk_ref, qg_ref, kvg_ref,
                     wq_ref, wk_ref, wv_ref, q_ref, k_ref, v_ref):
    hq = MLA_HEADS * QPAD
    zb = zb_ref[0].astype(F32)
    cq = zb[:, :MLA_Q_RANK]
    ckv = zb[:, MLA_Q_RANK:MLA_Q_RANK + MLA_KV_RANK]
    kr = zb[:, MLA_Q_RANK + MLA_KV_RANK:]
    cqn = cq * lax.rsqrt(jnp.mean(cq * cq, axis=-1, keepdims=True) + NORM_EPS) * qg_ref[...]
    ckn = ckv * lax.rsqrt(jnp.mean(ckv * ckv, axis=-1, keepdims=True) + NORM_EPS) * kvg_ref[...]
    q_all = _dot1(cqn, wq_ref[...])
    k_all = _dot1(ckn, wk_ref[...])
    v_ref[0] = _dg(wv_ref[...], ckn.astype(BF16), _NT).astype(BF16)
    cos_q, sin_q, cos_k = cq_ref[...], sq_ref[...], ck_ref[...]
    kr_rot = _rope_swap(kr) * sk_ref[...]
    inv_dim = 1.0 / MLA_QK
    for h in range(MLA_HEADS):
        sl = slice(h * QPAD, (h + 1) * QPAD)
        qh = q_all[:, sl]
        q_inv = lax.rsqrt(jnp.sum(qh * qh, axis=-1, keepdims=True) * inv_dim + NORM_EPS)
        q_ref[0, :, sl] = ((qh * cos_q + q_all[:, hq + h * QPAD:hq + (h + 1) * QPAD] * sin_q)
                           * q_inv).astype(BF16)
        kh = k_all[:, sl] + kr
        k_inv = lax.rsqrt(jnp.sum(kh * kh, axis=-1, keepdims=True) * inv_dim + NORM_EPS)
        k_ref[0, :, sl] = ((kh * cos_k + kr_rot) * k_inv).astype(BF16)


def _mla_proj(zb, tables, q_norm_g, kv_norm_g, wq_r, wk_r, wv_r, *, tm):
    b, l, _ = zb.shape
    hq = MLA_HEADS * QPAD
    full = lambda shape: pl.BlockSpec(shape, lambda i, j: (0,) * len(shape))
    table = pl.BlockSpec((tm, QPAD), lambda i, j: (j, 0))
    return pl.pallas_call(
        _mla_proj_kernel,
        grid=(b, l // tm),
        in_specs=[
            pl.BlockSpec((1, tm, ZB_COLS), lambda i, j: (i, j, 0)),
            table, table, table, table,
            full((1, MLA_Q_RANK)), full((1, MLA_KV_RANK)),
            full((MLA_Q_RANK, 2 * hq)), full((MLA_KV_RANK, hq)), full((MLA_DIM, MLA_KV_RANK)),
        ],
        out_specs=[
            pl.BlockSpec((1, tm, hq), lambda i, j: (i, j, 0)),
            pl.BlockSpec((1, tm, hq), lambda i, j: (i, j, 0)),
            pl.BlockSpec((1, MLA_DIM, tm), lambda i, j: (i, 0, j)),
        ],
        out_shape=[
            jax.ShapeDtypeStruct((b, l, hq), BF16),
            jax.ShapeDtypeStruct((b, l, hq), BF16),
            jax.ShapeDtypeStruct((b, MLA_DIM, l), BF16),
        ],
        compiler_params=pltpu.CompilerParams(
            dimension_semantics=("arbitrary", "arbitrary"), vmem_limit_bytes=VMEM_LIMIT),
        name="mla_proj",
    )(zb, *tables, q_norm_g.reshape(1, -1), kv_norm_g.reshape(1, -1), wq_r, wk_r, wv_r)


def _attend_tile(q_ref, k_ref, vt_ref, o_ref, n_blocks):
    tq = q_ref.shape[1]
    bk = KV_BLOCK
    q = [q_ref[0, :, h * QPAD:(h + 1) * QPAD] for h in range(2)]

    def scores(j):
        return [_dg(k_ref[0, j * bk:(j + 1) * bk, h * QPAD:(h + 1) * QPAD], q[h], _NT)
                for h in range(2)]

    ones_rows = jnp.ones((2 * SUBLANES, bk), BF16)
    m = [jnp.full((1, tq), -jnp.inf, F32) for _ in range(2)]
    acc = [jnp.zeros((MLA_V + 2 * SUBLANES, tq), F32) for _ in range(2)]
    pending = [scores(j) for j in range(min(SCORE_LOOKAHEAD, n_blocks))]
    for j in range(n_blocks):
        if j + SCORE_LOOKAHEAD < n_blocks:
            pending.append(scores(j + SCORE_LOOKAHEAD))
        s = pending.pop(0)
        for h in range(2):
            vt = jnp.concatenate(
                [vt_ref[0, h * MLA_V:(h + 1) * MLA_V, j * bk:(j + 1) * bk], ones_rows], axis=0)
            m_new = jnp.maximum(m[h], jnp.max(s[h], axis=0, keepdims=True))
            p = jnp.exp2(s[h] - m_new).astype(BF16)
            acc[h] = jnp.exp2(m[h] - m_new) * acc[h] + _dg(vt, p, _NN)
            m[h] = m_new
    out_t = jnp.concatenate([acc[h][:MLA_V] / acc[h][MLA_V:MLA_V + 1] for h in range(2)],
                            axis=0)
    o_ref[0] = out_t.T.astype(o_ref.dtype)


def _mla_attn_kernel(q_ref, k_ref, v_ref, o_ref, *, ctx_len, first_tile, n_ctx_tiles):
    n_all = k_ref.shape[1] // KV_BLOCK
    if first_tile >= n_ctx_tiles:
        _attend_tile(q_ref, k_ref, v_ref, o_ref, n_all)
        return
    is_ctx = (pl.program_id(2) + first_tile) < n_ctx_tiles

    @pl.when(is_ctx)
    def _():
        _attend_tile(q_ref, k_ref, v_ref, o_ref, ctx_len // KV_BLOCK)

    @pl.when(jnp.logical_not(is_ctx))
    def _():
        _attend_tile(q_ref, k_ref, v_ref, o_ref, n_all)


def _mla_attn(q, k, v, *, tq, ctx_len, first_tile):
    b, l, _ = q.shape
    assert l % KV_BLOCK == 0 and ctx_len % KV_BLOCK == 0
    n_q = l // tq - first_tile
    pairs = MLA_HEADS // 2
    kern = functools.partial(_mla_attn_kernel, ctx_len=ctx_len, first_tile=first_tile,
                             n_ctx_tiles=ctx_len // tq)
    return pl.pallas_call(
        kern,
        grid=(b, pairs, n_q),
        in_specs=[
            pl.BlockSpec((1, tq, 2 * QPAD), lambda i, p, j: (i, j + first_tile, p)),
            pl.BlockSpec((1, l, 2 * QPAD), lambda i, p, j: (i, 0, p)),
            pl.BlockSpec((1, 2 * MLA_V, l), lambda i, p, j: (i, p, 0)),
        ],
        out_specs=pl.BlockSpec((1, tq, 2 * MLA_V), lambda i, p, j: (i, j + first_tile, p)),
        out_shape=jax.ShapeDtypeStruct((b, l, MLA_DIM), ACT),
        compiler_params=pltpu.CompilerParams(
            dimension_semantics=("arbitrary", "arbitrary", "arbitrary"),
            vmem_limit_bytes=VMEM_LIMIT),
        name="mla_attn",
    )(q, k, v)


def _outproj_kernel(x_ref, yf_ref, yr_ref, bonus_ref, gate_ref, attn_ref, hf_ref, hr_ref,
                    o_ref, mod_ref, lng_ref, lnb_ref, mng_ref, wa_ref, wb_ref, wm_ref, out_ref):
    ones_bd = _seg_ones(RWKV_DIM, HEAD_DIM)
    inv = 1.0 / HEAD_DIM
    y = yf_ref[0].astype(F32) + yr_ref[0].astype(F32)
    mean = _segsum(y, ones_bd) * inv
    yc = y - mean
    var = _segsum(yc * yc, ones_bd) * inv
    yn = yc * lax.rsqrt(var + RWKV_GN_EPS) * lng_ref[...] + lnb_ref[...]
    ya = (yn + bonus_ref[0].astype(F32)) * gate_ref[0].astype(F32)
    hm = hf_ref[0].astype(F32) + hr_ref[0].astype(F32)
    hn = hm * lax.rsqrt(_segsum(hm * hm, ones_bd) * inv + NORM_EPS)
    ym = hn * mng_ref[...] * _sigmoid(o_ref[0].astype(F32))
    mix = (_dot1(ya, wa_ref[...]) + _dot1(attn_ref[0], wb_ref[...]) + _dot1(ym, wm_ref[...]))
    out_ref[0] = x_ref[0] + mod_ref[0, 0] * mix


def _out_proj(x, yf, yr, bonus, gate, attn, hf, hr, zm, modsel, ln_g, ln_b, mnorm_g,
              wa, wb, wm, *, tm, ctx_len, first_tile):
    b, l, d = x.shape
    n_ctx_tiles = ctx_len // tm
    n_tiles = l // tm - first_tile
    row = lambda w: pl.BlockSpec((1, tm, w), lambda i, j: (i, j + first_tile, 0))
    full = lambda shape: pl.BlockSpec(shape, lambda i, j: (0,) * len(shape))
    seg = lambda j: jnp.where(j + first_tile >= n_ctx_tiles, 1, 0)
    o_block = (MLSTM_QKW + MLSTM_DIM) // MLSTM_DIM
    return pl.pallas_call(
        _outproj_kernel,
        grid=(b, n_tiles),
        in_specs=[
            row(d), row(RWKV_DIM), row(RWKV_DIM), row(RWKV_DIM), row(RWKV_DIM), row(MLA_DIM),
            row(MLSTM_DIM), row(MLSTM_DIM),
            pl.BlockSpec((1, tm, MLSTM_DIM), lambda i, j: (i, j + first_tile, o_block)),
            pl.BlockSpec((1, 1, 1, d), lambda i, j: (i, seg(j), 0, 0)),
            full((1, RWKV_DIM)), full((1, RWKV_DIM)), full((1, MLSTM_DIM)),
            full((RWKV_DIM, d)), full((MLA_DIM, d)), full((MLSTM_DIM, d)),
        ],
        out_specs=pl.BlockSpec((1, tm, d), lambda i, j: (i, j, 0)),
        out_shape=jax.ShapeDtypeStruct((b, n_tiles * tm, d), F32),
        compiler_params=pltpu.CompilerParams(
            dimension_semantics=("arbitrary", "arbitrary"), vmem_limit_bytes=VMEM_LIMIT),
        name="out_proj",
    )(x, yf, yr, bonus, gate, attn, hf, hr, zm, modsel, ln_g.reshape(1, -1),
      ln_b.reshape(1, -1), mnorm_g.reshape(1, -1), wa, wb, wm)


def _ffn_kernel(x_ref, mod_ref, g_ref, w1_ref, w2_ref, out_ref, *, hidden):
    x = x_ref[0]
    h = _rmsnorm_mod(x, g_ref[...], mod_ref[0, 0, 0:1, :], mod_ref[0, 0, 1:2, :]).astype(BF16)
    gu = _dg(h, w1_ref[...], _NN)
    act = (_silu(gu[:, :hidden]) * gu[:, hidden:]).astype(BF16)
    out_ref[0] = x + mod_ref[0, 0, 2:3, :] * _dg(act, w2_ref[...], _NN)


def _ffn(x, modsel, norm_g, w1, w2, *, tm, n_ctx_tiles_here):
    b, l, d = x.shape
    hidden = w2.shape[0]
    seg = lambda j: jnp.where(j >= n_ctx_tiles_here, 1, 0)
    return pl.pallas_call(
        functools.partial(_ffn_kernel, hidden=hidden),
        grid=(b, l // tm),
        in_specs=[
            pl.BlockSpec((1, tm, d), lambda i, j: (i, j, 0)),
            pl.BlockSpec((1, 1, 3, d), lambda i, j: (i, seg(j), 0, 0)),
            pl.BlockSpec((1, d), lambda i, j: (0, 0)),
            pl.BlockSpec((d, 2 * hidden), lambda i, j: (0, 0)),
            pl.BlockSpec((hidden, d), lambda i, j: (0, 0)),
        ],
        out_specs=pl.BlockSpec((1, tm, d), lambda i, j: (i, j, 0)),
        out_shape=jax.ShapeDtypeStruct((b, l, d), F32),
        compiler_params=pltpu.CompilerParams(
            dimension_semantics=("arbitrary", "arbitrary"), vmem_limit_bytes=VMEM_LIMIT),
        name="ffn",
    )(x, modsel, norm_g.reshape(1, d), w1, w2)


def _arrange_w_in(w_in):
    d = w_in.shape[0]
    wa = w_in[:, :RWKV_COLS]
    wb = w_in[:, RWKV_COLS:RWKV_COLS + MLA_COLS]
    wm = w_in[:, RWKV_COLS + MLA_COLS:]
    gates = wm[:, MLSTM_QKW + 2 * MLSTM_DIM:]
    gate_blk = jnp.pad(gates, ((0, 0), (0, LANES - gates.shape[1])))
    kr = wb[:, MLA_Q_RANK + MLA_KV_RANK:]
    kr_blk = jnp.pad(kr, ((0, 0), (MLA_NOPE, LANES - MLA_NOPE - MLA_ROPE)))
    out = jnp.concatenate([wa, wm[:, :MLSTM_QKW + 2 * MLSTM_DIM], gate_blk,
                           wb[:, :MLA_Q_RANK + MLA_KV_RANK], kr_blk], axis=1)
    assert out.shape == (d, IN_COLS_PAD)
    return out.astype(BF16)


def _arrange_mla_weights(w_uq, w_ukv):
    rq = w_uq.shape[0]
    wq = w_uq.reshape(rq, MLA_HEADS, MLA_QK)
    wq = jnp.pad(wq, ((0, 0), (0, 0), (0, QPAD - MLA_QK)))
    half = MLA_ROPE // 2
    wq_swap = jnp.concatenate(
        [jnp.zeros_like(wq[:, :, :MLA_NOPE]), wq[:, :, MLA_NOPE + half:MLA_QK],
         wq[:, :, MLA_NOPE:MLA_NOPE + half], jnp.zeros_like(wq[:, :, MLA_QK:])], axis=2)
    wq = jnp.concatenate([wq.reshape(rq, -1), wq_swap.reshape(rq, -1)], axis=1)
    rk = w_ukv.shape[0]
    wkv = w_ukv.reshape(rk, MLA_HEADS, MLA_NOPE + MLA_V)
    wk = jnp.pad(wkv[:, :, :MLA_NOPE], ((0, 0), (0, 0), (0, QPAD - MLA_NOPE)))
    wk = wk.reshape(rk, MLA_HEADS * QPAD)
    wv_t = wkv[:, :, MLA_NOPE:].reshape(rk, MLA_DIM).T
    return wq.astype(BF16), wk.astype(BF16), wv_t.astype(BF16)


def _rope_tables(seq_len, ctx_len):
    rows = seq_len // GRID_W
    row = jnp.repeat(jnp.arange(rows, dtype=F32), GRID_W)
    col = jnp.tile(jnp.arange(GRID_W, dtype=F32), rows)
    n_freq = MLA_ROPE // 4
    inv = jnp.power(ROPE_BASE, -jnp.arange(n_freq, dtype=F32) / n_freq)
    ang = jnp.concatenate([row[:, None] * inv, col[:, None] * inv], axis=-1)
    cos, sin = jnp.cos(ang), jnp.sin(ang)
    pad_l, pad_r = MLA_NOPE, QPAD - MLA_QK
    cos_f = jnp.concatenate([jnp.ones((seq_len, pad_l), F32), cos, cos,
                             jnp.ones((seq_len, pad_r), F32)], axis=1)
    sin_f = jnp.concatenate([jnp.zeros((seq_len, pad_l), F32), sin, sin,
                             jnp.zeros((seq_len, pad_r), F32)], axis=1)
    cos_f = jnp.concatenate([jnp.ones((ctx_len, QPAD), F32), cos_f], axis=0)
    sin_f = jnp.concatenate([jnp.zeros((ctx_len, QPAD), F32), sin_f], axis=0)
    return cos_f, sin_f


def _gained_tables(cos_f, sin_f, gain, scale):
    g = jnp.pad(gain, (0, QPAD - gain.shape[0]))
    half = MLA_ROPE // 2
    lo, mid, hi = MLA_NOPE, MLA_NOPE + half, MLA_QK
    g_swap = jnp.concatenate([g[:lo], g[mid:hi], g[lo:mid], g[hi:]])
    sign = jnp.where(jnp.arange(QPAD) < mid, -1.0, 1.0).astype(F32)
    return cos_f * (g * scale), sin_f * (g_swap * sign * scale)


def _gate_bias(i_b, f_b):
    gb = jnp.stack([i_b, f_b], axis=1).reshape(-1)
    return jnp.pad(gb, (0, LANES - gb.shape[0])).reshape(1, LANES)


def kernel(x, c, ctx, c_ctx, mod_w, mod_b, norm1_g, norm2_g, w_in, w_out, ffn_w_in, ffn_w_out, rwkv_mu, rwkv_w0, rwkv_w2, rwkv_a0, rwkv_a2, rwkv_g2, rwkv_k_k, rwkv_k_a, rwkv_r_k, rwkv_ln_g, rwkv_ln_b, mla_q_norm_g, mla_w_uq, mla_kv_norm_g, mla_w_ukv, mla_q_qknorm_g, mla_k_qknorm_g, mlstm_conv_w, mlstm_conv_b, mlstm_i_b, mlstm_f_b, mlstm_norm_g):
    bsz, seq, d = x.shape
    ctx_len = ctx.shape[1]
    depth = mod_w.shape[0]
    tm = min(256, ctx_len)
    assert ctx_len % tm == 0 and seq % tm == 0 and ctx_len % CHUNK == 0 and seq % CHUNK == 0
    nb = ROWS_PER_STEP if bsz % ROWS_PER_STEP == 0 else 1
    nb_seq = SEQ_ROWS_PER_STEP if bsz % SEQ_ROWS_PER_STEP == 0 else nb

    xu = jnp.concatenate([ctx, x], axis=1)
    n_ctx_tiles = ctx_len // tm

    rows = -(-(bsz + 1) // SUBLANES) * SUBLANES
    cvec = jnp.zeros((rows, d), F32).at[:bsz].set(c).at[bsz].set(c_ctx)
    mod = _modulation(cvec, mod_w, mod_b).reshape(depth, rows, 6, d)
    mod_lat = mod[:, :bsz]
    mod_ctx = jnp.broadcast_to(mod[:, bsz][:, None], mod_lat.shape)
    modsel = jnp.stack([mod_ctx, mod_lat], axis=2)

    cos_f, sin_f = _rope_tables(seq, ctx_len)

    for i in range(depth):
        last = i == depth - 1
        first_tile = n_ctx_tiles if last else 0
        w_r = _arrange_w_in(w_in[i])
        za, zm, zg, zb = _in_proj(xu, modsel[i, :, :, 0:2], norm1_g[i], w_r, rwkv_mu[i],
                              mlstm_conv_w[i], mlstm_conv_b[i], tm=tm, ctx_len=ctx_len)
        g_loc, y_loc, a_tr, b_st, bonus, gate = _rwkv_local(
            za, rwkv_k_k[i], rwkv_k_a[i], rwkv_r_k[i], rwkv_w0[i], rwkv_w2[i], rwkv_a0[i],
            rwkv_a2[i], rwkv_g2[i], nb=nb)
        yf, yr = _rwkv_seq(g_loc, y_loc, a_tr, b_st, ctx_len=ctx_len, nb=nb_seq)
        wq_r, wk_r, wv_r = _arrange_mla_weights(mla_w_uq[i], mla_w_ukv[i])
        tables = (_gained_tables(cos_f, sin_f, mla_q_qknorm_g[i], MLA_QK ** -0.5 * LOG2_E)
                  + _gained_tables(cos_f, sin_f, mla_k_qknorm_g[i], 1.0))
        q, k, v = _mla_proj(zb, tables, mla_q_norm_g[i], mla_kv_norm_g[i],
                            wq_r, wk_r, wv_r, tm=tm)
        attn = _mla_attn(q, k, v, tq=tm, ctx_len=ctx_len, first_tile=first_tile)
        gb = _gate_bias(mlstm_i_b[i], mlstm_f_b[i])
        hf, hr = _mlstm_seq(zm, *_mlstm_local(zm, zg, gb, nb=nb), ctx_len=ctx_len, nb=nb_seq)
        wo = w_out[i].astype(BF16)
        xu = _out_proj(xu, yf, yr, bonus, gate, attn, hf, hr, zm, modsel[i, :, :, 2:3],
                       rwkv_ln_g[i], rwkv_ln_b[i], mlstm_norm_g[i],
                       wo[:RWKV_DIM], wo[RWKV_DIM:RWKV_DIM + MLA_DIM], wo[RWKV_DIM + MLA_DIM:],
                       tm=tm, ctx_len=ctx_len, first_tile=first_tile)
        xu = _ffn(xu, modsel[i, :, :, 3:6], norm2_g[i], ffn_w_in[i].astype(BF16),
                  ffn_w_out[i].astype(BF16), tm=tm,
                  n_ctx_tiles_here=0 if last else n_ctx_tiles)
    return xu
```

```python
import functools

import jax
import jax.numpy as jnp
from jax import lax
from jax.experimental import pallas as pl
from jax.experimental.pallas import tpu as pltpu

F32 = jnp.float32
BF16 = jnp.bfloat16

HEAD_DIM = 64
NORM_EPS = 1e-6
GRID_W = 64
ROPE_BASE = 10000.0
LOG2_E = 1.4426950408889634
RWKV_HEADS = 4
RWKV_DIM = RWKV_HEADS * HEAD_DIM
RWKV_W_LORA = 64
RWKV_A_LORA = 64
RWKV_G_LORA = 128
RWKV_COLS = 3 * RWKV_DIM + 2 * RWKV_W_LORA + 2 * RWKV_A_LORA + RWKV_G_LORA
RWKV_GN_EPS = 64e-5
MLA_HEADS = 8
MLA_NOPE = 64
MLA_ROPE = 32
MLA_V = HEAD_DIM
MLA_QK = MLA_NOPE + MLA_ROPE
MLA_Q_RANK = 512
MLA_KV_RANK = 256
MLA_DIM = MLA_HEADS * MLA_V
MLA_COLS = MLA_Q_RANK + MLA_KV_RANK + MLA_ROPE
MLSTM_HEADS = 4
MLSTM_QK = 32
MLSTM_V = HEAD_DIM
MLSTM_DIM = MLSTM_HEADS * MLSTM_V
MLSTM_QKW = 2 * MLSTM_HEADS * MLSTM_QK
GATE_SOFTCAP = 15.0
MLSTM_COLS = MLSTM_QKW + 2 * MLSTM_DIM + 4 * MLSTM_HEADS

LANES = 128
SUBLANES = 8
CHUNK = HEAD_DIM
ROWS_PER_STEP = 8
SEQ_ROWS_PER_STEP = 8
KV_BLOCK = 256
SCORE_LOOKAHEAD = 2
QPAD = LANES
ZB_COLS = MLA_Q_RANK + MLA_KV_RANK + LANES
ZM_COLS = MLSTM_QKW + 2 * MLSTM_DIM
SHIFT_COLS = RWKV_COLS + MLSTM_QKW
IN_COLS_PAD = RWKV_COLS + ZM_COLS + LANES + ZB_COLS
ACT = BF16
VMEM_LIMIT = 56 * 1024 * 1024


def _split(a):
    hi = a.astype(BF16)
    lo = (a - hi.astype(F32)).astype(BF16)
    return hi, lo


_NN = (((1,), (0,)), ((), ()))
_NT = (((1,), (1,)), ((), ()))
_TN = (((0,), (0,)), ((), ()))


def _dg(a, b, dims):
    return lax.dot_general(a, b, dims, preferred_element_type=F32)


def _dot1(a, b, dims=_NN):
    return _dg(a.astype(BF16), b.astype(BF16), dims)


def _dot3(a, b, dims=_NN):
    ah, al = _split(a)
    bh, bl = _split(b)
    return _dg(ah, bh, dims) + (_dg(ah, bl, dims) + _dg(al, bh, dims))


def _dot_exact_rhs(a, b01, dims=_NN):
    a1 = a.astype(BF16)
    r1 = a - a1.astype(F32)
    a2 = r1.astype(BF16)
    a3 = (r1 - a2.astype(F32)).astype(BF16)
    return _dg(a1, b01, dims) + (_dg(a2, b01, dims) + _dg(a3, b01, dims))


def _dot_exact_lhs(a01, b, dims=_NN):
    b1 = b.astype(BF16)
    r1 = b - b1.astype(F32)
    b2 = r1.astype(BF16)
    b3 = (r1 - b2.astype(F32)).astype(BF16)
    return _dg(a01, b1, dims) + (_dg(a01, b2, dims) + _dg(a01, b3, dims))


def _dot2_rhs01(a, b01):
    ah, al = _split(a)
    return _dg(ah, b01, _NN) + _dg(al, b01, _NN)


def _iota(shape, dim):
    return lax.broadcasted_iota(jnp.int32, shape, dim)


def _seg_ones(n, seg):
    r = _iota((n, n), 0) // seg
    c = _iota((n, n), 1) // seg
    return (r == c).astype(BF16)


def _segsum(x, ones_bd):
    return _dot2_rhs01(x, ones_bd)


def _softplus(x):
    return jnp.maximum(x, 0.0) + jnp.log(1.0 + jnp.exp(-jnp.abs(x)))


def _sigmoid(x):
    return 1.0 / (1.0 + jnp.exp(-x))


def _silu(x):
    return x * _sigmoid(x)


def _rmsnorm_mod(x, g, shift, scale):
    y = x * lax.rsqrt(jnp.mean(x * x, axis=-1, keepdims=True) + NORM_EPS)
    return (y * g) * (1.0 + scale) + shift


def _chunk_index(i, n_ctx_chunks, n_chunks, rev):
    if not rev:
        return i
    return jnp.where(i < n_ctx_chunks, n_ctx_chunks - 1 - i,
                     n_chunks - 1 - (i - n_ctx_chunks))


def _incl_mask(n, rev):
    r = _iota((n, n), 0)
    c = _iota((n, n), 1)
    return (c >= r) if rev else (c <= r)


def _mod_kernel(c_ref, w_ref, b_ref, o_ref):
    o_ref[0] = _dot3(_silu(c_ref[...]), w_ref[0]) + b_ref[0]


def _modulation(cvec, mod_w, mod_b):
    depth, d, n = mod_w.shape
    tn = 1536
    rows = cvec.shape[0]
    return pl.pallas_call(
        _mod_kernel,
        grid=(depth, n // tn),
        in_specs=[
            pl.BlockSpec((rows, d), lambda l, j: (0, 0)),
            pl.BlockSpec((1, d, tn), lambda l, j: (l, 0, j)),
            pl.BlockSpec((1, 1, tn), lambda l, j: (l, 0, j)),
        ],
        out_specs=pl.BlockSpec((1, rows, tn), lambda l, j: (l, 0, j)),
        out_shape=jax.ShapeDtypeStruct((depth, rows, n), F32),
        compiler_params=pltpu.CompilerParams(
            dimension_semantics=("arbitrary", "arbitrary"), vmem_limit_bytes=VMEM_LIMIT),
        name="adaln_mod",
    )(cvec, mod_w, mod_b.reshape(depth, 1, n))


def _inproj_kernel(x_ref, xp_ref, xn_ref, mod_ref, g_ref, w_ref, mu_ref, cw_ref, cb_ref,
                   za_ref, zm_ref, zg_ref, zb_ref, *, tm, n_ctx_tiles, n_tiles):
    j = pl.program_id(1)
    g = g_ref[...]
    shift = mod_ref[0, 0, 0:1, :]
    scale = mod_ref[0, 0, 1:2, :]
    h = _rmsnorm_mod(x_ref[0], g, shift, scale).astype(BF16)
    z = _dg(h, w_ref[...], _NN)
    halo = jnp.concatenate([xp_ref[0], xn_ref[0]], axis=0)
    hh = _rmsnorm_mod(halo, g, shift, scale).astype(BF16)
    zh = _dg(hh, w_ref[:, :SHIFT_COLS], _NN)
    prev_ok = jnp.logical_and(j != 0, j != n_ctx_tiles).astype(F32)
    next_ok = jnp.logical_and(j != n_ctx_tiles - 1, j != n_tiles - 1).astype(F32)
    zs = z[:, :SHIFT_COLS]
    row = _iota((tm, SHIFT_COLS), 0)
    prev = jnp.where(row == 0, zh[SUBLANES - 1:SUBLANES, :] * prev_ok, pltpu.roll(zs, 1, 0))
    nxt = jnp.where(row == tm - 1, zh[SUBLANES:SUBLANES + 1, :] * next_ok,
                    pltpu.roll(zs, tm - 1, 0))
    za = zs[:, :RWKV_COLS]
    shifted = za + mu_ref[...] * (0.5 * (prev[:, :RWKV_COLS] + nxt[:, :RWKV_COLS]) - za)
    za_ref[0] = shifted.astype(za_ref.dtype)
    qk = (prev[:, RWKV_COLS:] * cw_ref[0:1, :] + zs[:, RWKV_COLS:] * cw_ref[1:2, :]
          + nxt[:, RWKV_COLS:] * cw_ref[2:3, :] + cb_ref[...])
    zm_ref[0, :, :MLSTM_QKW] = _silu(qk).astype(zm_ref.dtype)
    zm_ref[0, :, MLSTM_QKW:] = z[:, SHIFT_COLS:RWKV_COLS + ZM_COLS].astype(zm_ref.dtype)
    gate_end = RWKV_COLS + ZM_COLS + LANES
    zg_ref[0] = z[:, RWKV_COLS + ZM_COLS:gate_end]
    zb_ref[0] = z[:, gate_end:].astype(zb_ref.dtype)


def _in_proj(x, modsel, norm_g, w_r, mu, conv_w, conv_b, *, tm, ctx_len):
    b, l, d = x.shape
    n_tiles = l // tm
    n_ctx_tiles = ctx_len // tm
    tm8 = tm // SUBLANES
    kern = functools.partial(_inproj_kernel, tm=tm, n_ctx_tiles=n_ctx_tiles, n_tiles=n_tiles)
    seg = lambda j: jnp.where(j >= n_ctx_tiles, 1, 0)
    return pl.pallas_call(
        kern,
        grid=(b, n_tiles),
        in_specs=[
            pl.BlockSpec((1, tm, d), lambda i, j: (i, j, 0)),
            pl.BlockSpec((1, SUBLANES, d), lambda i, j: (i, jnp.maximum(j * tm8 - 1, 0), 0)),
            pl.BlockSpec((1, SUBLANES, d),
                         lambda i, j: (i, jnp.minimum((j + 1) * tm8, l // SUBLANES - 1), 0)),
            pl.BlockSpec((1, 1, 2, d), lambda i, j: (i, seg(j), 0, 0)),
            pl.BlockSpec((1, d), lambda i, j: (0, 0)),
            pl.BlockSpec((d, IN_COLS_PAD), lambda i, j: (0, 0)),
            pl.BlockSpec((1, RWKV_COLS), lambda i, j: (0, 0)),
            pl.BlockSpec((3, MLSTM_QKW), lambda i, j: (0, 0)),
            pl.BlockSpec((1, MLSTM_QKW), lambda i, j: (0, 0)),
        ],
        out_specs=[
            pl.BlockSpec((1, tm, RWKV_COLS), lambda i, j: (i, j, 0)),
            pl.BlockSpec((1, tm, ZM_COLS), lambda i, j: (i, j, 0)),
            pl.BlockSpec((1, tm, LANES), lambda i, j: (i, j, 0)),
            pl.BlockSpec((1, tm, ZB_COLS), lambda i, j: (i, j, 0)),
        ],
        out_shape=[
            jax.ShapeDtypeStruct((b, l, RWKV_COLS), ACT),
            jax.ShapeDtypeStruct((b, l, ZM_COLS), ACT),
            jax.ShapeDtypeStruct((b, l, LANES), F32),
            jax.ShapeDtypeStruct((b, l, ZB_COLS), ACT),
        ],
        compiler_params=pltpu.CompilerParams(
            dimension_semantics=("arbitrary", "arbitrary"), vmem_limit_bytes=VMEM_LIMIT),
        name="in_proj",
    )(x, x, x, modsel, norm_g.reshape(1, d), w_r, mu.reshape(1, RWKV_COLS), conv_w,
      conv_b.reshape(1, MLSTM_QKW))


def _rwkv_local_kernel(za_ref, kk_ref, ka_ref, rk_ref, w0_ref, w2_ref, a0_ref, a2_ref, g2_ref,
                       g_ref, yl_ref, at_ref, bs_ref, bonus_ref, gate_ref):
    c = CHUNK
    nd = RWKV_DIM
    hd = HEAD_DIM
    nh = RWKV_HEADS
    nb = za_ref.shape[0]
    ones_bd = _seg_ones(nd, hd)
    eye4 = _iota((hd, nd), 0) == (_iota((hd, nd), 1) % hd)
    eye4_f = eye4.astype(F32)
    row2 = _iota((2 * c, 2 * nd), 0)
    rr = row2 % c
    cc = _iota((2 * c, 2 * nd), 1) % c
    diag_ok = jnp.logical_and(cc == rr, row2 >= c)
    keep = _diag_mask(nd, nd, hd, hd)
    bd = lambda x: _block_diag(x, nh, hd, hd, keep)
    units, pr, rhs, vb, pm, rm, qe, ke, gam, gmask = ([] for _ in range(10))
    for rb in range(nb):
        za = za_ref[rb].astype(F32)
        r = za[:, 0:nd]
        k = za[:, nd:2 * nd]
        v = za[:, 2 * nd:3 * nd]
        kkr = k * kk_ref[...]
        kk = kkr / jnp.maximum(jnp.sqrt(_segsum(kkr * kkr, ones_bd)), 1e-12)
        bonus_ref[rb] = (_segsum(r * k * rk_ref[...], ones_bd) * v).astype(bonus_ref.dtype)
        gd = za[:, 3 * nd + 2 * RWKV_W_LORA + 2 * RWKV_A_LORA:]
        gate_ref[rb] = _dot1(_sigmoid(gd), g2_ref[...]).astype(gate_ref.dtype)
        v_b = v.astype(BF16)
        for d in range(2):
            rev = d == 1
            wo = 3 * nd + d * RWKV_W_LORA
            ao = 3 * nd + 2 * RWKV_W_LORA + d * RWKV_A_LORA
            w_lo = _dot1(jnp.tanh(za[:, wo:wo + RWKV_W_LORA]), w2_ref[d])
            log_w = -_softplus(-(w0_ref[d] + w_lo)) - 0.5
            logdec = -jnp.exp(log_w)
            a = _sigmoid(a0_ref[d] + _dot1(za[:, ao:ao + RWKV_A_LORA], a2_ref[d]))
            kd = k * (1.0 + (a - 1.0) * ka_ref[...])
            bvec = kk * a
            cum = _dot_exact_lhs(_incl_mask(c, rev).astype(BF16), logdec)
            total = cum[0:1, :] if rev else cum[c - 1:c, :]
            e_neg = jnp.exp(-cum)
            e_end = jnp.exp(total - cum)
            pm_d = kk * jnp.exp(cum - logdec)
            rm_d = r * jnp.exp(cum)
            units.append((rb, d))
            pm.append(pm_d)
            rm.append(rm_d)
            pr.append(jnp.concatenate([pm_d, rm_d], axis=0).astype(BF16))
            rhs.append(jnp.concatenate([bd((bvec * e_neg).astype(BF16)),
                                        bd((kd * e_neg).astype(BF16))], axis=0))
            vb.append(v_b)
            qe.append((bvec * e_end).astype(BF16))
            ke.append((kd * e_end).astype(BF16))
            gam.append(jnp.exp(total))
            gmask.append(jnp.logical_or(cc > rr if rev else cc < rr, diag_ok))
    n = len(units)
    a4 = [jnp.where(gmask[i], _dg(pr[i], rhs[i], _NT), 0.0) for i in range(n)]
    a4b = [x.astype(BF16) for x in a4]
    pw = [_dg(a4b[i][:c, :nd], bd(a4b[i][:c, :nd]), _NN) for i in range(n)]
    lv = [_dg(a4b[i][:, nd:], bd(vb[i]), _NN) for i in range(n)]
    tinv = [eye4_f - a4[i][:c, :nd] for i in range(n)]
    covered = 2
    while True:
        pwd = [bd(x.astype(BF16)) for x in pw]
        tinv = [tinv[i] + _dg(tinv[i].astype(BF16), pwd[i], _NN) for i in range(n)]
        covered *= 2
        if covered >= c:
            break
        pw = [_dg(pw[i].astype(BF16), pwd[i], _NN) for i in range(n)]
    wz = [_dg(tinv[i].astype(BF16),
              jnp.concatenate([bd(pm[i].astype(BF16)), bd(lv[i][:c].astype(BF16))], axis=1),
              _NN).astype(BF16) for i in range(n)]
    awz = [_dg(a4b[i][c:, :nd],
               jnp.concatenate([bd(wz[i][:, :nd]), bd(wz[i][:, nd:])], axis=1), _NN)
           for i in range(n)]
    eye = _iota((hd, hd), 0) == _iota((hd, hd), 1)
    for i, (rb, d) in enumerate(units):
        g_ref[d, rb] = (rm[i] - awz[i][:, :nd]).astype(g_ref.dtype)
        yl_ref[d, rb] = (lv[i][c:] - awz[i][:, nd:]).astype(yl_ref.dtype)
        ats, bss = [], []
        for h in range(nh):
            sl = slice(h * hd, (h + 1) * hd)
            wz_h = jnp.concatenate([wz[i][:, sl], wz[i][:, nd + h * hd:nd + (h + 1) * hd]], axis=1)
            qwz = _dg(qe[i][:, sl], wz_h, _TN)
            ats.append(jnp.where(eye, jnp.broadcast_to(gam[i][:, sl], (hd, hd)), 0.0)
                       - qwz[:, :hd])
            bss.append(_dg(ke[i][:, sl], vb[i][:, sl], _TN) - qwz[:, hd:])
        at_ref[d, rb] = jnp.concatenate(ats, axis=1).astype(at_ref.dtype)
        bs_ref[d, rb] = jnp.concatenate(bss, axis=1).astype(bs_ref.dtype)


def _rwkv_local(za, k_k, k_a, r_k, w0, w2, a0, a2, g2, *, nb):
    b, l, _ = za.shape
    c = CHUNK
    nd = RWKV_DIM
    full = lambda shape: pl.BlockSpec(shape, lambda i, j: (0,) * len(shape))
    dir_spec = pl.BlockSpec((2, nb, c, nd), lambda i, j: (0, i, j, 0))
    dir_shape = jax.ShapeDtypeStruct((2, b, l, nd), ACT)
    row_spec = pl.BlockSpec((nb, c, nd), lambda i, j: (i, j, 0))
    row_shape = jax.ShapeDtypeStruct((b, l, nd), ACT)
    return pl.pallas_call(
        _rwkv_local_kernel,
        grid=(b // nb, l // c),
        in_specs=[
            pl.BlockSpec((nb, c, RWKV_COLS), lambda i, j: (i, j, 0)),
            full((1, nd)), full((1, nd)), full((1, nd)),
            full((2, 1, nd)), full((2, RWKV_W_LORA, nd)),
            full((2, 1, nd)), full((2, RWKV_A_LORA, nd)),
            full((RWKV_G_LORA, nd)),
        ],
        out_specs=[dir_spec] * 4 + [row_spec] * 2,
        out_shape=[dir_shape] * 4 + [row_shape] * 2,
        compiler_params=pltpu.CompilerParams(
            dimension_semantics=("arbitrary", "arbitrary"), vmem_limit_bytes=VMEM_LIMIT),
        name="rwkv_local",
    )(za, k_k.reshape(1, -1), k_a.reshape(1, -1), r_k.reshape(1, -1), w0.reshape(2, 1, nd),
      w2, a0.reshape(2, 1, nd), a2, g2)


def _rwkv_seq_kernel(gf_ref, ylf_ref, atf_ref, bsf_ref, gr_ref, ylr_ref, atr_ref, bsr_ref,
                     yf_ref, yr_ref, m_ref):
    nb = yf_ref.shape[0]

    @pl.when(pl.program_id(1) == 0)
    def _():
        m_ref[...] = jnp.zeros_like(m_ref)

    dirs = ((gf_ref, ylf_ref, atf_ref, bsf_ref, yf_ref), (gr_ref, ylr_ref, atr_ref, bsr_ref, yr_ref))
    units = [(rb, d) for rb in range(nb) for d in range(2)]
    keep = _diag_mask(RWKV_DIM, RWKV_DIM, HEAD_DIM, HEAD_DIM)
    m_bd = [_block_diag(m_ref[u].astype(BF16), RWKV_HEADS, HEAD_DIM, HEAD_DIM, keep)
            for u in range(len(units))]
    y = [_dg(dirs[d][0][0, rb], m_bd[u], _NN) for u, (rb, d) in enumerate(units)]
    m_new = [_dg(dirs[d][2][0, rb], m_bd[u], _NN) for u, (rb, d) in enumerate(units)]
    for u, (rb, d) in enumerate(units):
        dirs[d][4][rb] = (y[u] + dirs[d][1][0, rb].astype(F32)).astype(dirs[d][4].dtype)
        m_ref[u] = m_new[u] + dirs[d][3][0, rb].astype(F32)


def _rwkv_seq(g, yl, at, bs, *, ctx_len, nb):
    _, b, l, nd = g.shape
    c = CHUNK
    n_chunks = l // c
    n_ctx_chunks = ctx_len // c
    rev_idx = lambda j: _chunk_index(j, n_ctx_chunks, n_chunks, True)
    fwd = pl.BlockSpec((1, nb, c, nd), lambda i, j: (0, i, j, 0))
    bwd = pl.BlockSpec((1, nb, c, nd), lambda i, j: (1, i, rev_idx(j), 0))
    y_shape = jax.ShapeDtypeStruct((b, l, nd), ACT)
    return pl.pallas_call(
        _rwkv_seq_kernel,
        grid=(b // nb, n_chunks),
        in_specs=[fwd] * 4 + [bwd] * 4,
        out_specs=[pl.BlockSpec((nb, c, nd), lambda i, j: (i, j, 0)),
                   pl.BlockSpec((nb, c, nd), lambda i, j: (i, rev_idx(j), 0))],
        out_shape=[y_shape, y_shape],
        scratch_shapes=[pltpu.VMEM((nb * 2, HEAD_DIM, RWKV_DIM), F32)],
        compiler_params=pltpu.CompilerParams(
            dimension_semantics=("arbitrary", "arbitrary"), vmem_limit_bytes=VMEM_LIMIT),
        name="rwkv_seq",
    )(g, yl, at, bs, g, yl, at, bs)


def _cummax_rows(x, rev):
    n = x.shape[0]
    row = _iota(x.shape, 0)
    shift = 1
    while shift < n:
        if rev:
            moved = jnp.where(row >= n - shift, -jnp.inf, pltpu.roll(x, n - shift, 0))
        else:
            moved = jnp.where(row < shift, -jnp.inf, pltpu.roll(x, shift, 0))
        x = jnp.maximum(x, moved)
        shift *= 2
    return x


def _head_lane(d):
    return d * 2 * MLSTM_HEADS + MLSTM_HEADS


def _sel_expand(d, width):
    n = MLSTM_HEADS * width
    return (_iota((LANES, n), 0) == _head_lane(d) + _iota((LANES, n), 1) // width).astype(BF16)


def _sel_reduce(d, width):
    n = MLSTM_HEADS * width
    return (_iota((n, LANES), 1) == _head_lane(d) + _iota((n, LANES), 0) // width).astype(BF16)


def _diag_mask(rows, cols, row_block, col_block):
    return (_iota((rows, cols), 0) // row_block) == (_iota((rows, cols), 1) // col_block)


def _block_diag(x, reps, row_block, col_block, keep=None):
    t = jnp.concatenate([x] * reps, axis=0)
    if keep is None:
        keep = _diag_mask(t.shape[0], t.shape[1], row_block, col_block)
    return jnp.where(keep, t, jnp.zeros_like(t))


def _mlstm_local_kernel(zm_ref, zg_ref, gb_ref, num_ref, den_ref, mi_ref, b_ref, cl_ref,
                        rows_ref):
    c = CHUNK
    nh = MLSTM_HEADS
    dk = MLSTM_QK
    dv = MLSTM_V
    nb = zm_ref.shape[0]
    lane_j = _iota((c, nh * c), 1) % c
    row_s = _iota((c, nh * c), 0)
    cl_keep = _diag_mask(nh * dk, MLSTM_DIM, dk, dv)
    k_keep = _diag_mask(nh * c, nh * dk, c, dk)
    v_keep = _diag_mask(nh * c, MLSTM_DIM, c, dv)
    sel_c = [_sel_expand(d, c) for d in range(2)]
    sel_k = [_sel_expand(d, dk) for d in range(2)]
    sel_r = [_sel_reduce(d, c) for d in range(2)]
    k, vb, log_f, li, qk, vbd = [], [], [], [], [], []
    for rb in range(nb):
        zm = zm_ref[rb]
        qb = (zm[:, :nh * dk].astype(F32) * (dk ** -0.5)).astype(BF16)
        k.append(zm[:, nh * dk:MLSTM_QKW].astype(F32))
        vb.append(zm[:, MLSTM_QKW:MLSTM_QKW + MLSTM_DIM].astype(BF16))
        gates = zg_ref[rb] + gb_ref[...]
        capped = GATE_SOFTCAP * jnp.tanh(gates * (1.0 / GATE_SOFTCAP))
        log_f.append(-_softplus(-capped))
        li.append(pltpu.roll(capped, nh, 1))
        qk.append(_dg(qb, _block_diag(k[rb].astype(BF16), nh, c, dk, k_keep), _NT))
        vbd.append(_block_diag(vb[rb], nh, c, dv, v_keep))
    units = [(rb, d) for rb in range(nb) for d in range(2)]
    b_all = [_dot_exact_lhs(_incl_mask(c, d == 1).astype(BF16), log_f[rb]) for rb, d in units]
    x = [li[rb] - b_all[u] for u, (rb, d) in enumerate(units)]
    cm = [_cummax_rows(x[u], d == 1) for u, (rb, d) in enumerate(units)]
    e1 = [_dot_exact_rhs(-cm[u], sel_c[d]) for u, (rb, d) in enumerate(units)]
    xe = [_dot_exact_rhs(x[u], sel_c[d]) for u, (rb, d) in enumerate(units)]
    g_row, m_loc, e32 = [], [], []
    for u, (rb, d) in enumerate(units):
        last = 0 if d == 1 else c - 1
        g_row.append(b_all[u][last:last + 1, :])
        w_end = g_row[u] - b_all[u] + li[rb]
        m_loc.append(jnp.max(w_end, axis=0, keepdims=True))
        e32.append(_dot2_rhs01(jnp.exp(w_end - m_loc[u]), sel_k[d]))
    wi = []
    for u, (rb, d) in enumerate(units):
        x_row = jnp.sum(jnp.where(row_s == lane_j, xe[u], 0.0), axis=0, keepdims=True)
        earlier = (lane_j >= row_s) if d == 1 else (lane_j <= row_s)
        wi.append((qk[rb] * jnp.exp(jnp.where(earlier, e1[u] + x_row, -jnp.inf))).astype(BF16))
    nd = [_dg(wi[u], jnp.concatenate([vbd[rb], sel_r[d]], axis=1), _NN)
          for u, (rb, d) in enumerate(units)]
    ke = [k[rb] * e32[u] for u, (rb, d) in enumerate(units)]
    full = [_dg(ke[u].astype(BF16), vb[rb], _TN) for u, (rb, d) in enumerate(units)]
    for u, (rb, d) in enumerate(units):
        num_ref[d, rb] = nd[u][:, :MLSTM_DIM].astype(num_ref.dtype)
        den_ref[d, rb] = nd[u][:, MLSTM_DIM:]
        mi_ref[d, rb] = b_all[u] + cm[u]
        b_ref[d, rb] = b_all[u]
        f = jnp.where(cl_keep, full[u], 0.0)
        cl_ref[d, rb] = ((f[0:dk] + f[dk:2 * dk])
                         + (f[2 * dk:3 * dk] + f[3 * dk:4 * dk])).astype(cl_ref.dtype)
        rows_ref[d, rb] = jnp.concatenate(
            [jnp.sum(ke[u], axis=0, keepdims=True), m_loc[u], g_row[u],
             jnp.zeros((SUBLANES - 3, LANES), F32)], axis=0)


def _mlstm_local(zm, zg, gate_bias, *, nb):
    b, l, _ = zm.shape
    c = CHUNK
    n_chunks = l // c
    spec = lambda rows, w, dt: pl.BlockSpec((2, nb, rows, w), lambda i, j: (0, i, j, 0))
    shape = lambda rows, w, dt: jax.ShapeDtypeStruct((2, b, n_chunks * rows, w), dt)
    outs = [(c, MLSTM_DIM, ACT), (c, LANES, F32), (c, LANES, F32), (c, LANES, F32),
            (MLSTM_QK, MLSTM_DIM, ACT), (SUBLANES, LANES, F32)]
    return pl.pallas_call(
        _mlstm_local_kernel,
        grid=(b // nb, n_chunks),
        in_specs=[
            pl.BlockSpec((nb, c, ZM_COLS), lambda i, j: (i, j, 0)),
            pl.BlockSpec((nb, c, LANES), lambda i, j: (i, j, 0)),
            pl.BlockSpec((1, LANES), lambda i, j: (0, 0)),
        ],
        out_specs=[spec(*o) for o in outs],
        out_shape=[shape(*o) for o in outs],
        compiler_params=pltpu.CompilerParams(
            dimension_semantics=("arbitrary", "arbitrary"), vmem_limit_bytes=VMEM_LIMIT),
        name="mlstm_local",
    )(zm, zg, gate_bias)


def _mlstm_seq_kernel(qf_ref, numf_ref, denf_ref, mif_ref, bf_ref, clf_ref, rowsf_ref,
                      qr_ref, numr_ref, denr_ref, mir_ref, br_ref, clr_ref, rowsr_ref,
                      hf_ref, hr_ref, cbd_ref, n_ref, m_ref):
    c = CHUNK
    nh = MLSTM_HEADS
    dk = MLSTM_QK
    dv = MLSTM_V
    nb = hf_ref.shape[0]

    @pl.when(pl.program_id(1) == 0)
    def _():
        cbd_ref[...] = jnp.zeros_like(cbd_ref)
        n_ref[...] = jnp.zeros_like(n_ref)
        m_ref[...] = jnp.zeros_like(m_ref)

    dirs = ((qf_ref, numf_ref, denf_ref, mif_ref, bf_ref, clf_ref, rowsf_ref, hf_ref),
            (qr_ref, numr_ref, denr_ref, mir_ref, br_ref, clr_ref, rowsr_ref, hr_ref))
    units = [(rb, d) for rb in range(nb) for d in range(2)]
    nu = len(units)
    cl_keep = _diag_mask(nh * dk, MLSTM_DIM, dk, dv)
    sel_v = [_sel_expand(d, dv) for d in range(2)]
    sel_k = [_sel_expand(d, dk) for d in range(2)]
    sel_r = [_sel_reduce(d, dk) for d in range(2)]
    q = [dirs[d][0][rb].astype(F32) * (dk ** -0.5) for rb, d in units]
    rows = [dirs[d][6][0, rb] for rb, d in units]
    m_row = [m_ref[u] for u in range(nu)]
    n_row = [n_ref[u] for u in range(nu)]
    cbd = [cbd_ref[u] for u in range(nu)]
    qn = [_dot2_rhs01(q[u] * n_row[u], sel_r[d]) for u, (rb, d) in enumerate(units)]
    qc = [_dg(q[u].astype(BF16), cbd[u].astype(BF16), _NN) for u in range(nu)]
    a12, srow = [], []
    for u, (rb, d) in enumerate(units):
        mi = dirs[d][3][0, rb]
        log_inter = dirs[d][4][0, rb] + m_row[u]
        m_out = jnp.maximum(log_inter, mi)
        s_intra = jnp.exp(mi - m_out)
        s_inter = jnp.exp(log_inter - m_out)
        den = s_intra * dirs[d][2][0, rb] + s_inter * qn[u]
        dinv = 1.0 / jnp.maximum(jnp.abs(den), jnp.exp(-m_out))
        a12.append(jnp.concatenate([s_intra * dinv, s_inter * dinv], axis=0))
        m_loc, g_row = rows[u][1:2], rows[u][2:3]
        m_new = jnp.maximum(g_row + m_row[u], m_loc)
        srow.append(jnp.concatenate(
            [jnp.exp(g_row + m_row[u] - m_new), jnp.exp(m_loc - m_new),
             jnp.zeros((SUBLANES - 2, LANES), F32)], axis=0))
        m_ref[u] = m_new
    a12e = [_dot2_rhs01(a12[u], sel_v[d]) for u, (rb, d) in enumerate(units)]
    s_v = [_dot2_rhs01(srow[u], sel_v[d]) for u, (rb, d) in enumerate(units)]
    s_k = [_dot2_rhs01(srow[u], sel_k[d]) for u, (rb, d) in enumerate(units)]
    for u, (rb, d) in enumerate(units):
        h_out = a12e[u][:c] * dirs[d][1][0, rb].astype(F32) + a12e[u][c:] * qc[u]
        dirs[d][7][rb] = h_out.astype(dirs[d][7].dtype)
        cl_full = _block_diag(dirs[d][5][0, rb].astype(F32), nh, dk, dv, cl_keep)
        cbd_ref[u] = s_v[u][0:1] * cbd[u] + s_v[u][1:2] * cl_full
        n_ref[u] = s_k[u][0:1] * n_row[u] + s_k[u][1:2] * rows[u][0:1]


def _mlstm_seq(zm, num, den, mi, bsum, cl, rows, *, ctx_len, nb):
    b, l, _ = zm.shape
    c = CHUNK
    n_chunks = l // c
    n_ctx_chunks = ctx_len // c
    rev_idx = lambda j: _chunk_index(j, n_ctx_chunks, n_chunks, True)

    def specs(d):
        cidx = (lambda j: j) if d == 0 else rev_idx
        blk = lambda r, w: pl.BlockSpec((1, nb, r, w), lambda i, j: (d, i, cidx(j), 0))
        return [pl.BlockSpec((nb, c, MLSTM_HEADS * MLSTM_QK), lambda i, j: (i, cidx(j), 0)),
                blk(c, MLSTM_DIM), blk(c, LANES), blk(c, LANES), blk(c, LANES),
                blk(MLSTM_QK, MLSTM_DIM), blk(SUBLANES, LANES)]

    h_shape = jax.ShapeDtypeStruct((b, l, MLSTM_DIM), ACT)
    args = (zm, num, den, mi, bsum, cl, rows)
    return pl.pallas_call(
        _mlstm_seq_kernel,
        grid=(b // nb, n_chunks),
        in_specs=specs(0) + specs(1),
        out_specs=[pl.BlockSpec((nb, c, MLSTM_DIM), lambda i, j: (i, j, 0)),
                   pl.BlockSpec((nb, c, MLSTM_DIM), lambda i, j: (i, rev_idx(j), 0))],
        out_shape=[h_shape, h_shape],
        scratch_shapes=[
            pltpu.VMEM((2 * nb, MLSTM_HEADS * MLSTM_QK, MLSTM_DIM), F32),
            pltpu.VMEM((2 * nb, 1, LANES), F32),
            pltpu.VMEM((2 * nb, 1, LANES), F32),
        ],
        compiler_params=pltpu.CompilerParams(
            dimension_semantics=("arbitrary", "arbitrary"), vmem_limit_bytes=VMEM_LIMIT),
        name="mlstm_seq",
    )(*args, *args)


def _rope_swap(x):
    lane = _iota(x.shape, 1)
    half = MLA_ROPE // 2
    return jnp.where(lane < MLA_NOPE + half, pltpu.roll(x, LANES - half, 1),
                     pltpu.roll(x, half, 1))


def _mla_proj_kernel(zb_ref, cq_ref, sq_ref, ck_ref, sk_ref, qg_ref, kvg_ref,
                     wq_ref, wk_ref, wv_ref, q_ref, k_ref, v_ref):
    hq = MLA_HEADS * QPAD
    zb = zb_ref[0].astype(F32)
    cq = zb[:, :MLA_Q_RANK]
    ckv = zb[:, MLA_Q_RANK:MLA_Q_RANK + MLA_KV_RANK]
    kr = zb[:, MLA_Q_RANK + MLA_KV_RANK:]
    cqn = cq * lax.rsqrt(jnp.mean(cq * cq, axis=-1, keepdims=True) + NORM_EPS) * qg_ref[...]
    ckn = ckv * lax.rsqrt(jnp.mean(ckv * ckv, axis=-1, keepdims=True) + NORM_EPS) * kvg_ref[...]
    q_all = _dot1(cqn, wq_ref[...])
    k_all = _dot1(ckn, wk_ref[...])
    v_ref[0] = _dg(wv_ref[...], ckn.astype(BF16), _NT).astype(BF16)
    cos_q, sin_q, cos_k = cq_ref[...], sq_ref[...], ck_ref[...]
    kr_rot = _rope_swap(kr) * sk_ref[...]
    inv_dim = 1.0 / MLA_QK
    for h in range(MLA_HEADS):
        sl = slice(h * QPAD, (h + 1) * QPAD)
        qh = q_all[:, sl]
        q_inv = lax.rsqrt(jnp.sum(qh * qh, axis=-1, keepdims=True) * inv_dim + NORM_EPS)
        q_ref[0, :, sl] = ((qh * cos_q + q_all[:, hq + h * QPAD:hq + (h + 1) * QPAD] * sin_q)
                           * q_inv).astype(BF16)
        kh = k_all[:, sl] + kr
        k_inv = lax.rsqrt(jnp.sum(kh * kh, axis=-1, keepdims=True) * inv_dim + NORM_EPS)
        k_ref[0, :, sl] = ((kh * cos_k + kr_rot) * k_inv).astype(BF16)


def _mla_proj(zb, tables, q_norm_g, kv_norm_g, wq_r, wk_r, wv_r, *, tm):
    b, l, _ = zb.shape
    hq = MLA_HEADS * QPAD
    full = lambda shape: pl.BlockSpec(shape, lambda i, j: (0,) * len(shape))
    table = pl.BlockSpec((tm, QPAD), lambda i, j: (j, 0))
    return pl.pallas_call(
        _mla_proj_kernel,
        grid=(b, l // tm),
        in_specs=[
            pl.BlockSpec((1, tm, ZB_COLS), lambda i, j: (i, j, 0)),
            table, table, table, table,
            full((1, MLA_Q_RANK)), full((1, MLA_KV_RANK)),
            full((MLA_Q_RANK, 2 * hq)), full((MLA_KV_RANK, hq)), full((MLA_DIM, MLA_KV_RANK)),
        ],
        out_specs=[
            pl.BlockSpec((1, tm, hq), lambda i, j: (i, j, 0)),
            pl.BlockSpec((1, tm, hq), lambda i, j: (i, j, 0)),
            pl.BlockSpec((1, MLA_DIM, tm), lambda i, j: (i, 0, j)),
        ],
        out_shape=[
            jax.ShapeDtypeStruct((b, l, hq), BF16),
            jax.ShapeDtypeStruct((b, l, hq), BF16),
            jax.ShapeDtypeStruct((b, MLA_DIM, l), BF16),
        ],
        compiler_params=pltpu.CompilerParams(
            dimension_semantics=("arbitrary", "arbitrary"), vmem_limit_bytes=VMEM_LIMIT),
        name="mla_proj",
    )(zb, *tables, q_norm_g.reshape(1, -1), kv_norm_g.reshape(1, -1), wq_r, wk_r, wv_r)


def _attend_tile(q_ref, k_ref, vt_ref, o_ref, n_keys):
    tq = q_ref.shape[1]
    starts = list(range(0, n_keys, KV_BLOCK))
    bounds = [(s0, min(s0 + KV_BLOCK, n_keys)) for s0 in starts]
    n_blocks = len(bounds)
    q = [q_ref[0, :, h * QPAD:(h + 1) * QPAD] for h in range(2)]

    def scores(j):
        lo, hi = bounds[j]
        return [_dg(k_ref[0, lo:hi, h * QPAD:(h + 1) * QPAD], q[h], _NT)
                for h in range(2)]

    m = [jnp.full((1, tq), -jnp.inf, F32) for _ in range(2)]
    acc = [jnp.zeros((MLA_V + 2 * SUBLANES, tq), F32) for _ in range(2)]
    pending = [scores(j) for j in range(min(SCORE_LOOKAHEAD, n_blocks))]
    for j in range(n_blocks):
        if j + SCORE_LOOKAHEAD < n_blocks:
            pending.append(scores(j + SCORE_LOOKAHEAD))
        s = pending.pop(0)
        lo, hi = bounds[j]
        ones_rows = jnp.ones((2 * SUBLANES, hi - lo), BF16)
        for h in range(2):
            vt = jnp.concatenate([vt_ref[0, h * MLA_V:(h + 1) * MLA_V, lo:hi], ones_rows], axis=0)
            m_new = jnp.maximum(m[h], jnp.max(s[h], axis=0, keepdims=True))
            p = jnp.exp2(s[h] - m_new).astype(BF16)
            acc[h] = jnp.exp2(m[h] - m_new) * acc[h] + _dg(vt, p, _NN)
            m[h] = m_new
    out_t = jnp.concatenate([acc[h][:MLA_V] / acc[h][MLA_V:MLA_V + 1] for h in range(2)],
                            axis=0)
    o_ref[0] = out_t.T.astype(o_ref.dtype)


def _mla_attn_kernel(q_ref, k_ref, v_ref, o_ref, *, ctx_len, first_tile, n_ctx_tiles):
    n_all = k_ref.shape[1]
    if first_tile >= n_ctx_tiles:
        _attend_tile(q_ref, k_ref, v_ref, o_ref, n_all)
        return
    is_ctx = (pl.program_id(2) + first_tile) < n_ctx_tiles

    @pl.when(is_ctx)
    def _():
        _attend_tile(q_ref, k_ref, v_ref, o_ref, ctx_len)

    @pl.when(jnp.logical_not(is_ctx))
    def _():
        _attend_tile(q_ref, k_ref, v_ref, o_ref, n_all)


def _mla_attn(q, k, v, *, tq, ctx_len, first_tile):
    b, l, _ = q.shape
    assert l % LANES == 0 and ctx_len % LANES == 0
    n_q = l // tq - first_tile
    pairs = MLA_HEADS // 2
    kern = functools.partial(_mla_attn_kernel, ctx_len=ctx_len, first_tile=first_tile,
                             n_ctx_tiles=ctx_len // tq)
    return pl.pallas_call(
        kern,
        grid=(b, pairs, n_q),
        in_specs=[
            pl.BlockSpec((1, tq, 2 * QPAD), lambda i, p, j: (i, j + first_tile, p)),
            pl.BlockSpec((1, l, 2 * QPAD), lambda i, p, j: (i, 0, p)),
            pl.BlockSpec((1, 2 * MLA_V, l), lambda i, p, j: (i, p, 0)),
        ],
        out_specs=pl.BlockSpec((1, tq, 2 * MLA_V), lambda i, p, j: (i, j + first_tile, p)),
        out_shape=jax.ShapeDtypeStruct((b, l, MLA_DIM), ACT),
        compiler_params=pltpu.CompilerParams(
            dimension_semantics=("arbitrary", "arbitrary", "arbitrary"),
            vmem_limit_bytes=VMEM_LIMIT),
        name="mla_attn",
    )(q, k, v)


def _mix_ffn_kernel(x_ref, yf_ref, yr_ref, bonus_ref, gate_ref, attn_ref, hf_ref, hr_ref,
                    o_ref, mod_ref, lng_ref, lnb_ref, mng_ref, wa_ref, wb_ref, wm_ref,
                    g2_ref, w1_ref, w2_ref, out_ref, *, hidden):
    ones_bd = _seg_ones(RWKV_DIM, HEAD_DIM)
    inv = 1.0 / HEAD_DIM
    y = yf_ref[0].astype(F32) + yr_ref[0].astype(F32)
    mean = _segsum(y, ones_bd) * inv
    yc = y - mean
    var = _segsum(yc * yc, ones_bd) * inv
    yn = yc * lax.rsqrt(var + RWKV_GN_EPS) * lng_ref[...] + lnb_ref[...]
    ya = (yn + bonus_ref[0].astype(F32)) * gate_ref[0].astype(F32)
    hm = hf_ref[0].astype(F32) + hr_ref[0].astype(F32)
    hn = hm * lax.rsqrt(_segsum(hm * hm, ones_bd) * inv + NORM_EPS)
    ym = hn * mng_ref[...] * _sigmoid(o_ref[0].astype(F32))
    mix = (_dot1(ya, wa_ref[...]) + _dot1(attn_ref[0], wb_ref[...]) + _dot1(ym, wm_ref[...]))
    x = x_ref[0] + mod_ref[0, 0, 0:1, :] * mix
    h = _rmsnorm_mod(x, g2_ref[...], mod_ref[0, 0, 1:2, :], mod_ref[0, 0, 2:3, :]).astype(BF16)
    gu = _dg(h, w1_ref[...], _NN)
    act = (_silu(gu[:, :hidden]) * gu[:, hidden:]).astype(BF16)
    out_ref[0] = x + mod_ref[0, 0, 3:4, :] * _dg(act, w2_ref[...], _NN)


def _mix_ffn(x, yf, yr, bonus, gate, attn, hf, hr, zm, modsel, ln_g, ln_b, mnorm_g,
             wa, wb, wm, norm2_g, w1, w2, *, tm, ctx_len, first_tile):
    b, l, d = x.shape
    hidden = w2.shape[0]
    n_ctx_tiles = ctx_len // tm
    n_tiles = l // tm - first_tile
    row = lambda w: pl.BlockSpec((1, tm, w), lambda i, j: (i, j + first_tile, 0))
    full = lambda shape: pl.BlockSpec(shape, lambda i, j: (0,) * len(shape))
    seg = lambda j: jnp.where(j + first_tile >= n_ctx_tiles, 1, 0)
    o_block = (MLSTM_QKW + MLSTM_DIM) // MLSTM_DIM
    return pl.pallas_call(
        functools.partial(_mix_ffn_kernel, hidden=hidden),
        grid=(b, n_tiles),
        in_specs=[
            row(d), row(RWKV_DIM), row(RWKV_DIM), row(RWKV_DIM), row(RWKV_DIM), row(MLA_DIM),
            row(MLSTM_DIM), row(MLSTM_DIM),
            pl.BlockSpec((1, tm, MLSTM_DIM), lambda i, j: (i, j + first_tile, o_block)),
            pl.BlockSpec((1, 1, 4, d), lambda i, j: (i, seg(j), 0, 0)),
            full((1, RWKV_DIM)), full((1, RWKV_DIM)), full((1, MLSTM_DIM)),
            full((RWKV_DIM, d)), full((MLA_DIM, d)), full((MLSTM_DIM, d)),
            full((1, d)), full((d, 2 * hidden)), full((hidden, d)),
        ],
        out_specs=pl.BlockSpec((1, tm, d), lambda i, j: (i, j, 0)),
        out_shape=jax.ShapeDtypeStruct((b, n_tiles * tm, d), F32),
        compiler_params=pltpu.CompilerParams(
            dimension_semantics=("arbitrary", "arbitrary"), vmem_limit_bytes=VMEM_LIMIT),
        name="mix_ffn",
    )(x, yf, yr, bonus, gate, attn, hf, hr, zm, modsel, ln_g.reshape(1, -1),
      ln_b.reshape(1, -1), mnorm_g.reshape(1, -1), wa, wb, wm, norm2_g.reshape(1, d), w1, w2)


def _arrange_w_in(w_in):
    d = w_in.shape[0]
    wa = w_in[:, :RWKV_COLS]
    wb = w_in[:, RWKV_COLS:RWKV_COLS + MLA_COLS]
    wm = w_in[:, RWKV_COLS + MLA_COLS:]
    gates = wm[:, MLSTM_QKW + 2 * MLSTM_DIM:]
    gate_blk = jnp.pad(gates, ((0, 0), (0, LANES - gates.shape[1])))
    kr = wb[:, MLA_Q_RANK + MLA_KV_RANK:]
    kr_blk = jnp.pad(kr, ((0, 0), (MLA_NOPE, LANES - MLA_NOPE - MLA_ROPE)))
    out = jnp.concatenate([wa, wm[:, :MLSTM_QKW + 2 * MLSTM_DIM], gate_blk,
                           wb[:, :MLA_Q_RANK + MLA_KV_RANK], kr_blk], axis=1)
    assert out.shape == (d, IN_COLS_PAD)
    return out.astype(BF16)


def _arrange_mla_weights(w_uq, w_ukv):
    rq = w_uq.shape[0]
    wq = w_uq.reshape(rq, MLA_HEADS, MLA_QK)
    wq = jnp.pad(wq, ((0, 0), (0, 0), (0, QPAD - MLA_QK)))
    half = MLA_ROPE // 2
    wq_swap = jnp.concatenate(
        [jnp.zeros_like(wq[:, :, :MLA_NOPE]), wq[:, :, MLA_NOPE + half:MLA_QK],
         wq[:, :, MLA_NOPE:MLA_NOPE + half], jnp.zeros_like(wq[:, :, MLA_QK:])], axis=2)
    wq = jnp.concatenate([wq.reshape(rq, -1), wq_swap.reshape(rq, -1)], axis=1)
    rk = w_ukv.shape[0]
    wkv = w_ukv.reshape(rk, MLA_HEADS, MLA_NOPE + MLA_V)
    wk = jnp.pad(wkv[:, :, :MLA_NOPE], ((0, 0), (0, 0), (0, QPAD - MLA_NOPE)))
    wk = wk.reshape(rk, MLA_HEADS * QPAD)
    wv_t = wkv[:, :, MLA_NOPE:].reshape(rk, MLA_DIM).T
    return wq.astype(BF16), wk.astype(BF16), wv_t.astype(BF16)


def _rope_tables(seq_len, ctx_len):
    rows = seq_len // GRID_W
    row = jnp.repeat(jnp.arange(rows, dtype=F32), GRID_W)
    col = jnp.tile(jnp.arange(GRID_W, dtype=F32), rows)
    n_freq = MLA_ROPE // 4
    inv = jnp.power(ROPE_BASE, -jnp.arange(n_freq, dtype=F32) / n_freq)
    ang = jnp.concatenate([row[:, None] * inv, col[:, None] * inv], axis=-1)
    cos, sin = jnp.cos(ang), jnp.sin(ang)
    pad_l, pad_r = MLA_NOPE, QPAD - MLA_QK
    cos_f = jnp.concatenate([jnp.ones((seq_len, pad_l), F32), cos, cos,
                             jnp.ones((seq_len, pad_r), F32)], axis=1)
    sin_f = jnp.concatenate([jnp.zeros((seq_len, pad_l), F32), sin, sin,
                             jnp.zeros((seq_len, pad_r), F32)], axis=1)
    cos_f = jnp.concatenate([jnp.ones((ctx_len, QPAD), F32), cos_f], axis=0)
    sin_f = jnp.concatenate([jnp.zeros((ctx_len, QPAD), F32), sin_f], axis=0)
    return cos_f, sin_f


def _gained_tables(cos_f, sin_f, gain, scale):
    g = jnp.pad(gain, (0, QPAD - gain.shape[0]))
    half = MLA_ROPE // 2
    lo, mid, hi = MLA_NOPE, MLA_NOPE + half, MLA_QK
    g_swap = jnp.concatenate([g[:lo], g[mid:hi], g[lo:mid], g[hi:]])
    sign = jnp.where(jnp.arange(QPAD) < mid, -1.0, 1.0).astype(F32)
    return cos_f * (g * scale), sin_f * (g_swap * sign * scale)


def _gate_bias(i_b, f_b):
    gb = jnp.stack([i_b, f_b], axis=1).reshape(-1)
    return jnp.pad(gb, (0, LANES - gb.shape[0])).reshape(1, LANES)


def kernel(x, c, ctx, c_ctx, mod_w, mod_b, norm1_g, norm2_g, w_in, w_out, ffn_w_in, ffn_w_out, rwkv_mu, rwkv_w0, rwkv_w2, rwkv_a0, rwkv_a2, rwkv_g2, rwkv_k_k, rwkv_k_a, rwkv_r_k, rwkv_ln_g, rwkv_ln_b, mla_q_norm_g, mla_w_uq, mla_kv_norm_g, mla_w_ukv, mla_q_qknorm_g, mla_k_qknorm_g, mlstm_conv_w, mlstm_conv_b, mlstm_i_b, mlstm_f_b, mlstm_norm_g):
    bsz, seq, d = x.shape
    ctx_len = ctx.shape[1]
    depth = mod_w.shape[0]
    tm = min(256, ctx_len)
    assert ctx_len % tm == 0 and seq % tm == 0 and ctx_len % CHUNK == 0 and seq % CHUNK == 0
    nb = ROWS_PER_STEP if bsz % ROWS_PER_STEP == 0 else 1
    nb_seq = SEQ_ROWS_PER_STEP if bsz % SEQ_ROWS_PER_STEP == 0 else nb

    xu = jnp.concatenate([ctx, x], axis=1)
    n_ctx_tiles = ctx_len // tm

    rows = -(-(bsz + 1) // SUBLANES) * SUBLANES
    cvec = jnp.zeros((rows, d), F32).at[:bsz].set(c).at[bsz].set(c_ctx)
    mod = _modulation(cvec, mod_w, mod_b).reshape(depth, rows, 6, d)
    mod_lat = mod[:, :bsz]
    mod_ctx = jnp.broadcast_to(mod[:, bsz][:, None], mod_lat.shape)
    modsel = jnp.stack([mod_ctx, mod_lat], axis=2)

    cos_f, sin_f = _rope_tables(seq, ctx_len)

    for i in range(depth):
        last = i == depth - 1
        first_tile = n_ctx_tiles if last else 0
        w_r = _arrange_w_in(w_in[i])
        za, zm, zg, zb = _in_proj(xu, modsel[i, :, :, 0:2], norm1_g[i], w_r, rwkv_mu[i],
                              mlstm_conv_w[i], mlstm_conv_b[i], tm=tm, ctx_len=ctx_len)
        g_loc, y_loc, a_tr, b_st, bonus, gate = _rwkv_local(
            za, rwkv_k_k[i], rwkv_k_a[i], rwkv_r_k[i], rwkv_w0[i], rwkv_w2[i], rwkv_a0[i],
            rwkv_a2[i], rwkv_g2[i], nb=nb)
        yf, yr = _rwkv_seq(g_loc, y_loc, a_tr, b_st, ctx_len=ctx_len, nb=nb_seq)
        wq_r, wk_r, wv_r = _arrange_mla_weights(mla_w_uq[i], mla_w_ukv[i])
        tables = (_gained_tables(cos_f, sin_f, mla_q_qknorm_g[i], MLA_QK ** -0.5 * LOG2_E)
                  + _gained_tables(cos_f, sin_f, mla_k_qknorm_g[i], 1.0))
        q, k, v = _mla_proj(zb, tables, mla_q_norm_g[i], mla_kv_norm_g[i],
                            wq_r, wk_r, wv_r, tm=tm)
        attn = _mla_attn(q, k, v, tq=tm, ctx_len=ctx_len, first_tile=first_tile)
        gb = _gate_bias(mlstm_i_b[i], mlstm_f_b[i])
        hf, hr = _mlstm_seq(zm, *_mlstm_local(zm, zg, gb, nb=nb), ctx_len=ctx_len, nb=nb_seq)
        wo = w_out[i].astype(BF16)
        xu = _mix_ffn(xu, yf, yr, bonus, gate, attn, hf, hr, zm, modsel[i, :, :, 2:6],
                      rwkv_ln_g[i], rwkv_ln_b[i], mlstm_norm_g[i],
                      wo[:RWKV_DIM], wo[RWKV_DIM:RWKV_DIM + MLA_DIM], wo[RWKV_DIM + MLA_DIM:],
                      norm2_g[i], ffn_w_in[i].astype(BF16), ffn_w_out[i].astype(BF16),
                      tm=tm, ctx_len=ctx_len, first_tile=first_tile)
    return xu
```

```python
import functools

import jax
import jax.numpy as jnp
from jax import lax
from jax.experimental import pallas as pl
from jax.experimental.pallas import tpu as pltpu

F32 = jnp.float32
BF16 = jnp.bfloat16

HEAD_DIM = 64
NORM_EPS = 1e-6
GRID_W = 64
ROPE_BASE = 10000.0
LOG2_E = 1.4426950408889634
RWKV_HEADS = 4
RWKV_DIM = RWKV_HEADS * HEAD_DIM
RWKV_W_LORA = 64
RWKV_A_LORA = 64
RWKV_G_LORA = 128
RWKV_COLS = 3 * RWKV_DIM + 2 * RWKV_W_LORA + 2 * RWKV_A_LORA + RWKV_G_LORA
RWKV_GN_EPS = 64e-5
MLA_HEADS = 8
MLA_NOPE = 64
MLA_ROPE = 32
MLA_V = HEAD_DIM
MLA_QK = MLA_NOPE + MLA_ROPE
MLA_Q_RANK = 512
MLA_KV_RANK = 256
MLA_DIM = MLA_HEADS * MLA_V
MLA_COLS = MLA_Q_RANK + MLA_KV_RANK + MLA_ROPE
MLSTM_HEADS = 4
MLSTM_QK = 32
MLSTM_V = HEAD_DIM
MLSTM_DIM = MLSTM_HEADS * MLSTM_V
MLSTM_QKW = 2 * MLSTM_HEADS * MLSTM_QK
GATE_SOFTCAP = 15.0
MLSTM_COLS = MLSTM_QKW + 2 * MLSTM_DIM + 4 * MLSTM_HEADS

LANES = 128
SUBLANES = 8
CHUNK = HEAD_DIM
ROWS_PER_STEP = 8
SEQ_ROWS_PER_STEP = 8
KV_BLOCK = 256
ATTN_HEADS_PER_STEP = 4
SCORE_LOOKAHEAD = 1
QPAD = LANES
ZB_COLS = MLA_Q_RANK + MLA_KV_RANK + LANES
ZM_COLS = MLSTM_QKW + 2 * MLSTM_DIM
SHIFT_COLS = RWKV_COLS + MLSTM_QKW
IN_COLS_PAD = RWKV_COLS + ZM_COLS + LANES + ZB_COLS
ACT = BF16
VMEM_LIMIT = 56 * 1024 * 1024


def _split(a):
    hi = a.astype(BF16)
    lo = (a - hi.astype(F32)).astype(BF16)
    return hi, lo


_NN = (((1,), (0,)), ((), ()))
_NT = (((1,), (1,)), ((), ()))
_TN = (((0,), (0,)), ((), ()))


def _dg(a, b, dims):
    return lax.dot_general(a, b, dims, preferred_element_type=F32)


def _dot1(a, b, dims=_NN):
    return _dg(a.astype(BF16), b.astype(BF16), dims)


def _dot3(a, b, dims=_NN):
    ah, al = _split(a)
    bh, bl = _split(b)
    return _dg(ah, bh, dims) + (_dg(ah, bl, dims) + _dg(al, bh, dims))


def _dot_exact_rhs(a, b01, dims=_NN):
    a1 = a.astype(BF16)
    r1 = a - a1.astype(F32)
    a2 = r1.astype(BF16)
    a3 = (r1 - a2.astype(F32)).astype(BF16)
    return _dg(a1, b01, dims) + (_dg(a2, b01, dims) + _dg(a3, b01, dims))


def _dot_exact_lhs(a01, b, dims=_NN):
    b1 = b.astype(BF16)
    r1 = b - b1.astype(F32)
    b2 = r1.astype(BF16)
    b3 = (r1 - b2.astype(F32)).astype(BF16)
    return _dg(a01, b1, dims) + (_dg(a01, b2, dims) + _dg(a01, b3, dims))


def _dot2_rhs01(a, b01):
    ah, al = _split(a)
    return _dg(ah, b01, _NN) + _dg(al, b01, _NN)


def _iota(shape, dim):
    return lax.broadcasted_iota(jnp.int32, shape, dim)


def _seg_ones(n, seg):
    r = _iota((n, n), 0) // seg
    c = _iota((n, n), 1) // seg
    return (r == c).astype(BF16)


def _segsum(x, ones_bd):
    return _dot2_rhs01(x, ones_bd)


def _softplus(x):
    return jnp.maximum(x, 0.0) + jnp.log(1.0 + jnp.exp(-jnp.abs(x)))


def _sigmoid(x):
    return 1.0 / (1.0 + jnp.exp(-x))


def _silu(x):
    return x * _sigmoid(x)


def _rmsnorm_mod(x, g, shift, scale):
    y = x * lax.rsqrt(jnp.mean(x * x, axis=-1, keepdims=True) + NORM_EPS)
    return (y * g) * (1.0 + scale) + shift


def _chunk_index(i, n_ctx_chunks, n_chunks, rev):
    if not rev:
        return i
    return jnp.where(i < n_ctx_chunks, n_ctx_chunks - 1 - i,
                     n_chunks - 1 - (i - n_ctx_chunks))


def _incl_mask(n, rev):
    r = _iota((n, n), 0)
    c = _iota((n, n), 1)
    return (c >= r) if rev else (c <= r)


def _mod_kernel(c_ref, w_ref, b_ref, o_ref):
    o_ref[0] = _dot3(_silu(c_ref[...]), w_ref[0]) + b_ref[0]


def _modulation(cvec, mod_w, mod_b):
    depth, d, n = mod_w.shape
    tn = 1536
    rows = cvec.shape[0]
    return pl.pallas_call(
        _mod_kernel,
        grid=(depth, n // tn),
        in_specs=[
            pl.BlockSpec((rows, d), lambda l, j: (0, 0)),
            pl.BlockSpec((1, d, tn), lambda l, j: (l, 0, j)),
            pl.BlockSpec((1, 1, tn), lambda l, j: (l, 0, j)),
        ],
        out_specs=pl.BlockSpec((1, rows, tn), lambda l, j: (l, 0, j)),
        out_shape=jax.ShapeDtypeStruct((depth, rows, n), F32),
        compiler_params=pltpu.CompilerParams(
            dimension_semantics=("arbitrary", "arbitrary"), vmem_limit_bytes=VMEM_LIMIT),
        name="adaln_mod",
    )(cvec, mod_w, mod_b.reshape(depth, 1, n))


def _inproj_kernel(x_ref, xp_ref, xn_ref, mod_ref, g_ref, w_ref, mu_ref, cw_ref, cb_ref,
                   za_ref, zm_ref, zg_ref, zb_ref, *, tm, n_ctx_tiles, n_tiles):
    j = pl.program_id(1)
    g = g_ref[...]
    shift = mod_ref[0, 0, 0:1, :]
    scale = mod_ref[0, 0, 1:2, :]
    h = _rmsnorm_mod(x_ref[0], g, shift, scale).astype(BF16)
    z = _dg(h, w_ref[...], _NN)
    halo = jnp.concatenate([xp_ref[0], xn_ref[0]], axis=0)
    hh = _rmsnorm_mod(halo, g, shift, scale).astype(BF16)
    zh = _dg(hh, w_ref[:, :SHIFT_COLS], _NN)
    prev_ok = jnp.logical_and(j != 0, j != n_ctx_tiles).astype(F32)
    next_ok = jnp.logical_and(j != n_ctx_tiles - 1, j != n_tiles - 1).astype(F32)
    zs = z[:, :SHIFT_COLS]
    row = _iota((tm, SHIFT_COLS), 0)
    prev = jnp.where(row == 0, zh[SUBLANES - 1:SUBLANES, :] * prev_ok, pltpu.roll(zs, 1, 0))
    nxt = jnp.where(row == tm - 1, zh[SUBLANES:SUBLANES + 1, :] * next_ok,
                    pltpu.roll(zs, tm - 1, 0))
    za = zs[:, :RWKV_COLS]
    shifted = za + mu_ref[...] * (0.5 * (prev[:, :RWKV_COLS] + nxt[:, :RWKV_COLS]) - za)
    za_ref[0] = shifted.astype(za_ref.dtype)
    qk = (prev[:, RWKV_COLS:] * cw_ref[0:1, :] + zs[:, RWKV_COLS:] * cw_ref[1:2, :]
          + nxt[:, RWKV_COLS:] * cw_ref[2:3, :] + cb_ref[...])
    zm_ref[0, :, :MLSTM_QKW] = _silu(qk).astype(zm_ref.dtype)
    zm_ref[0, :, MLSTM_QKW:] = z[:, SHIFT_COLS:RWKV_COLS + ZM_COLS].astype(zm_ref.dtype)
    gate_end = RWKV_COLS + ZM_COLS + LANES
    zg_ref[0] = z[:, RWKV_COLS + ZM_COLS:gate_end]
    zb_ref[0] = z[:, gate_end:].astype(zb_ref.dtype)


def _in_proj(x, modsel, norm_g, w_r, mu, conv_w, conv_b, *, tm, ctx_len):
    b, l, d = x.shape
    n_tiles = l // tm
    n_ctx_tiles = ctx_len // tm
    tm8 = tm // SUBLANES
    kern = functools.partial(_inproj_kernel, tm=tm, n_ctx_tiles=n_ctx_tiles, n_tiles=n_tiles)
    seg = lambda j: jnp.where(j >= n_ctx_tiles, 1, 0)
    return pl.pallas_call(
        kern,
        grid=(b, n_tiles),
        in_specs=[
            pl.BlockSpec((1, tm, d), lambda i, j: (i, j, 0)),
            pl.BlockSpec((1, SUBLANES, d), lambda i, j: (i, jnp.maximum(j * tm8 - 1, 0), 0)),
            pl.BlockSpec((1, SUBLANES, d),
                         lambda i, j: (i, jnp.minimum((j + 1) * tm8, l // SUBLANES - 1), 0)),
            pl.BlockSpec((1, 1, 2, d), lambda i, j: (i, seg(j), 0, 0)),
            pl.BlockSpec((1, d), lambda i, j: (0, 0)),
            pl.BlockSpec((d, IN_COLS_PAD), lambda i, j: (0, 0)),
            pl.BlockSpec((1, RWKV_COLS), lambda i, j: (0, 0)),
            pl.BlockSpec((3, MLSTM_QKW), lambda i, j: (0, 0)),
            pl.BlockSpec((1, MLSTM_QKW), lambda i, j: (0, 0)),
        ],
        out_specs=[
            pl.BlockSpec((1, tm, RWKV_COLS), lambda i, j: (i, j, 0)),
            pl.BlockSpec((1, tm, ZM_COLS), lambda i, j: (i, j, 0)),
            pl.BlockSpec((1, tm, LANES), lambda i, j: (i, j, 0)),
            pl.BlockSpec((1, tm, ZB_COLS), lambda i, j: (i, j, 0)),
        ],
        out_shape=[
            jax.ShapeDtypeStruct((b, l, RWKV_COLS), ACT),
            jax.ShapeDtypeStruct((b, l, ZM_COLS), ACT),
            jax.ShapeDtypeStruct((b, l, LANES), F32),
            jax.ShapeDtypeStruct((b, l, ZB_COLS), ACT),
        ],
        compiler_params=pltpu.CompilerParams(
            dimension_semantics=("arbitrary", "arbitrary"), vmem_limit_bytes=VMEM_LIMIT),
        name="in_proj",
    )(x, x, x, modsel, norm_g.reshape(1, d), w_r, mu.reshape(1, RWKV_COLS), conv_w,
      conv_b.reshape(1, MLSTM_QKW))


def _rwkv_local_kernel(za_ref, kk_ref, ka_ref, rk_ref, w0_ref, w2_ref, a0_ref, a2_ref, g2_ref,
                       g_ref, yl_ref, at_ref, bs_ref, bonus_ref, gate_ref):
    c = CHUNK
    nd = RWKV_DIM
    hd = HEAD_DIM
    nh = RWKV_HEADS
    nb = za_ref.shape[0]
    ones_bd = _seg_ones(nd, hd)
    eye4 = _iota((hd, nd), 0) == (_iota((hd, nd), 1) % hd)
    eye4_f = eye4.astype(F32)
    row2 = _iota((2 * c, 2 * nd), 0)
    rr = row2 % c
    cc = _iota((2 * c, 2 * nd), 1) % c
    diag_ok = jnp.logical_and(cc == rr, row2 >= c)
    keep = _diag_mask(nd, nd, hd, hd)
    bd = lambda x: _block_diag(x, nh, hd, hd, keep)
    units, pr, rhs, vb, pm, rm, qe, ke, gam, gmask = ([] for _ in range(10))
    for rb in range(nb):
        za = za_ref[rb].astype(F32)
        r = za[:, 0:nd]
        k = za[:, nd:2 * nd]
        v = za[:, 2 * nd:3 * nd]
        kkr = k * kk_ref[...]
        kk = kkr / jnp.maximum(jnp.sqrt(_segsum(kkr * kkr, ones_bd)), 1e-12)
        bonus_ref[rb] = (_segsum(r * k * rk_ref[...], ones_bd) * v).astype(bonus_ref.dtype)
        gd = za[:, 3 * nd + 2 * RWKV_W_LORA + 2 * RWKV_A_LORA:]
        gate_ref[rb] = _dot1(_sigmoid(gd), g2_ref[...]).astype(gate_ref.dtype)
        v_b = v.astype(BF16)
        for d in range(2):
            rev = d == 1
            wo = 3 * nd + d * RWKV_W_LORA
            ao = 3 * nd + 2 * RWKV_W_LORA + d * RWKV_A_LORA
            w_lo = _dot1(jnp.tanh(za[:, wo:wo + RWKV_W_LORA]), w2_ref[d])
            log_w = -_softplus(-(w0_ref[d] + w_lo)) - 0.5
            logdec = -jnp.exp(log_w)
            a = _sigmoid(a0_ref[d] + _dot1(za[:, ao:ao + RWKV_A_LORA], a2_ref[d]))
            kd = k * (1.0 + (a - 1.0) * ka_ref[...])
            bvec = kk * a
            ld_hi, ld_lo = _split(logdec)
            tri = _incl_mask(c, rev).astype(BF16)
            cum = _dg(tri, ld_hi, _NN) + _dg(tri, ld_lo, _NN)
            total = cum[0:1, :] if rev else cum[c - 1:c, :]
            e_neg = jnp.exp(-cum)
            e_end = jnp.exp(total - cum)
            pm_d = kk * jnp.exp(cum - logdec)
            rm_d = r * jnp.exp(cum)
            units.append((rb, d))
            pm.append(pm_d)
            rm.append(rm_d)
            pr.append(jnp.concatenate([pm_d, rm_d], axis=0).astype(BF16))
            rhs.append(jnp.concatenate([bd((bvec * e_neg).astype(BF16)),
                                        bd((kd * e_neg).astype(BF16))], axis=0))
            vb.append(v_b)
            qe.append((bvec * e_end).astype(BF16))
            ke.append((kd * e_end).astype(BF16))
            gam.append(jnp.exp(total))
            gmask.append(jnp.logical_or(cc > rr if rev else cc < rr, diag_ok))
    n = len(units)
    a4 = [jnp.where(gmask[i], _dg(pr[i], rhs[i], _NT), 0.0) for i in range(n)]
    a4b = [x.astype(BF16) for x in a4]
    pw = [_dg(a4b[i][:c, :nd], bd(a4b[i][:c, :nd]), _NN) for i in range(n)]
    lv = [_dg(a4b[i][:, nd:], bd(vb[i]), _NN) for i in range(n)]
    tinv = [eye4_f - a4[i][:c, :nd] for i in range(n)]
    covered = 2
    while True:
        pwd = [bd(x.astype(BF16)) for x in pw]
        tinv = [tinv[i] + _dg(tinv[i].astype(BF16), pwd[i], _NN) for i in range(n)]
        covered *= 2
        if covered >= c:
            break
        pw = [_dg(pw[i].astype(BF16), pwd[i], _NN) for i in range(n)]
    wz = [_dg(tinv[i].astype(BF16),
              jnp.concatenate([bd(pm[i].astype(BF16)), bd(lv[i][:c].astype(BF16))], axis=1),
              _NN).astype(BF16) for i in range(n)]
    awz = [_dg(a4b[i][c:, :nd],
               jnp.concatenate([bd(wz[i][:, :nd]), bd(wz[i][:, nd:])], axis=1), _NN)
           for i in range(n)]
    eye = _iota((hd, hd), 0) == _iota((hd, hd), 1)
    for i, (rb, d) in enumerate(units):
        g_ref[d, rb] = (rm[i] - awz[i][:, :nd]).astype(g_ref.dtype)
        yl_ref[d, rb] = (lv[i][c:] - awz[i][:, nd:]).astype(yl_ref.dtype)
        ats, bss = [], []
        for h in range(nh):
            sl = slice(h * hd, (h + 1) * hd)
            wz_h = jnp.concatenate([wz[i][:, sl], wz[i][:, nd + h * hd:nd + (h + 1) * hd]], axis=1)
            qwz = _dg(qe[i][:, sl], wz_h, _TN)
            ats.append(jnp.where(eye, jnp.broadcast_to(gam[i][:, sl], (hd, hd)), 0.0)
                       - qwz[:, :hd])
            bss.append(_dg(ke[i][:, sl], vb[i][:, sl], _TN) - qwz[:, hd:])
        at_ref[d, rb] = jnp.concatenate(ats, axis=1).astype(at_ref.dtype)
        bs_ref[d, rb] = jnp.concatenate(bss, axis=1).astype(bs_ref.dtype)


def _rwkv_local(za, k_k, k_a, r_k, w0, w2, a0, a2, g2, *, nb):
    b, l, _ = za.shape
    c = CHUNK
    nd = RWKV_DIM
    full = lambda shape: pl.BlockSpec(shape, lambda i, j: (0,) * len(shape))
    dir_spec = pl.BlockSpec((2, nb, c, nd), lambda i, j: (0, i, j, 0))
    dir_shape = jax.ShapeDtypeStruct((2, b, l, nd), ACT)
    row_spec = pl.BlockSpec((nb, c, nd), lambda i, j: (i, j, 0))
    row_shape = jax.ShapeDtypeStruct((b, l, nd), ACT)
    return pl.pallas_call(
        _rwkv_local_kernel,
        grid=(b // nb, l // c),
        in_specs=[
            pl.BlockSpec((nb, c, RWKV_COLS), lambda i, j: (i, j, 0)),
            full((1, nd)), full((1, nd)), full((1, nd)),
            full((2, 1, nd)), full((2, RWKV_W_LORA, nd)),
            full((2, 1, nd)), full((2, RWKV_A_LORA, nd)),
            full((RWKV_G_LORA, nd)),
        ],
        out_specs=[dir_spec] * 4 + [row_spec] * 2,
        out_shape=[dir_shape] * 4 + [row_shape] * 2,
        compiler_params=pltpu.CompilerParams(
            dimension_semantics=("arbitrary", "arbitrary"), vmem_limit_bytes=VMEM_LIMIT),
        name="rwkv_local",
    )(za, k_k.reshape(1, -1), k_a.reshape(1, -1), r_k.reshape(1, -1), w0.reshape(2, 1, nd),
      w2, a0.reshape(2, 1, nd), a2, g2)


def _rwkv_seq_kernel(gf_ref, ylf_ref, atf_ref, bsf_ref, gr_ref, ylr_ref, atr_ref, bsr_ref,
                     yf_ref, yr_ref, m_ref):
    nb = yf_ref.shape[0]

    @pl.when(pl.program_id(1) == 0)
    def _():
        m_ref[...] = jnp.zeros_like(m_ref)

    dirs = ((gf_ref, ylf_ref, atf_ref, bsf_ref, yf_ref), (gr_ref, ylr_ref, atr_ref, bsr_ref, yr_ref))
    units = [(rb, d) for rb in range(nb) for d in range(2)]
    keep = _diag_mask(RWKV_DIM, RWKV_DIM, HEAD_DIM, HEAD_DIM)
    m_bd = [_block_diag(m_ref[u].astype(BF16), RWKV_HEADS, HEAD_DIM, HEAD_DIM, keep)
            for u in range(len(units))]
    y = [_dg(dirs[d][0][0, rb], m_bd[u], _NN) for u, (rb, d) in enumerate(units)]
    m_new = [_dg(dirs[d][2][0, rb], m_bd[u], _NN) for u, (rb, d) in enumerate(units)]
    for u, (rb, d) in enumerate(units):
        dirs[d][4][rb] = (y[u] + dirs[d][1][0, rb].astype(F32)).astype(dirs[d][4].dtype)
        m_ref[u] = m_new[u] + dirs[d][3][0, rb].astype(F32)


def _rwkv_seq(g, yl, at, bs, *, ctx_len, nb):
    _, b, l, nd = g.shape
    c = CHUNK
    n_chunks = l // c
    n_ctx_chunks = ctx_len // c
    rev_idx = lambda j: _chunk_index(j, n_ctx_chunks, n_chunks, True)
    fwd = pl.BlockSpec((1, nb, c, nd), lambda i, j: (0, i, j, 0))
    bwd = pl.BlockSpec((1, nb, c, nd), lambda i, j: (1, i, rev_idx(j), 0))
    y_shape = jax.ShapeDtypeStruct((b, l, nd), ACT)
    return pl.pallas_call(
        _rwkv_seq_kernel,
        grid=(b // nb, n_chunks),
        in_specs=[fwd] * 4 + [bwd] * 4,
        out_specs=[pl.BlockSpec((nb, c, nd), lambda i, j: (i, j, 0)),
                   pl.BlockSpec((nb, c, nd), lambda i, j: (i, rev_idx(j), 0))],
        out_shape=[y_shape, y_shape],
        scratch_shapes=[pltpu.VMEM((nb * 2, HEAD_DIM, RWKV_DIM), F32)],
        compiler_params=pltpu.CompilerParams(
            dimension_semantics=("arbitrary", "arbitrary"), vmem_limit_bytes=VMEM_LIMIT),
        name="rwkv_seq",
    )(g, yl, at, bs, g, yl, at, bs)


def _cummax_rows(x, rev):
    n = x.shape[0]
    row = _iota(x.shape, 0)
    shift = 1
    while shift < n:
        if rev:
            moved = jnp.where(row >= n - shift, -jnp.inf, pltpu.roll(x, n - shift, 0))
        else:
            moved = jnp.where(row < shift, -jnp.inf, pltpu.roll(x, shift, 0))
        x = jnp.maximum(x, moved)
        shift *= 2
    return x


def _head_lane(d):
    return d * 2 * MLSTM_HEADS + MLSTM_HEADS


def _sel_expand(d, width):
    n = MLSTM_HEADS * width
    return (_iota((LANES, n), 0) == _head_lane(d) + _iota((LANES, n), 1) // width).astype(BF16)


def _sel_reduce(d, width):
    n = MLSTM_HEADS * width
    return (_iota((n, LANES), 1) == _head_lane(d) + _iota((n, LANES), 0) // width).astype(BF16)


def _diag_mask(rows, cols, row_block, col_block):
    return (_iota((rows, cols), 0) // row_block) == (_iota((rows, cols), 1) // col_block)


def _block_diag(x, reps, row_block, col_block, keep=None):
    t = jnp.concatenate([x] * reps, axis=0)
    if keep is None:
        keep = _diag_mask(t.shape[0], t.shape[1], row_block, col_block)
    return jnp.where(keep, t, jnp.zeros_like(t))


def _mlstm_local_kernel(zm_ref, zg_ref, gb_ref, num_ref, den_ref, mi_ref, b_ref, cl_ref,
                        rows_ref):
    c = CHUNK
    nh = MLSTM_HEADS
    dk = MLSTM_QK
    dv = MLSTM_V
    nb = zm_ref.shape[0]
    lane_j = _iota((c, nh * c), 1) % c
    row_s = _iota((c, nh * c), 0)
    cl_keep = _diag_mask(nh * dk, MLSTM_DIM, dk, dv)
    k_keep = _diag_mask(nh * c, nh * dk, c, dk)
    v_keep = _diag_mask(nh * c, MLSTM_DIM, c, dv)
    sel_c = [_sel_expand(d, c) for d in range(2)]
    sel_k = [_sel_expand(d, dk) for d in range(2)]
    sel_r = [_sel_reduce(d, c) for d in range(2)]
    k, vb, log_f, li, qk, vbd = [], [], [], [], [], []
    for rb in range(nb):
        zm = zm_ref[rb]
        qb = (zm[:, :nh * dk].astype(F32) * (dk ** -0.5)).astype(BF16)
        k.append(zm[:, nh * dk:MLSTM_QKW].astype(F32))
        vb.append(zm[:, MLSTM_QKW:MLSTM_QKW + MLSTM_DIM].astype(BF16))
        gates = zg_ref[rb] + gb_ref[...]
        capped = GATE_SOFTCAP * jnp.tanh(gates * (1.0 / GATE_SOFTCAP))
        log_f.append(-_softplus(-capped))
        li.append(pltpu.roll(capped, nh, 1))
        qk.append(_dg(qb, _block_diag(k[rb].astype(BF16), nh, c, dk, k_keep), _NT))
        vbd.append(_block_diag(vb[rb], nh, c, dv, v_keep))
    units = [(rb, d) for rb in range(nb) for d in range(2)]
    b_all = [_dot_exact_lhs(_incl_mask(c, d == 1).astype(BF16), log_f[rb]) for rb, d in units]
    x = [li[rb] - b_all[u] for u, (rb, d) in enumerate(units)]
    cm = [_cummax_rows(x[u], d == 1) for u, (rb, d) in enumerate(units)]
    e1 = [_dot_exact_rhs(-cm[u], sel_c[d]) for u, (rb, d) in enumerate(units)]
    xe = [_dot_exact_rhs(x[u], sel_c[d]) for u, (rb, d) in enumerate(units)]
    g_row, m_loc, e32 = [], [], []
    for u, (rb, d) in enumerate(units):
        last = 0 if d == 1 else c - 1
        g_row.append(b_all[u][last:last + 1, :])
        w_end = g_row[u] - b_all[u] + li[rb]
        m_loc.append(jnp.max(w_end, axis=0, keepdims=True))
        e32.append(_dot2_rhs01(jnp.exp(w_end - m_loc[u]), sel_k[d]))
    wi = []
    for u, (rb, d) in enumerate(units):
        x_row = jnp.sum(jnp.where(row_s == lane_j, xe[u], 0.0), axis=0, keepdims=True)
        earlier = (lane_j >= row_s) if d == 1 else (lane_j <= row_s)
        wi.append((qk[rb] * jnp.exp(jnp.where(earlier, e1[u] + x_row, -jnp.inf))).astype(BF16))
    nd = [_dg(wi[u], jnp.concatenate([vbd[rb], sel_r[d]], axis=1), _NN)
          for u, (rb, d) in enumerate(units)]
    ke = [k[rb] * e32[u] for u, (rb, d) in enumerate(units)]
    full = [_dg(ke[u].astype(BF16), vb[rb], _TN) for u, (rb, d) in enumerate(units)]
    for u, (rb, d) in enumerate(units):
        num_ref[d, rb] = nd[u][:, :MLSTM_DIM].astype(num_ref.dtype)
        den_ref[d, rb] = nd[u][:, MLSTM_DIM:]
        mi_ref[d, rb] = b_all[u] + cm[u]
        b_ref[d, rb] = b_all[u]
        f = jnp.where(cl_keep, full[u], 0.0)
        cl_ref[d, rb] = ((f[0:dk] + f[dk:2 * dk])
                         + (f[2 * dk:3 * dk] + f[3 * dk:4 * dk])).astype(cl_ref.dtype)
        rows_ref[d, rb] = jnp.concatenate(
            [jnp.sum(ke[u], axis=0, keepdims=True), m_loc[u], g_row[u],
             jnp.zeros((SUBLANES - 3, LANES), F32)], axis=0)


def _mlstm_local(zm, zg, gate_bias, *, nb):
    b, l, _ = zm.shape
    c = CHUNK
    n_chunks = l // c
    spec = lambda rows, w, dt: pl.BlockSpec((2, nb, rows, w), lambda i, j: (0, i, j, 0))
    shape = lambda rows, w, dt: jax.ShapeDtypeStruct((2, b, n_chunks * rows, w), dt)
    outs = [(c, MLSTM_DIM, ACT), (c, LANES, F32), (c, LANES, F32), (c, LANES, F32),
            (MLSTM_QK, MLSTM_DIM, ACT), (SUBLANES, LANES, F32)]
    return pl.pallas_call(
        _mlstm_local_kernel,
        grid=(b // nb, n_chunks),
        in_specs=[
            pl.BlockSpec((nb, c, ZM_COLS), lambda i, j: (i, j, 0)),
            pl.BlockSpec((nb, c, LANES), lambda i, j: (i, j, 0)),
            pl.BlockSpec((1, LANES), lambda i, j: (0, 0)),
        ],
        out_specs=[spec(*o) for o in outs],
        out_shape=[shape(*o) for o in outs],
        compiler_params=pltpu.CompilerParams(
            dimension_semantics=("arbitrary", "arbitrary"), vmem_limit_bytes=VMEM_LIMIT),
        name="mlstm_local",
    )(zm, zg, gate_bias)


def _mlstm_seq_kernel(qf_ref, numf_ref, denf_ref, mif_ref, bf_ref, clf_ref, rowsf_ref,
                      qr_ref, numr_ref, denr_ref, mir_ref, br_ref, clr_ref, rowsr_ref,
                      hf_ref, hr_ref, cbd_ref, n_ref, m_ref):
    c = CHUNK
    nh = MLSTM_HEADS
    dk = MLSTM_QK
    dv = MLSTM_V
    nb = hf_ref.shape[0]

    @pl.when(pl.program_id(1) == 0)
    def _():
        cbd_ref[...] = jnp.zeros_like(cbd_ref)
        n_ref[...] = jnp.zeros_like(n_ref)
        m_ref[...] = jnp.zeros_like(m_ref)

    dirs = ((qf_ref, numf_ref, denf_ref, mif_ref, bf_ref, clf_ref, rowsf_ref, hf_ref),
            (qr_ref, numr_ref, denr_ref, mir_ref, br_ref, clr_ref, rowsr_ref, hr_ref))
    units = [(rb, d) for rb in range(nb) for d in range(2)]
    nu = len(units)
    cl_keep = _diag_mask(nh * dk, MLSTM_DIM, dk, dv)
    sel_v = [_sel_expand(d, dv) for d in range(2)]
    sel_k = [_sel_expand(d, dk) for d in range(2)]
    sel_r = [_sel_reduce(d, dk) for d in range(2)]
    q = [dirs[d][0][rb].astype(F32) * (dk ** -0.5) for rb, d in units]
    rows = [dirs[d][6][0, rb] for rb, d in units]
    m_row = [m_ref[u] for u in range(nu)]
    n_row = [n_ref[u] for u in range(nu)]
    cbd = [cbd_ref[u] for u in range(nu)]
    qn = [_dot2_rhs01(q[u] * n_row[u], sel_r[d]) for u, (rb, d) in enumerate(units)]
    qc = [_dg(q[u].astype(BF16), cbd[u].astype(BF16), _NN) for u in range(nu)]
    a12, srow = [], []
    for u, (rb, d) in enumerate(units):
        mi = dirs[d][3][0, rb]
        log_inter = dirs[d][4][0, rb] + m_row[u]
        m_out = jnp.maximum(log_inter, mi)
        s_intra = jnp.exp(mi - m_out)
        s_inter = jnp.exp(log_inter - m_out)
        den = s_intra * dirs[d][2][0, rb] + s_inter * qn[u]
        dinv = 1.0 / jnp.maximum(jnp.abs(den), jnp.exp(-m_out))
        a12.append(jnp.concatenate([s_intra * dinv, s_inter * dinv], axis=0))
        m_loc, g_row = rows[u][1:2], rows[u][2:3]
        m_new = jnp.maximum(g_row + m_row[u], m_loc)
        srow.append(jnp.concatenate(
            [jnp.exp(g_row + m_row[u] - m_new), jnp.exp(m_loc - m_new),
             jnp.zeros((SUBLANES - 2, LANES), F32)], axis=0))
        m_ref[u] = m_new
    a12e = [_dot2_rhs01(a12[u], sel_v[d]) for u, (rb, d) in enumerate(units)]
    s_v = [_dot2_rhs01(srow[u], sel_v[d]) for u, (rb, d) in enumerate(units)]
    s_k = [_dot2_rhs01(srow[u], sel_k[d]) for u, (rb, d) in enumerate(units)]
    for u, (rb, d) in enumerate(units):
        h_out = a12e[u][:c] * dirs[d][1][0, rb].astype(F32) + a12e[u][c:] * qc[u]
        dirs[d][7][rb] = h_out.astype(dirs[d][7].dtype)
        cl_full = _block_diag(dirs[d][5][0, rb].astype(F32), nh, dk, dv, cl_keep)
        cbd_ref[u] = s_v[u][0:1] * cbd[u] + s_v[u][1:2] * cl_full
        n_ref[u] = s_k[u][0:1] * n_row[u] + s_k[u][1:2] * rows[u][0:1]


def _mlstm_seq(zm, num, den, mi, bsum, cl, rows, *, ctx_len, nb):
    b, l, _ = zm.shape
    c = CHUNK
    n_chunks = l // c
    n_ctx_chunks = ctx_len // c
    rev_idx = lambda j: _chunk_index(j, n_ctx_chunks, n_chunks, True)

    def specs(d):
        cidx = (lambda j: j) if d == 0 else rev_idx
        blk = lambda r, w: pl.BlockSpec((1, nb, r, w), lambda i, j: (d, i, cidx(j), 0))
        return [pl.BlockSpec((nb, c, MLSTM_HEADS * MLSTM_QK), lambda i, j: (i, cidx(j), 0)),
                blk(c, MLSTM_DIM), blk(c, LANES), blk(c, LANES), blk(c, LANES),
                blk(MLSTM_QK, MLSTM_DIM), blk(SUBLANES, LANES)]

    h_shape = jax.ShapeDtypeStruct((b, l, MLSTM_DIM), ACT)
    args = (zm, num, den, mi, bsum, cl, rows)
    return pl.pallas_call(
        _mlstm_seq_kernel,
        grid=(b // nb, n_chunks),
        in_specs=specs(0) + specs(1),
        out_specs=[pl.BlockSpec((nb, c, MLSTM_DIM), lambda i, j: (i, j, 0)),
                   pl.BlockSpec((nb, c, MLSTM_DIM), lambda i, j: (i, rev_idx(j), 0))],
        out_shape=[h_shape, h_shape],
        scratch_shapes=[
            pltpu.VMEM((2 * nb, MLSTM_HEADS * MLSTM_QK, MLSTM_DIM), F32),
            pltpu.VMEM((2 * nb, 1, LANES), F32),
            pltpu.VMEM((2 * nb, 1, LANES), F32),
        ],
        compiler_params=pltpu.CompilerParams(
            dimension_semantics=("arbitrary", "arbitrary"), vmem_limit_bytes=VMEM_LIMIT),
        name="mlstm_seq",
    )(*args, *args)


def _rope_swap(x):
    lane = _iota(x.shape, 1)
    half = MLA_ROPE // 2
    return jnp.where(lane < MLA_NOPE + half, pltpu.roll(x, LANES - half, 1),
                     pltpu.roll(x, half, 1))


def _mla_proj_kernel(zb_ref, cq_ref, sq_ref, ck_ref, sk_ref, qg_ref, kvg_ref,
                     wq_ref, wk_ref, wv_ref, q_ref, k_ref, v_ref):
    hq = MLA_HEADS * QPAD
    zb = zb_ref[0].astype(F32)
    cq = zb[:, :MLA_Q_RANK]
    ckv = zb[:, MLA_Q_RANK:MLA_Q_RANK + MLA_KV_RANK]
    kr = zb[:, MLA_Q_RANK + MLA_KV_RANK:]
    cqn = cq * lax.rsqrt(jnp.mean(cq * cq, axis=-1, keepdims=True) + NORM_EPS) * qg_ref[...]
    ckn = ckv * lax.rsqrt(jnp.mean(ckv * ckv, axis=-1, keepdims=True) + NORM_EPS) * kvg_ref[...]
    q_all = _dot1(cqn, wq_ref[...])
    k_all = _dot1(ckn, wk_ref[...])
    v_ref[0] = _dg(wv_ref[...], ckn.astype(BF16), _NT).astype(BF16)
    cos_q, sin_q, cos_k = cq_ref[...], sq_ref[...], ck_ref[...]
    kr_rot = _rope_swap(kr) * sk_ref[...]
    inv_dim = 1.0 / MLA_QK
    for h in range(MLA_HEADS):
        sl = slice(h * QPAD, (h + 1) * QPAD)
        qh = q_all[:, sl]
        q_inv = lax.rsqrt(jnp.sum(qh * qh, axis=-1, keepdims=True) * inv_dim + NORM_EPS)
        q_ref[0, :, sl] = ((qh * cos_q + q_all[:, hq + h * QPAD:hq + (h + 1) * QPAD] * sin_q)
                           * q_inv).astype(BF16)
        kh = k_all[:, sl] + kr
        k_inv = lax.rsqrt(jnp.sum(kh * kh, axis=-1, keepdims=True) * inv_dim + NORM_EPS)
        k_ref[0, :, sl] = ((kh * cos_k + kr_rot) * k_inv).astype(BF16)


def _mla_proj(zb, tables, q_norm_g, kv_norm_g, wq_r, wk_r, wv_r, *, tm):
    b, l, _ = zb.shape
    hq = MLA_HEADS * QPAD
    full = lambda shape: pl.BlockSpec(shape, lambda i, j: (0,) * len(shape))
    table = pl.BlockSpec((tm, QPAD), lambda i, j: (j, 0))
    return pl.pallas_call(
        _mla_proj_kernel,
        grid=(b, l // tm),
        in_specs=[
            pl.BlockSpec((1, tm, ZB_COLS), lambda i, j: (i, j, 0)),
            table, table, table, table,
            full((1, MLA_Q_RANK)), full((1, MLA_KV_RANK)),
            full((MLA_Q_RANK, 2 * hq)), full((MLA_KV_RANK, hq)), full((MLA_DIM, MLA_KV_RANK)),
        ],
        out_specs=[
            pl.BlockSpec((1, tm, hq), lambda i, j: (i, j, 0)),
            pl.BlockSpec((1, tm, hq), lambda i, j: (i, j, 0)),
            pl.BlockSpec((1, MLA_DIM, tm), lambda i, j: (i, 0, j)),
        ],
        out_shape=[
            jax.ShapeDtypeStruct((b, l, hq), BF16),
            jax.ShapeDtypeStruct((b, l, hq), BF16),
            jax.ShapeDtypeStruct((b, MLA_DIM, l), BF16),
        ],
        compiler_params=pltpu.CompilerParams(
            dimension_semantics=("arbitrary", "arbitrary"), vmem_limit_bytes=VMEM_LIMIT),
        name="mla_proj",
    )(zb, *tables, q_norm_g.reshape(1, -1), kv_norm_g.reshape(1, -1), wq_r, wk_r, wv_r)


def _attend_tile(q_ref, k_ref, vt_ref, o_ref, n_keys):
    tq = q_ref.shape[1]
    starts = list(range(0, n_keys, KV_BLOCK))
    bounds = [(s0, min(s0 + KV_BLOCK, n_keys)) for s0 in starts]
    n_blocks = len(bounds)
    heads = range(ATTN_HEADS_PER_STEP)
    q = [q_ref[0, :, h * QPAD:(h + 1) * QPAD] for h in heads]

    def scores(j):
        lo, hi = bounds[j]
        return [_dg(k_ref[0, lo:hi, h * QPAD:(h + 1) * QPAD], q[h], _NT)
                for h in heads]

    m = [jnp.full((1, tq), -jnp.inf, F32) for _ in heads]
    acc = [jnp.zeros((MLA_V + 2 * SUBLANES, tq), F32) for _ in heads]
    pending = [scores(j) for j in range(min(SCORE_LOOKAHEAD, n_blocks))]
    for j in range(n_blocks):
        if j + SCORE_LOOKAHEAD < n_blocks:
            pending.append(scores(j + SCORE_LOOKAHEAD))
        s = pending.pop(0)
        lo, hi = bounds[j]
        ones_rows = jnp.ones((2 * SUBLANES, hi - lo), BF16)
        for h in heads:
            vt = jnp.concatenate([vt_ref[0, h * MLA_V:(h + 1) * MLA_V, lo:hi], ones_rows], axis=0)
            m_new = jnp.maximum(m[h], jnp.max(s[h], axis=0, keepdims=True))
            p = jnp.exp2(s[h] - m_new).astype(BF16)
            acc[h] = jnp.exp2(m[h] - m_new) * acc[h] + _dg(vt, p, _NN)
            m[h] = m_new
    out_t = jnp.concatenate([acc[h][:MLA_V] / acc[h][MLA_V:MLA_V + 1] for h in heads],
                            axis=0)
    o_ref[0] = out_t.T.astype(o_ref.dtype)


def _mla_attn_kernel(q_ref, k_ref, v_ref, o_ref, *, ctx_len, first_tile, n_ctx_tiles):
    n_all = k_ref.shape[1]
    if first_tile >= n_ctx_tiles:
        _attend_tile(q_ref, k_ref, v_ref, o_ref, n_all)
        return
    is_ctx = (pl.program_id(2) + first_tile) < n_ctx_tiles

    @pl.when(is_ctx)
    def _():
        _attend_tile(q_ref, k_ref, v_ref, o_ref, ctx_len)

    @pl.when(jnp.logical_not(is_ctx))
    def _():
        _attend_tile(q_ref, k_ref, v_ref, o_ref, n_all)


def _mla_attn(q, k, v, *, tq, ctx_len, first_tile):
    b, l, _ = q.shape
    assert l % LANES == 0 and ctx_len % LANES == 0
    n_q = l // tq - first_tile
    hps = ATTN_HEADS_PER_STEP
    kern = functools.partial(_mla_attn_kernel, ctx_len=ctx_len, first_tile=first_tile,
                             n_ctx_tiles=ctx_len // tq)
    return pl.pallas_call(
        kern,
        grid=(b, MLA_HEADS // hps, n_q),
        in_specs=[
            pl.BlockSpec((1, tq, hps * QPAD), lambda i, p, j: (i, j + first_tile, p)),
            pl.BlockSpec((1, l, hps * QPAD), lambda i, p, j: (i, 0, p)),
            pl.BlockSpec((1, hps * MLA_V, l), lambda i, p, j: (i, p, 0)),
        ],
        out_specs=pl.BlockSpec((1, tq, hps * MLA_V), lambda i, p, j: (i, j + first_tile, p)),
        out_shape=jax.ShapeDtypeStruct((b, l, MLA_DIM), ACT),
        compiler_params=pltpu.CompilerParams(
            dimension_semantics=("arbitrary", "arbitrary", "arbitrary"),
            vmem_limit_bytes=VMEM_LIMIT),
        name="mla_attn",
    )(q, k, v)


def _mix_ffn_kernel(x_ref, yf_ref, yr_ref, bonus_ref, gate_ref, attn_ref, hf_ref, hr_ref,
                    o_ref, mod_ref, lng_ref, lnb_ref, mng_ref, wa_ref, wb_ref, wm_ref,
                    g2_ref, w1_ref, w2_ref, out_ref, *, hidden):
    ones_bd = _seg_ones(RWKV_DIM, HEAD_DIM)
    inv = 1.0 / HEAD_DIM
    y = yf_ref[0].astype(F32) + yr_ref[0].astype(F32)
    mean = _segsum(y, ones_bd) * inv
    yc = y - mean
    var = _segsum(yc * yc, ones_bd) * inv
    yn = yc * lax.rsqrt(var + RWKV_GN_EPS) * lng_ref[...] + lnb_ref[...]
    ya = (yn + bonus_ref[0].astype(F32)) * gate_ref[0].astype(F32)
    hm = hf_ref[0].astype(F32) + hr_ref[0].astype(F32)
    hn = hm * lax.rsqrt(_segsum(hm * hm, ones_bd) * inv + NORM_EPS)
    ym = hn * mng_ref[...] * _sigmoid(o_ref[0].astype(F32))
    mix = (_dot1(ya, wa_ref[...]) + _dot1(attn_ref[0], wb_ref[...]) + _dot1(ym, wm_ref[...]))
    x = x_ref[0] + mod_ref[0, 0, 0:1, :] * mix
    h = _rmsnorm_mod(x, g2_ref[...], mod_ref[0, 0, 1:2, :], mod_ref[0, 0, 2:3, :]).astype(BF16)
    gu = _dg(h, w1_ref[...], _NN)
    act = (_silu(gu[:, :hidden]) * gu[:, hidden:]).astype(BF16)
    out_ref[0] = x + mod_ref[0, 0, 3:4, :] * _dg(act, w2_ref[...], _NN)


def _mix_ffn(x, yf, yr, bonus, gate, attn, hf, hr, zm, modsel, ln_g, ln_b, mnorm_g,
             wa, wb, wm, norm2_g, w1, w2, *, tm, ctx_len, first_tile):
    b, l, d = x.shape
    hidden = w2.shape[0]
    n_ctx_tiles = ctx_len // tm
    n_tiles = l // tm - first_tile
    row = lambda w: pl.BlockSpec((1, tm, w), lambda i, j: (i, j + first_tile, 0))
    full = lambda shape: pl.BlockSpec(shape, lambda i, j: (0,) * len(shape))
    seg = lambda j: jnp.where(j + first_tile >= n_ctx_tiles, 1, 0)
    o_block = (MLSTM_QKW + MLSTM_DIM) // MLSTM_DIM
    return pl.pallas_call(
        functools.partial(_mix_ffn_kernel, hidden=hidden),
        grid=(b, n_tiles),
        in_specs=[
            row(d), row(RWKV_DIM), row(RWKV_DIM), row(RWKV_DIM), row(RWKV_DIM), row(MLA_DIM),
            row(MLSTM_DIM), row(MLSTM_DIM),
            pl.BlockSpec((1, tm, MLSTM_DIM), lambda i, j: (i, j + first_tile, o_block)),
            pl.BlockSpec((1, 1, 4, d), lambda i, j: (i, seg(j), 0, 0)),
            full((1, RWKV_DIM)), full((1, RWKV_DIM)), full((1, MLSTM_DIM)),
            full((RWKV_DIM, d)), full((MLA_DIM, d)), full((MLSTM_DIM, d)),
            full((1, d)), full((d, 2 * hidden)), full((hidden, d)),
        ],
        out_specs=pl.BlockSpec((1, tm, d), lambda i, j: (i, j, 0)),
        out_shape=jax.ShapeDtypeStruct((b, n_tiles * tm, d), F32),
        compiler_params=pltpu.CompilerParams(
            dimension_semantics=("arbitrary", "arbitrary"), vmem_limit_bytes=VMEM_LIMIT),
        name="mix_ffn",
    )(x, yf, yr, bonus, gate, attn, hf, hr, zm, modsel, ln_g.reshape(1, -1),
      ln_b.reshape(1, -1), mnorm_g.reshape(1, -1), wa, wb, wm, norm2_g.reshape(1, d), w1, w2)


def _arrange_w_in(w_in):
    d = w_in.shape[0]
    wa = w_in[:, :RWKV_COLS]
    wb = w_in[:, RWKV_COLS:RWKV_COLS + MLA_COLS]
    wm = w_in[:, RWKV_COLS + MLA_COLS:]
    gates = wm[:, MLSTM_QKW + 2 * MLSTM_DIM:]
    gate_blk = jnp.pad(gates, ((0, 0), (0, LANES - gates.shape[1])))
    kr = wb[:, MLA_Q_RANK + MLA_KV_RANK:]
    kr_blk = jnp.pad(kr, ((0, 0), (MLA_NOPE, LANES - MLA_NOPE - MLA_ROPE)))
    out = jnp.concatenate([wa, wm[:, :MLSTM_QKW + 2 * MLSTM_DIM], gate_blk,
                           wb[:, :MLA_Q_RANK + MLA_KV_RANK], kr_blk], axis=1)
    assert out.shape == (d, IN_COLS_PAD)
    return out.astype(BF16)


def _arrange_mla_weights(w_uq, w_ukv):
    rq = w_uq.shape[0]
    wq = w_uq.reshape(rq, MLA_HEADS, MLA_QK)
    wq = jnp.pad(wq, ((0, 0), (0, 0), (0, QPAD - MLA_QK)))
    half = MLA_ROPE // 2
    wq_swap = jnp.concatenate(
        [jnp.zeros_like(wq[:, :, :MLA_NOPE]), wq[:, :, MLA_NOPE + half:MLA_QK],
         wq[:, :, MLA_NOPE:MLA_NOPE + half], jnp.zeros_like(wq[:, :, MLA_QK:])], axis=2)
    wq = jnp.concatenate([wq.reshape(rq, -1), wq_swap.reshape(rq, -1)], axis=1)
    rk = w_ukv.shape[0]
    wkv = w_ukv.reshape(rk, MLA_HEADS, MLA_NOPE + MLA_V)
    wk = jnp.pad(wkv[:, :, :MLA_NOPE], ((0, 0), (0, 0), (0, QPAD - MLA_NOPE)))
    wk = wk.reshape(rk, MLA_HEADS * QPAD)
    wv_t = wkv[:, :, MLA_NOPE:].reshape(rk, MLA_DIM).T
    return wq.astype(BF16), wk.astype(BF16), wv_t.astype(BF16)


def _rope_tables(seq_len, ctx_len):
    rows = seq_len // GRID_W
    row = jnp.repeat(jnp.arange(rows, dtype=F32), GRID_W)
    col = jnp.tile(jnp.arange(GRID_W, dtype=F32), rows)
    n_freq = MLA_ROPE // 4
    inv = jnp.power(ROPE_BASE, -jnp.arange(n_freq, dtype=F32) / n_freq)
    ang = jnp.concatenate([row[:, None] * inv, col[:, None] * inv], axis=-1)
    cos, sin = jnp.cos(ang), jnp.sin(ang)
    pad_l, pad_r = MLA_NOPE, QPAD - MLA_QK
    cos_f = jnp.concatenate([jnp.ones((seq_len, pad_l), F32), cos, cos,
                             jnp.ones((seq_len, pad_r), F32)], axis=1)
    sin_f = jnp.concatenate([jnp.zeros((seq_len, pad_l), F32), sin, sin,
                             jnp.zeros((seq_len, pad_r), F32)], axis=1)
    cos_f = jnp.concatenate([jnp.ones((ctx_len, QPAD), F32), cos_f], axis=0)
    sin_f = jnp.concatenate([jnp.zeros((ctx_len, QPAD), F32), sin_f], axis=0)
    return cos_f, sin_f


def _gained_tables(cos_f, sin_f, gain, scale):
    g = jnp.pad(gain, (0, QPAD - gain.shape[0]))
    half = MLA_ROPE // 2
    lo, mid, hi = MLA_NOPE, MLA_NOPE + half, MLA_QK
    g_swap = jnp.concatenate([g[:lo], g[mid:hi], g[lo:mid], g[hi:]])
    sign = jnp.where(jnp.arange(QPAD) < mid, -1.0, 1.0).astype(F32)
    return cos_f * (g * scale), sin_f * (g_swap * sign * scale)


def _gate_bias(i_b, f_b):
    gb = jnp.stack([i_b, f_b], axis=1).reshape(-1)
    return jnp.pad(gb, (0, LANES - gb.shape[0])).reshape(1, LANES)


def kernel(x, c, ctx, c_ctx, mod_w, mod_b, norm1_g, norm2_g, w_in, w_out, ffn_w_in, ffn_w_out, rwkv_mu, rwkv_w0, rwkv_w2, rwkv_a0, rwkv_a2, rwkv_g2, rwkv_k_k, rwkv_k_a, rwkv_r_k, rwkv_ln_g, rwkv_ln_b, mla_q_norm_g, mla_w_uq, mla_kv_norm_g, mla_w_ukv, mla_q_qknorm_g, mla_k_qknorm_g, mlstm_conv_w, mlstm_conv_b, mlstm_i_b, mlstm_f_b, mlstm_norm_g):
    bsz, seq, d = x.shape
    ctx_len = ctx.shape[1]
    depth = mod_w.shape[0]
    tm = min(256, ctx_len)
    assert ctx_len % tm == 0 and seq % tm == 0 and ctx_len % CHUNK == 0 and seq % CHUNK == 0
    nb = ROWS_PER_STEP if bsz % ROWS_PER_STEP == 0 else 1
    nb_seq = SEQ_ROWS_PER_STEP if bsz % SEQ_ROWS_PER_STEP == 0 else nb

    xu = jnp.concatenate([ctx, x], axis=1)
    n_ctx_tiles = ctx_len // tm

    rows = -(-(bsz + 1) // SUBLANES) * SUBLANES
    cvec = jnp.zeros((rows, d), F32).at[:bsz].set(c).at[bsz].set(c_ctx)
    mod = _modulation(cvec, mod_w, mod_b).reshape(depth, rows, 6, d)
    mod_lat = mod[:, :bsz]
    mod_ctx = jnp.broadcast_to(mod[:, bsz][:, None], mod_lat.shape)
    modsel = jnp.stack([mod_ctx, mod_lat], axis=2)

    cos_f, sin_f = _rope_tables(seq, ctx_len)

    for i in range(depth):
        last = i == depth - 1
        first_tile = n_ctx_tiles if last else 0
        w_r = _arrange_w_in(w_in[i])
        za, zm, zg, zb = _in_proj(xu, modsel[i, :, :, 0:2], norm1_g[i], w_r, rwkv_mu[i],
                              mlstm_conv_w[i], mlstm_conv_b[i], tm=tm, ctx_len=ctx_len)
        g_loc, y_loc, a_tr, b_st, bonus, gate = _rwkv_local(
            za, rwkv_k_k[i], rwkv_k_a[i], rwkv_r_k[i], rwkv_w0[i], rwkv_w2[i], rwkv_a0[i],
            rwkv_a2[i], rwkv_g2[i], nb=nb)
        yf, yr = _rwkv_seq(g_loc, y_loc, a_tr, b_st, ctx_len=ctx_len, nb=nb_seq)
        wq_r, wk_r, wv_r = _arrange_mla_weights(mla_w_uq[i], mla_w_ukv[i])
        tables = (_gained_tables(cos_f, sin_f, mla_q_qknorm_g[i], MLA_QK ** -0.5 * LOG2_E)
                  + _gained_tables(cos_f, sin_f, mla_k_qknorm_g[i], 1.0))
        q, k, v = _mla_proj(zb, tables, mla_q_norm_g[i], mla_kv_norm_g[i],
                            wq_r, wk_r, wv_r, tm=tm)
        attn = _mla_attn(q, k, v, tq=tm, ctx_len=ctx_len, first_tile=first_tile)
        gb = _gate_bias(mlstm_i_b[i], mlstm_f_b[i])
        hf, hr = _mlstm_seq(zm, *_mlstm_local(zm, zg, gb, nb=nb), ctx_len=ctx_len, nb=nb_seq)
        wo = w_out[i].astype(BF16)
        xu = _mix_ffn(xu, yf, yr, bonus, gate, attn, hf, hr, zm, modsel[i, :, :, 2:6],
                      rwkv_ln_g[i], rwkv_ln_b[i], mlstm_norm_g[i],
                      wo[:RWKV_DIM], wo[RWKV_DIM:RWKV_DIM + MLA_DIM], wo[RWKV_DIM + MLA_DIM:],
                      norm2_g[i], ffn_w_in[i].astype(BF16), ffn_w_out[i].astype(BF16),
                      tm=tm, ctx_len=ctx_len, first_tile=first_tile)
    return xu
```

```python
import functools

import jax
import jax.numpy as jnp
from jax import lax
from jax.experimental import pallas as pl
from jax.experimental.pallas import tpu as pltpu

F32 = jnp.float32
BF16 = jnp.bfloat16

HEAD_DIM = 64
NORM_EPS = 1e-6
GRID_W = 64
ROPE_BASE = 10000.0
LOG2_E = 1.4426950408889634
RWKV_HEADS = 4
RWKV_DIM = RWKV_HEADS * HEAD_DIM
RWKV_W_LORA = 64
RWKV_A_LORA = 64
RWKV_G_LORA = 128
RWKV_COLS = 3 * RWKV_DIM + 2 * RWKV_W_LORA + 2 * RWKV_A_LORA + RWKV_G_LORA
RWKV_GN_EPS = 64e-5
MLA_HEADS = 8
MLA_NOPE = 64
MLA_ROPE = 32
MLA_V = HEAD_DIM
MLA_QK = MLA_NOPE + MLA_ROPE
MLA_Q_RANK = 512
MLA_KV_RANK = 256
MLA_DIM = MLA_HEADS * MLA_V
MLA_COLS = MLA_Q_RANK + MLA_KV_RANK + MLA_ROPE
MLSTM_HEADS = 4
MLSTM_QK = 32
MLSTM_V = HEAD_DIM
MLSTM_DIM = MLSTM_HEADS * MLSTM_V
MLSTM_QKW = 2 * MLSTM_HEADS * MLSTM_QK
GATE_SOFTCAP = 15.0
MLSTM_COLS = MLSTM_QKW + 2 * MLSTM_DIM + 4 * MLSTM_HEADS

LANES = 128
SUBLANES = 8
CHUNK = HEAD_DIM
ROWS_PER_STEP = 8
SEQ_ROWS_PER_STEP = 8
KV_BLOCK = 256
STAT_SHIFT = 16
ATTN_HEADS_PER_STEP = 4
SCORE_LOOKAHEAD = 1
QPAD = LANES
ZB_COLS = MLA_Q_RANK + MLA_KV_RANK + LANES
ZM_COLS = MLSTM_QKW + 2 * MLSTM_DIM
SHIFT_COLS = RWKV_COLS + MLSTM_QKW
IN_COLS_PAD = RWKV_COLS + ZM_COLS + LANES + ZB_COLS
ACT = BF16
VMEM_LIMIT = 56 * 1024 * 1024


def _split(a):
    hi = a.astype(BF16)
    lo = (a - hi.astype(F32)).astype(BF16)
    return hi, lo


_NN = (((1,), (0,)), ((), ()))
_NT = (((1,), (1,)), ((), ()))
_TN = (((0,), (0,)), ((), ()))


def _dg(a, b, dims):
    return lax.dot_general(a, b, dims, preferred_element_type=F32)


def _dot1(a, b, dims=_NN):
    return _dg(a.astype(BF16), b.astype(BF16), dims)


def _dot3(a, b, dims=_NN):
    ah, al = _split(a)
    bh, bl = _split(b)
    return _dg(ah, bh, dims) + (_dg(ah, bl, dims) + _dg(al, bh, dims))


def _dot_exact_rhs(a, b01, dims=_NN):
    a1 = a.astype(BF16)
    r1 = a - a1.astype(F32)
    a2 = r1.astype(BF16)
    a3 = (r1 - a2.astype(F32)).astype(BF16)
    return _dg(a1, b01, dims) + (_dg(a2, b01, dims) + _dg(a3, b01, dims))


def _dot_exact_lhs(a01, b, dims=_NN):
    b1 = b.astype(BF16)
    r1 = b - b1.astype(F32)
    b2 = r1.astype(BF16)
    b3 = (r1 - b2.astype(F32)).astype(BF16)
    return _dg(a01, b1, dims) + (_dg(a01, b2, dims) + _dg(a01, b3, dims))


def _dot2_rhs01(a, b01):
    ah, al = _split(a)
    return _dg(ah, b01, _NN) + _dg(al, b01, _NN)


def _iota(shape, dim):
    return lax.broadcasted_iota(jnp.int32, shape, dim)


def _seg_ones(n, seg):
    r = _iota((n, n), 0) // seg
    c = _iota((n, n), 1) // seg
    return (r == c).astype(BF16)


def _segsum(x, ones_bd):
    return _dot2_rhs01(x, ones_bd)


def _softplus(x):
    return jnp.maximum(x, 0.0) + jnp.log(1.0 + jnp.exp(-jnp.abs(x)))


def _sigmoid(x):
    return 1.0 / (1.0 + jnp.exp(-x))


def _silu(x):
    return x * _sigmoid(x)


def _rmsnorm_mod(x, g, shift, scale):
    y = x * lax.rsqrt(jnp.mean(x * x, axis=-1, keepdims=True) + NORM_EPS)
    return (y * g) * (1.0 + scale) + shift


def _chunk_index(i, n_ctx_chunks, n_chunks, rev):
    if not rev:
        return i
    return jnp.where(i < n_ctx_chunks, n_ctx_chunks - 1 - i,
                     n_chunks - 1 - (i - n_ctx_chunks))


def _incl_mask(n, rev):
    r = _iota((n, n), 0)
    c = _iota((n, n), 1)
    return (c >= r) if rev else (c <= r)


def _mod_kernel(c_ref, w_ref, b_ref, o_ref):
    o_ref[0] = _dot3(_silu(c_ref[...]), w_ref[0]) + b_ref[0]


def _modulation(cvec, mod_w, mod_b):
    depth, d, n = mod_w.shape
    tn = 1536
    rows = cvec.shape[0]
    return pl.pallas_call(
        _mod_kernel,
        grid=(depth, n // tn),
        in_specs=[
            pl.BlockSpec((rows, d), lambda l, j: (0, 0)),
            pl.BlockSpec((1, d, tn), lambda l, j: (l, 0, j)),
            pl.BlockSpec((1, 1, tn), lambda l, j: (l, 0, j)),
        ],
        out_specs=pl.BlockSpec((1, rows, tn), lambda l, j: (l, 0, j)),
        out_shape=jax.ShapeDtypeStruct((depth, rows, n), F32),
        compiler_params=pltpu.CompilerParams(
            dimension_semantics=("arbitrary", "arbitrary"), vmem_limit_bytes=VMEM_LIMIT),
        name="adaln_mod",
    )(cvec, mod_w, mod_b.reshape(depth, 1, n))


def _inproj_kernel(x_ref, xp_ref, xn_ref, mod_ref, g_ref, w_ref, mu_ref, cw_ref, cb_ref,
                   za_ref, zm_ref, zg_ref, zb_ref, *, tm, n_ctx_tiles, n_tiles):
    j = pl.program_id(1)
    g = g_ref[...]
    shift = mod_ref[0, 0, 0:1, :]
    scale = mod_ref[0, 0, 1:2, :]
    h = _rmsnorm_mod(x_ref[0], g, shift, scale).astype(BF16)
    z = _dg(h, w_ref[...], _NN)
    halo = jnp.concatenate([xp_ref[0], xn_ref[0]], axis=0)
    hh = _rmsnorm_mod(halo, g, shift, scale).astype(BF16)
    zh = _dg(hh, w_ref[:, :SHIFT_COLS], _NN)
    prev_ok = jnp.logical_and(j != 0, j != n_ctx_tiles).astype(F32)
    next_ok = jnp.logical_and(j != n_ctx_tiles - 1, j != n_tiles - 1).astype(F32)
    zs = z[:, :SHIFT_COLS]
    row = _iota((tm, SHIFT_COLS), 0)
    prev = jnp.where(row == 0, zh[SUBLANES - 1:SUBLANES, :] * prev_ok, pltpu.roll(zs, 1, 0))
    nxt = jnp.where(row == tm - 1, zh[SUBLANES:SUBLANES + 1, :] * next_ok,
                    pltpu.roll(zs, tm - 1, 0))
    za = zs[:, :RWKV_COLS]
    shifted = za + mu_ref[...] * (0.5 * (prev[:, :RWKV_COLS] + nxt[:, :RWKV_COLS]) - za)
    za_ref[0] = shifted.astype(za_ref.dtype)
    qk = (prev[:, RWKV_COLS:] * cw_ref[0:1, :] + zs[:, RWKV_COLS:] * cw_ref[1:2, :]
          + nxt[:, RWKV_COLS:] * cw_ref[2:3, :] + cb_ref[...])
    zm_ref[0, :, :MLSTM_QKW] = _silu(qk).astype(zm_ref.dtype)
    zm_ref[0, :, MLSTM_QKW:] = z[:, SHIFT_COLS:RWKV_COLS + ZM_COLS].astype(zm_ref.dtype)
    gate_end = RWKV_COLS + ZM_COLS + LANES
    zg_ref[0] = z[:, RWKV_COLS + ZM_COLS:gate_end]
    zb_ref[0] = z[:, gate_end:].astype(zb_ref.dtype)


def _in_proj(x, modsel, norm_g, w_r, mu, conv_w, conv_b, *, tm, ctx_len):
    b, l, d = x.shape
    n_tiles = l // tm
    n_ctx_tiles = ctx_len // tm
    tm8 = tm // SUBLANES
    kern = functools.partial(_inproj_kernel, tm=tm, n_ctx_tiles=n_ctx_tiles, n_tiles=n_tiles)
    seg = lambda j: jnp.where(j >= n_ctx_tiles, 1, 0)
    return pl.pallas_call(
        kern,
        grid=(b, n_tiles),
        in_specs=[
            pl.BlockSpec((1, tm, d), lambda i, j: (i, j, 0)),
            pl.BlockSpec((1, SUBLANES, d), lambda i, j: (i, jnp.maximum(j * tm8 - 1, 0), 0)),
            pl.BlockSpec((1, SUBLANES, d),
                         lambda i, j: (i, jnp.minimum((j + 1) * tm8, l // SUBLANES - 1), 0)),
            pl.BlockSpec((1, 1, 2, d), lambda i, j: (i, seg(j), 0, 0)),
            pl.BlockSpec((1, d), lambda i, j: (0, 0)),
            pl.BlockSpec((d, IN_COLS_PAD), lambda i, j: (0, 0)),
            pl.BlockSpec((1, RWKV_COLS), lambda i, j: (0, 0)),
            pl.BlockSpec((3, MLSTM_QKW), lambda i, j: (0, 0)),
            pl.BlockSpec((1, MLSTM_QKW), lambda i, j: (0, 0)),
        ],
        out_specs=[
            pl.BlockSpec((1, tm, RWKV_COLS), lambda i, j: (i, j, 0)),
            pl.BlockSpec((1, tm, ZM_COLS), lambda i, j: (i, j, 0)),
            pl.BlockSpec((1, tm, LANES), lambda i, j: (i, j, 0)),
            pl.BlockSpec((1, tm, ZB_COLS), lambda i, j: (i, j, 0)),
        ],
        out_shape=[
            jax.ShapeDtypeStruct((b, l, RWKV_COLS), ACT),
            jax.ShapeDtypeStruct((b, l, ZM_COLS), ACT),
            jax.ShapeDtypeStruct((b, l, LANES), F32),
            jax.ShapeDtypeStruct((b, l, ZB_COLS), ACT),
        ],
        compiler_params=pltpu.CompilerParams(
            dimension_semantics=("arbitrary", "arbitrary"), vmem_limit_bytes=VMEM_LIMIT),
        name="in_proj",
    )(x, x, x, modsel, norm_g.reshape(1, d), w_r, mu.reshape(1, RWKV_COLS), conv_w,
      conv_b.reshape(1, MLSTM_QKW))


def _rwkv_local_kernel(za_ref, kk_ref, ka_ref, rk_ref, w0_ref, w2_ref, a0_ref, a2_ref, g2_ref,
                       g_ref, yl_ref, at_ref, bs_ref, bonus_ref, gate_ref):
    c = CHUNK
    nd = RWKV_DIM
    hd = HEAD_DIM
    nh = RWKV_HEADS
    nb = za_ref.shape[0]
    ones_bd = _seg_ones(nd, hd)
    eye4 = _iota((hd, nd), 0) == (_iota((hd, nd), 1) % hd)
    eye4_f = eye4.astype(F32)
    row2 = _iota((2 * c, 2 * nd), 0)
    rr = row2 % c
    cc = _iota((2 * c, 2 * nd), 1) % c
    diag_ok = jnp.logical_and(cc == rr, row2 >= c)
    keep = _diag_mask(nd, nd, hd, hd)
    bd = lambda x: _block_diag(x, nh, hd, hd, keep)
    units, pr, rhs, vb, pm, rm, qe, ke, gam, gmask = ([] for _ in range(10))
    for rb in range(nb):
        za = za_ref[rb].astype(F32)
        r = za[:, 0:nd]
        k = za[:, nd:2 * nd]
        v = za[:, 2 * nd:3 * nd]
        kkr = k * kk_ref[...]
        kk = kkr / jnp.maximum(jnp.sqrt(_segsum(kkr * kkr, ones_bd)), 1e-12)
        bonus_ref[rb] = (_segsum(r * k * rk_ref[...], ones_bd) * v).astype(bonus_ref.dtype)
        gd = za[:, 3 * nd + 2 * RWKV_W_LORA + 2 * RWKV_A_LORA:]
        gate_ref[rb] = _dot1(_sigmoid(gd), g2_ref[...]).astype(gate_ref.dtype)
        v_b = v.astype(BF16)
        for d in range(2):
            rev = d == 1
            wo = 3 * nd + d * RWKV_W_LORA
            ao = 3 * nd + 2 * RWKV_W_LORA + d * RWKV_A_LORA
            w_lo = _dot1(jnp.tanh(za[:, wo:wo + RWKV_W_LORA]), w2_ref[d])
            log_w = -_softplus(-(w0_ref[d] + w_lo)) - 0.5
            logdec = -jnp.exp(log_w)
            a = _sigmoid(a0_ref[d] + _dot1(za[:, ao:ao + RWKV_A_LORA], a2_ref[d]))
            kd = k * (1.0 + (a - 1.0) * ka_ref[...])
            bvec = kk * a
            ld_hi, ld_lo = _split(logdec)
            tri = _incl_mask(c, rev).astype(BF16)
            cum = _dg(tri, ld_hi, _NN) + _dg(tri, ld_lo, _NN)
            total = cum[0:1, :] if rev else cum[c - 1:c, :]
            e_neg = jnp.exp(-cum)
            e_end = jnp.exp(total - cum)
            pm_d = kk * jnp.exp(cum - logdec)
            rm_d = r * jnp.exp(cum)
            units.append((rb, d))
            pm.append(pm_d)
            rm.append(rm_d)
            pr.append(jnp.concatenate([pm_d, rm_d], axis=0).astype(BF16))
            rhs.append(jnp.concatenate([bd((bvec * e_neg).astype(BF16)),
                                        bd((kd * e_neg).astype(BF16))], axis=0))
            vb.append(v_b)
            qe.append((bvec * e_end).astype(BF16))
            ke.append((kd * e_end).astype(BF16))
            gam.append(jnp.exp(total))
            gmask.append(jnp.logical_or(cc > rr if rev else cc < rr, diag_ok))
    n = len(units)
    a4 = [jnp.where(gmask[i], _dg(pr[i], rhs[i], _NT), 0.0) for i in range(n)]
    a4b = [x.astype(BF16) for x in a4]
    pw = [_dg(a4b[i][:c, :nd], bd(a4b[i][:c, :nd]), _NN) for i in range(n)]
    lv = [_dg(a4b[i][:, nd:], bd(vb[i]), _NN) for i in range(n)]
    tinv = [eye4_f - a4[i][:c, :nd] for i in range(n)]
    covered = 2
    while True:
        pwd = [bd(x.astype(BF16)) for x in pw]
        tinv = [tinv[i] + _dg(tinv[i].astype(BF16), pwd[i], _NN) for i in range(n)]
        covered *= 2
        if covered >= c:
            break
        pw = [_dg(pw[i].astype(BF16), pwd[i], _NN) for i in range(n)]
    wz = [_dg(tinv[i].astype(BF16),
              jnp.concatenate([bd(pm[i].astype(BF16)), bd(lv[i][:c].astype(BF16))], axis=1),
              _NN).astype(BF16) for i in range(n)]
    awz = [_dg(a4b[i][c:, :nd],
               jnp.concatenate([bd(wz[i][:, :nd]), bd(wz[i][:, nd:])], axis=1), _NN)
           for i in range(n)]
    eye = _iota((hd, hd), 0) == _iota((hd, hd), 1)
    for i, (rb, d) in enumerate(units):
        g_ref[d, rb] = (rm[i] - awz[i][:, :nd]).astype(g_ref.dtype)
        yl_ref[d, rb] = (lv[i][c:] - awz[i][:, nd:]).astype(yl_ref.dtype)
        ats, bss = [], []
        for h in range(nh):
            sl = slice(h * hd, (h + 1) * hd)
            wz_h = jnp.concatenate([wz[i][:, sl], wz[i][:, nd + h * hd:nd + (h + 1) * hd]], axis=1)
            qwz = _dg(qe[i][:, sl], wz_h, _TN)
            ats.append(jnp.where(eye, jnp.broadcast_to(gam[i][:, sl], (hd, hd)), 0.0)
                       - qwz[:, :hd])
            bss.append(_dg(ke[i][:, sl], vb[i][:, sl], _TN) - qwz[:, hd:])
        at_ref[d, rb] = jnp.concatenate(ats, axis=1).astype(at_ref.dtype)
        bs_ref[d, rb] = jnp.concatenate(bss, axis=1).astype(bs_ref.dtype)


def _rwkv_local(za, k_k, k_a, r_k, w0, w2, a0, a2, g2, *, nb):
    b, l, _ = za.shape
    c = CHUNK
    nd = RWKV_DIM
    full = lambda shape: pl.BlockSpec(shape, lambda i, j: (0,) * len(shape))
    dir_spec = pl.BlockSpec((2, nb, c, nd), lambda i, j: (0, i, j, 0))
    dir_shape = jax.ShapeDtypeStruct((2, b, l, nd), ACT)
    row_spec = pl.BlockSpec((nb, c, nd), lambda i, j: (i, j, 0))
    row_shape = jax.ShapeDtypeStruct((b, l, nd), ACT)
    return pl.pallas_call(
        _rwkv_local_kernel,
        grid=(b // nb, l // c),
        in_specs=[
            pl.BlockSpec((nb, c, RWKV_COLS), lambda i, j: (i, j, 0)),
            full((1, nd)), full((1, nd)), full((1, nd)),
            full((2, 1, nd)), full((2, RWKV_W_LORA, nd)),
            full((2, 1, nd)), full((2, RWKV_A_LORA, nd)),
            full((RWKV_G_LORA, nd)),
        ],
        out_specs=[dir_spec] * 4 + [row_spec] * 2,
        out_shape=[dir_shape] * 4 + [row_shape] * 2,
        compiler_params=pltpu.CompilerParams(
            dimension_semantics=("arbitrary", "arbitrary"), vmem_limit_bytes=VMEM_LIMIT),
        name="rwkv_local",
    )(za, k_k.reshape(1, -1), k_a.reshape(1, -1), r_k.reshape(1, -1), w0.reshape(2, 1, nd),
      w2, a0.reshape(2, 1, nd), a2, g2)


def _rwkv_seq_kernel(gf_ref, ylf_ref, atf_ref, bsf_ref, gr_ref, ylr_ref, atr_ref, bsr_ref,
                     yf_ref, yr_ref, m_ref):
    nb = yf_ref.shape[0]

    @pl.when(pl.program_id(1) == 0)
    def _():
        m_ref[...] = jnp.zeros_like(m_ref)

    dirs = ((gf_ref, ylf_ref, atf_ref, bsf_ref, yf_ref), (gr_ref, ylr_ref, atr_ref, bsr_ref, yr_ref))
    units = [(rb, d) for rb in range(nb) for d in range(2)]
    keep = _diag_mask(RWKV_DIM, RWKV_DIM, HEAD_DIM, HEAD_DIM)
    m_bd = [_block_diag(m_ref[u].astype(BF16), RWKV_HEADS, HEAD_DIM, HEAD_DIM, keep)
            for u in range(len(units))]
    y = [_dg(dirs[d][0][0, rb], m_bd[u], _NN) for u, (rb, d) in enumerate(units)]
    m_new = [_dg(dirs[d][2][0, rb], m_bd[u], _NN) for u, (rb, d) in enumerate(units)]
    for u, (rb, d) in enumerate(units):
        dirs[d][4][rb] = (y[u] + dirs[d][1][0, rb].astype(F32)).astype(dirs[d][4].dtype)
        m_ref[u] = m_new[u] + dirs[d][3][0, rb].astype(F32)


def _rwkv_seq(g, yl, at, bs, *, ctx_len, nb):
    _, b, l, nd = g.shape
    c = CHUNK
    n_chunks = l // c
    n_ctx_chunks = ctx_len // c
    rev_idx = lambda j: _chunk_index(j, n_ctx_chunks, n_chunks, True)
    fwd = pl.BlockSpec((1, nb, c, nd), lambda i, j: (0, i, j, 0))
    bwd = pl.BlockSpec((1, nb, c, nd), lambda i, j: (1, i, rev_idx(j), 0))
    y_shape = jax.ShapeDtypeStruct((b, l, nd), ACT)
    return pl.pallas_call(
        _rwkv_seq_kernel,
        grid=(b // nb, n_chunks),
        in_specs=[fwd] * 4 + [bwd] * 4,
        out_specs=[pl.BlockSpec((nb, c, nd), lambda i, j: (i, j, 0)),
                   pl.BlockSpec((nb, c, nd), lambda i, j: (i, rev_idx(j), 0))],
        out_shape=[y_shape, y_shape],
        scratch_shapes=[pltpu.VMEM((nb * 2, HEAD_DIM, RWKV_DIM), F32)],
        compiler_params=pltpu.CompilerParams(
            dimension_semantics=("arbitrary", "arbitrary"), vmem_limit_bytes=VMEM_LIMIT),
        name="rwkv_seq",
    )(g, yl, at, bs, g, yl, at, bs)


def _cummax_rows(x, rev):
    n = x.shape[0]
    row = _iota(x.shape, 0)
    shift = 1
    while shift < n:
        if rev:
            moved = jnp.where(row >= n - shift, -jnp.inf, pltpu.roll(x, n - shift, 0))
        else:
            moved = jnp.where(row < shift, -jnp.inf, pltpu.roll(x, shift, 0))
        x = jnp.maximum(x, moved)
        shift *= 2
    return x


def _head_lane(d):
    return d * 2 * MLSTM_HEADS + MLSTM_HEADS


def _sel_expand(d, width):
    n = MLSTM_HEADS * width
    return (_iota((LANES, n), 0) == _head_lane(d) + _iota((LANES, n), 1) // width).astype(BF16)


def _sel_reduce(d, width):
    n = MLSTM_HEADS * width
    return (_iota((n, LANES), 1) == _head_lane(d) + _iota((n, LANES), 0) // width).astype(BF16)


def _diag_mask(rows, cols, row_block, col_block):
    return (_iota((rows, cols), 0) // row_block) == (_iota((rows, cols), 1) // col_block)


def _block_diag(x, reps, row_block, col_block, keep=None):
    t = jnp.concatenate([x] * reps, axis=0)
    if keep is None:
        keep = _diag_mask(t.shape[0], t.shape[1], row_block, col_block)
    return jnp.where(keep, t, jnp.zeros_like(t))


def _mlstm_local_kernel(zm_ref, zg_ref, gb_ref, num_ref, st_ref, cl_ref, rows_ref):
    c = CHUNK
    nh = MLSTM_HEADS
    dk = MLSTM_QK
    dv = MLSTM_V
    nb = zm_ref.shape[0]
    lane_j = _iota((c, nh * c), 1) % c
    row_s = _iota((c, nh * c), 0)
    lane_g = _iota((c, LANES), 1)
    cl_keep = _diag_mask(nh * dk, MLSTM_DIM, dk, dv)
    k_keep = _diag_mask(nh * c, nh * dk, c, dk)
    v_keep = _diag_mask(nh * c, MLSTM_DIM, c, dv)
    sel_c = [_sel_expand(d, c) for d in range(2)]
    sel_k = [_sel_expand(d, dk) for d in range(2)]
    sel_r = [_sel_reduce(d, c) for d in range(2)]
    k, vb, log_f, li, qk, vbd = [], [], [], [], [], []
    for rb in range(nb):
        zm = zm_ref[rb]
        qb = (zm[:, :nh * dk].astype(F32) * (dk ** -0.5)).astype(BF16)
        k.append(zm[:, nh * dk:MLSTM_QKW].astype(F32))
        vb.append(zm[:, MLSTM_QKW:MLSTM_QKW + MLSTM_DIM].astype(BF16))
        gates = zg_ref[rb] + gb_ref[...]
        capped = GATE_SOFTCAP * jnp.tanh(gates * (1.0 / GATE_SOFTCAP))
        log_f.append(-_softplus(-capped))
        li.append(pltpu.roll(capped, nh, 1))
        qk.append(_dg(qb, _block_diag(k[rb].astype(BF16), nh, c, dk, k_keep), _NT))
        vbd.append(_block_diag(vb[rb], nh, c, dv, v_keep))
    units = [(rb, d) for rb in range(nb) for d in range(2)]
    b_all = [_dot_exact_lhs(_incl_mask(c, d == 1).astype(BF16), log_f[rb]) for rb, d in units]
    x = [li[rb] - b_all[u] for u, (rb, d) in enumerate(units)]
    cm = [_cummax_rows(x[u], d == 1) for u, (rb, d) in enumerate(units)]
    e1 = [_dot_exact_rhs(-cm[u], sel_c[d]) for u, (rb, d) in enumerate(units)]
    xe = [_dot_exact_rhs(x[u], sel_c[d]) for u, (rb, d) in enumerate(units)]
    g_row, m_loc, e32 = [], [], []
    for u, (rb, d) in enumerate(units):
        last = 0 if d == 1 else c - 1
        g_row.append(b_all[u][last:last + 1, :])
        w_end = g_row[u] - b_all[u] + li[rb]
        m_loc.append(jnp.max(w_end, axis=0, keepdims=True))
        e32.append(_dot2_rhs01(jnp.exp(w_end - m_loc[u]), sel_k[d]))
    wi = []
    for u, (rb, d) in enumerate(units):
        x_row = jnp.sum(jnp.where(row_s == lane_j, xe[u], 0.0), axis=0, keepdims=True)
        earlier = (lane_j >= row_s) if d == 1 else (lane_j <= row_s)
        wi.append((qk[rb] * jnp.exp(jnp.where(earlier, e1[u] + x_row, -jnp.inf))).astype(BF16))
    nd = [_dg(wi[u], jnp.concatenate([vbd[rb], sel_r[d]], axis=1), _NN)
          for u, (rb, d) in enumerate(units)]
    ke = [k[rb] * e32[u] for u, (rb, d) in enumerate(units)]
    full = [_dg(ke[u].astype(BF16), vb[rb], _TN) for u, (rb, d) in enumerate(units)]
    for u, (rb, d) in enumerate(units):
        num_ref[d, rb] = nd[u][:, :MLSTM_DIM].astype(num_ref.dtype)
        on_heads = jnp.logical_and(lane_g >= _head_lane(d), lane_g < _head_lane(d) + nh)
        st_ref[d, rb] = (nd[u][:, MLSTM_DIM:]
                         + pltpu.roll(jnp.where(on_heads, b_all[u] + cm[u], 0.0), STAT_SHIFT, 1)
                         + pltpu.roll(jnp.where(on_heads, b_all[u], 0.0), 2 * STAT_SHIFT, 1))
        f = jnp.where(cl_keep, full[u], 0.0)
        cl_ref[d, rb] = ((f[0:dk] + f[dk:2 * dk])
                         + (f[2 * dk:3 * dk] + f[3 * dk:4 * dk])).astype(cl_ref.dtype)
        rows_ref[d, rb] = jnp.concatenate(
            [jnp.sum(ke[u], axis=0, keepdims=True), m_loc[u], g_row[u],
             jnp.zeros((SUBLANES - 3, LANES), F32)], axis=0)


def _mlstm_local(zm, zg, gate_bias, *, nb):
    b, l, _ = zm.shape
    c = CHUNK
    n_chunks = l // c
    spec = lambda rows, w, dt: pl.BlockSpec((2, nb, rows, w), lambda i, j: (0, i, j, 0))
    shape = lambda rows, w, dt: jax.ShapeDtypeStruct((2, b, n_chunks * rows, w), dt)
    outs = [(c, MLSTM_DIM, ACT), (c, LANES, F32), (MLSTM_QK, MLSTM_DIM, ACT),
            (SUBLANES, LANES, F32)]
    return pl.pallas_call(
        _mlstm_local_kernel,
        grid=(b // nb, n_chunks),
        in_specs=[
            pl.BlockSpec((nb, c, ZM_COLS), lambda i, j: (i, j, 0)),
            pl.BlockSpec((nb, c, LANES), lambda i, j: (i, j, 0)),
            pl.BlockSpec((1, LANES), lambda i, j: (0, 0)),
        ],
        out_specs=[spec(*o) for o in outs],
        out_shape=[shape(*o) for o in outs],
        compiler_params=pltpu.CompilerParams(
            dimension_semantics=("arbitrary", "arbitrary"), vmem_limit_bytes=VMEM_LIMIT),
        name="mlstm_local",
    )(zm, zg, gate_bias)


def _mlstm_seq_kernel(qf_ref, numf_ref, stf_ref, clf_ref, rowsf_ref,
                      qr_ref, numr_ref, str_ref, clr_ref, rowsr_ref,
                      hf_ref, hr_ref, cbd_ref, n_ref, m_ref):
    c = CHUNK
    nh = MLSTM_HEADS
    dk = MLSTM_QK
    dv = MLSTM_V
    nb = hf_ref.shape[0]

    @pl.when(pl.program_id(1) == 0)
    def _():
        cbd_ref[...] = jnp.zeros_like(cbd_ref)
        n_ref[...] = jnp.zeros_like(n_ref)
        m_ref[...] = jnp.zeros_like(m_ref)

    dirs = ((qf_ref, numf_ref, stf_ref, clf_ref, rowsf_ref, hf_ref),
            (qr_ref, numr_ref, str_ref, clr_ref, rowsr_ref, hr_ref))
    units = [(rb, d) for rb in range(nb) for d in range(2)]
    nu = len(units)
    cl_keep = _diag_mask(nh * dk, MLSTM_DIM, dk, dv)
    sel_v = [_sel_expand(d, dv) for d in range(2)]
    sel_k = [_sel_expand(d, dk) for d in range(2)]
    sel_r = [_sel_reduce(d, dk) for d in range(2)]
    q = [dirs[d][0][rb].astype(F32) * (dk ** -0.5) for rb, d in units]
    rows = [dirs[d][4][0, rb] for rb, d in units]
    m_row = [m_ref[u] for u in range(nu)]
    n_row = [n_ref[u] for u in range(nu)]
    cbd = [cbd_ref[u] for u in range(nu)]
    qn = [_dot2_rhs01(q[u] * n_row[u], sel_r[d]) for u, (rb, d) in enumerate(units)]
    qc = [_dg(q[u].astype(BF16), cbd[u].astype(BF16), _NN) for u in range(nu)]
    a12, srow = [], []
    for u, (rb, d) in enumerate(units):
        stats = dirs[d][2][0, rb]
        mi = pltpu.roll(stats, LANES - STAT_SHIFT, 1)
        log_inter = pltpu.roll(stats, LANES - 2 * STAT_SHIFT, 1) + m_row[u]
        m_out = jnp.maximum(log_inter, mi)
        s_intra = jnp.exp(mi - m_out)
        s_inter = jnp.exp(log_inter - m_out)
        den = s_intra * stats + s_inter * qn[u]
        dinv = 1.0 / jnp.maximum(jnp.abs(den), jnp.exp(-m_out))
        a12.append(jnp.concatenate([s_intra * dinv, s_inter * dinv], axis=0))
        m_loc, g_row = rows[u][1:2], rows[u][2:3]
        m_new = jnp.maximum(g_row + m_row[u], m_loc)
        srow.append(jnp.concatenate(
            [jnp.exp(g_row + m_row[u] - m_new), jnp.exp(m_loc - m_new),
             jnp.zeros((SUBLANES - 2, LANES), F32)], axis=0))
        m_ref[u] = m_new
    a12e = [_dot2_rhs01(a12[u], sel_v[d]) for u, (rb, d) in enumerate(units)]
    s_v = [_dot2_rhs01(srow[u], sel_v[d]) for u, (rb, d) in enumerate(units)]
    s_k = [_dot2_rhs01(srow[u], sel_k[d]) for u, (rb, d) in enumerate(units)]
    for u, (rb, d) in enumerate(units):
        h_out = a12e[u][:c] * dirs[d][1][0, rb].astype(F32) + a12e[u][c:] * qc[u]
        dirs[d][5][rb] = h_out.astype(dirs[d][5].dtype)
        cl_full = _block_diag(dirs[d][3][0, rb].astype(F32), nh, dk, dv, cl_keep)
        cbd_ref[u] = s_v[u][0:1] * cbd[u] + s_v[u][1:2] * cl_full
        n_ref[u] = s_k[u][0:1] * n_row[u] + s_k[u][1:2] * rows[u][0:1]


def _mlstm_seq(zm, num, stats, cl, rows, *, ctx_len, nb):
    b, l, _ = zm.shape
    c = CHUNK
    n_chunks = l // c
    n_ctx_chunks = ctx_len // c
    rev_idx = lambda j: _chunk_index(j, n_ctx_chunks, n_chunks, True)

    def specs(d):
        cidx = (lambda j: j) if d == 0 else rev_idx
        blk = lambda r, w: pl.BlockSpec((1, nb, r, w), lambda i, j: (d, i, cidx(j), 0))
        return [pl.BlockSpec((nb, c, MLSTM_HEADS * MLSTM_QK), lambda i, j: (i, cidx(j), 0)),
                blk(c, MLSTM_DIM), blk(c, LANES), blk(MLSTM_QK, MLSTM_DIM),
                blk(SUBLANES, LANES)]

    h_shape = jax.ShapeDtypeStruct((b, l, MLSTM_DIM), ACT)
    args = (zm, num, stats, cl, rows)
    return pl.pallas_call(
        _mlstm_seq_kernel,
        grid=(b // nb, n_chunks),
        in_specs=specs(0) + specs(1),
        out_specs=[pl.BlockSpec((nb, c, MLSTM_DIM), lambda i, j: (i, j, 0)),
                   pl.BlockSpec((nb, c, MLSTM_DIM), lambda i, j: (i, rev_idx(j), 0))],
        out_shape=[h_shape, h_shape],
        scratch_shapes=[
            pltpu.VMEM((2 * nb, MLSTM_HEADS * MLSTM_QK, MLSTM_DIM), F32),
            pltpu.VMEM((2 * nb, 1, LANES), F32),
            pltpu.VMEM((2 * nb, 1, LANES), F32),
        ],
        compiler_params=pltpu.CompilerParams(
            dimension_semantics=("arbitrary", "arbitrary"), vmem_limit_bytes=VMEM_LIMIT),
        name="mlstm_seq",
    )(*args, *args)


def _rope_swap(x):
    lane = _iota(x.shape, 1)
    half = MLA_ROPE // 2
    return jnp.where(lane < MLA_NOPE + half, pltpu.roll(x, LANES - half, 1),
                     pltpu.roll(x, half, 1))


def _mla_proj_kernel(zb_ref, cq_ref, sq_ref, ck_ref, sk_ref, qg_ref, kvg_ref,
                     wq_ref, wk_ref, wv_ref, q_ref, k_ref, v_ref):
    hq = MLA_HEADS * QPAD
    zb = zb_ref[0].astype(F32)
    cq = zb[:, :MLA_Q_RANK]
    ckv = zb[:, MLA_Q_RANK:MLA_Q_RANK + MLA_KV_RANK]
    kr = zb[:, MLA_Q_RANK + MLA_KV_RANK:]
    cqn = cq * lax.rsqrt(jnp.mean(cq * cq, axis=-1, keepdims=True) + NORM_EPS) * qg_ref[...]
    ckn = ckv * lax.rsqrt(jnp.mean(ckv * ckv, axis=-1, keepdims=True) + NORM_EPS) * kvg_ref[...]
    q_all = _dot1(cqn, wq_ref[...])
    k_all = _dot1(ckn, wk_ref[...])
    v_ref[0] = _dg(wv_ref[...], ckn.astype(BF16), _NT).astype(BF16)
    cos_q, sin_q, cos_k = cq_ref[...], sq_ref[...], ck_ref[...]
    kr_rot = _rope_swap(kr) * sk_ref[...]
    inv_dim = 1.0 / MLA_QK
    for h in range(MLA_HEADS):
        sl = slice(h * QPAD, (h + 1) * QPAD)
        qh = q_all[:, sl]
        q_inv = lax.rsqrt(jnp.sum(qh * qh, axis=-1, keepdims=True) * inv_dim + NORM_EPS)
        q_ref[0, :, sl] = ((qh * cos_q + q_all[:, hq + h * QPAD:hq + (h + 1) * QPAD] * sin_q)
                           * q_inv).astype(BF16)
        kh = k_all[:, sl] + kr
        k_inv = lax.rsqrt(jnp.sum(kh * kh, axis=-1, keepdims=True) * inv_dim + NORM_EPS)
        k_ref[0, :, sl] = ((kh * cos_k + kr_rot) * k_inv).astype(BF16)


def _mla_proj(zb, tables, q_norm_g, kv_norm_g, wq_r, wk_r, wv_r, *, tm):
    b, l, _ = zb.shape
    hq = MLA_HEADS * QPAD
    full = lambda shape: pl.BlockSpec(shape, lambda i, j: (0,) * len(shape))
    table = pl.BlockSpec((tm, QPAD), lambda i, j: (j, 0))
    return pl.pallas_call(
        _mla_proj_kernel,
        grid=(b, l // tm),
        in_specs=[
            pl.BlockSpec((1, tm, ZB_COLS), lambda i, j: (i, j, 0)),
            table, table, table, table,
            full((1, MLA_Q_RANK)), full((1, MLA_KV_RANK)),
            full((MLA_Q_RANK, 2 * hq)), full((MLA_KV_RANK, hq)), full((MLA_DIM, MLA_KV_RANK)),
        ],
        out_specs=[
            pl.BlockSpec((1, tm, hq), lambda i, j: (i, j, 0)),
            pl.BlockSpec((1, tm, hq), lambda i, j: (i, j, 0)),
            pl.BlockSpec((1, MLA_DIM, tm), lambda i, j: (i, 0, j)),
        ],
        out_shape=[
            jax.ShapeDtypeStruct((b, l, hq), BF16),
            jax.ShapeDtypeStruct((b, l, hq), BF16),
            jax.ShapeDtypeStruct((b, MLA_DIM, l), BF16),
        ],
        compiler_params=pltpu.CompilerParams(
            dimension_semantics=("arbitrary", "arbitrary"), vmem_limit_bytes=VMEM_LIMIT),
        name="mla_proj",
    )(zb, *tables, q_norm_g.reshape(1, -1), kv_norm_g.reshape(1, -1), wq_r, wk_r, wv_r)


def _attend_tile(q_ref, k_ref, vt_ref, o_ref, n_keys):
    tq = q_ref.shape[1]
    starts = list(range(0, n_keys, KV_BLOCK))
    bounds = [(s0, min(s0 + KV_BLOCK, n_keys)) for s0 in starts]
    n_blocks = len(bounds)
    heads = range(ATTN_HEADS_PER_STEP)
    q = [q_ref[0, :, h * QPAD:(h + 1) * QPAD] for h in heads]

    def scores(j):
        lo, hi = bounds[j]
        return [_dg(k_ref[0, lo:hi, h * QPAD:(h + 1) * QPAD], q[h], _NT)
                for h in heads]

    m = [jnp.full((1, tq), -jnp.inf, F32) for _ in heads]
    acc = [jnp.zeros((MLA_V + 2 * SUBLANES, tq), F32) for _ in heads]
    pending = [scores(j) for j in range(min(SCORE_LOOKAHEAD, n_blocks))]
    for j in range(n_blocks):
        if j + SCORE_LOOKAHEAD < n_blocks:
            pending.append(scores(j + SCORE_LOOKAHEAD))
        s = pending.pop(0)
        lo, hi = bounds[j]
        ones_rows = jnp.ones((2 * SUBLANES, hi - lo), BF16)
        for h in heads:
            vt = jnp.concatenate([vt_ref[0, h * MLA_V:(h + 1) * MLA_V, lo:hi], ones_rows], axis=0)
            m_new = jnp.maximum(m[h], jnp.max(s[h], axis=0, keepdims=True))
            p = jnp.exp2(s[h] - m_new).astype(BF16)
            acc[h] = jnp.exp2(m[h] - m_new) * acc[h] + _dg(vt, p, _NN)
            m[h] = m_new
    out_t = jnp.concatenate([acc[h][:MLA_V] / acc[h][MLA_V:MLA_V + 1] for h in heads],
                            axis=0)
    o_ref[0] = out_t.T.astype(o_ref.dtype)


def _mla_attn_kernel(q_ref, k_ref, v_ref, o_ref, *, ctx_len, first_tile, n_ctx_tiles):
    n_all = k_ref.shape[1]
    if first_tile >= n_ctx_tiles:
        _attend_tile(q_ref, k_ref, v_ref, o_ref, n_all)
        return
    is_ctx = (pl.program_id(2) + first_tile) < n_ctx_tiles

    @pl.when(is_ctx)
    def _():
        _attend_tile(q_ref, k_ref, v_ref, o_ref, ctx_len)

    @pl.when(jnp.logical_not(is_ctx))
    def _():
        _attend_tile(q_ref, k_ref, v_ref, o_ref, n_all)


def _mla_attn(q, k, v, *, tq, ctx_len, first_tile):
    b, l, _ = q.shape
    assert l % LANES == 0 and ctx_len % LANES == 0
    n_q = l // tq - first_tile
    hps = ATTN_HEADS_PER_STEP
    kern = functools.partial(_mla_attn_kernel, ctx_len=ctx_len, first_tile=first_tile,
                             n_ctx_tiles=ctx_len // tq)
    return pl.pallas_call(
        kern,
        grid=(b, MLA_HEADS // hps, n_q),
        in_specs=[
            pl.BlockSpec((1, tq, hps * QPAD), lambda i, p, j: (i, j + first_tile, p)),
            pl.BlockSpec((1, l, hps * QPAD), lambda i, p, j: (i, 0, p)),
            pl.BlockSpec((1, hps * MLA_V, l), lambda i, p, j: (i, p, 0)),
        ],
        out_specs=pl.BlockSpec((1, tq, hps * MLA_V), lambda i, p, j: (i, j + first_tile, p)),
        out_shape=jax.ShapeDtypeStruct((b, l, MLA_DIM), ACT),
        compiler_params=pltpu.CompilerParams(
            dimension_semantics=("arbitrary", "arbitrary", "arbitrary"),
            vmem_limit_bytes=VMEM_LIMIT),
        name="mla_attn",
    )(q, k, v)


def _mix_ffn_kernel(x_ref, yf_ref, yr_ref, bonus_ref, gate_ref, attn_ref, hf_ref, hr_ref,
                    o_ref, mod_ref, lng_ref, lnb_ref, mng_ref, wa_ref, wb_ref, wm_ref,
                    g2_ref, w1_ref, w2_ref, out_ref, *, hidden):
    ones_bd = _seg_ones(RWKV_DIM, HEAD_DIM)
    inv = 1.0 / HEAD_DIM
    y = yf_ref[0].astype(F32) + yr_ref[0].astype(F32)
    mean = _segsum(y, ones_bd) * inv
    yc = y - mean
    var = _segsum(yc * yc, ones_bd) * inv
    yn = yc * lax.rsqrt(var + RWKV_GN_EPS) * lng_ref[...] + lnb_ref[...]
    ya = (yn + bonus_ref[0].astype(F32)) * gate_ref[0].astype(F32)
    hm = hf_ref[0].astype(F32) + hr_ref[0].astype(F32)
    hn = hm * lax.rsqrt(_segsum(hm * hm, ones_bd) * inv + NORM_EPS)
    ym = hn * mng_ref[...] * _sigmoid(o_ref[0].astype(F32))
    mix = (_dot1(ya, wa_ref[...]) + _dot1(attn_ref[0], wb_ref[...]) + _dot1(ym, wm_ref[...]))
    x = x_ref[0] + mod_ref[0, 0, 0:1, :] * mix
    h = _rmsnorm_mod(x, g2_ref[...], mod_ref[0, 0, 1:2, :], mod_ref[0, 0, 2:3, :]).astype(BF16)
    gu = _dg(h, w1_ref[...], _NN)
    act = (_silu(gu[:, :hidden]) * gu[:, hidden:]).astype(BF16)
    out_ref[0] = x + mod_ref[0, 0, 3:4, :] * _dg(act, w2_ref[...], _NN)


def _mix_ffn(x, yf, yr, bonus, gate, attn, hf, hr, zm, modsel, ln_g, ln_b, mnorm_g,
             wa, wb, wm, norm2_g, w1, w2, *, tm, ctx_len, first_tile):
    b, l, d = x.shape
    hidden = w2.shape[0]
    n_ctx_tiles = ctx_len // tm
    n_tiles = l // tm - first_tile
    row = lambda w: pl.BlockSpec((1, tm, w), lambda i, j: (i, j + first_tile, 0))
    full = lambda shape: pl.BlockSpec(shape, lambda i, j: (0,) * len(shape))
    seg = lambda j: jnp.where(j + first_tile >= n_ctx_tiles, 1, 0)
    o_block = (MLSTM_QKW + MLSTM_DIM) // MLSTM_DIM
    return pl.pallas_call(
        functools.partial(_mix_ffn_kernel, hidden=hidden),
        grid=(b, n_tiles),
        in_specs=[
            row(d), row(RWKV_DIM), row(RWKV_DIM), row(RWKV_DIM), row(RWKV_DIM), row(MLA_DIM),
            row(MLSTM_DIM), row(MLSTM_DIM),
            pl.BlockSpec((1, tm, MLSTM_DIM), lambda i, j: (i, j + first_tile, o_block)),
            pl.BlockSpec((1, 1, 4, d), lambda i, j: (i, seg(j), 0, 0)),
            full((1, RWKV_DIM)), full((1, RWKV_DIM)), full((1, MLSTM_DIM)),
            full((RWKV_DIM, d)), full((MLA_DIM, d)), full((MLSTM_DIM, d)),
            full((1, d)), full((d, 2 * hidden)), full((hidden, d)),
        ],
        out_specs=pl.BlockSpec((1, tm, d), lambda i, j: (i, j, 0)),
        out_shape=jax.ShapeDtypeStruct((b, n_tiles * tm, d), F32),
        compiler_params=pltpu.CompilerParams(
            dimension_semantics=("arbitrary", "arbitrary"), vmem_limit_bytes=VMEM_LIMIT),
        name="mix_ffn",
    )(x, yf, yr, bonus, gate, attn, hf, hr, zm, modsel, ln_g.reshape(1, -1),
      ln_b.reshape(1, -1), mnorm_g.reshape(1, -1), wa, wb, wm, norm2_g.reshape(1, d), w1, w2)


def _arrange_w_in(w_in):
    d = w_in.shape[0]
    wa = w_in[:, :RWKV_COLS]
    wb = w_in[:, RWKV_COLS:RWKV_COLS + MLA_COLS]
    wm = w_in[:, RWKV_COLS + MLA_COLS:]
    gates = wm[:, MLSTM_QKW + 2 * MLSTM_DIM:]
    gate_blk = jnp.pad(gates, ((0, 0), (0, LANES - gates.shape[1])))
    kr = wb[:, MLA_Q_RANK + MLA_KV_RANK:]
    kr_blk = jnp.pad(kr, ((0, 0), (MLA_NOPE, LANES - MLA_NOPE - MLA_ROPE)))
    out = jnp.concatenate([wa, wm[:, :MLSTM_QKW + 2 * MLSTM_DIM], gate_blk,
                           wb[:, :MLA_Q_RANK + MLA_KV_RANK], kr_blk], axis=1)
    assert out.shape == (d, IN_COLS_PAD)
    return out.astype(BF16)


def _arrange_mla_weights(w_uq, w_ukv):
    rq = w_uq.shape[0]
    wq = w_uq.reshape(rq, MLA_HEADS, MLA_QK)
    wq = jnp.pad(wq, ((0, 0), (0, 0), (0, QPAD - MLA_QK)))
    half = MLA_ROPE // 2
    wq_swap = jnp.concatenate(
        [jnp.zeros_like(wq[:, :, :MLA_NOPE]), wq[:, :, MLA_NOPE + half:MLA_QK],
         wq[:, :, MLA_NOPE:MLA_NOPE + half], jnp.zeros_like(wq[:, :, MLA_QK:])], axis=2)
    wq = jnp.concatenate([wq.reshape(rq, -1), wq_swap.reshape(rq, -1)], axis=1)
    rk = w_ukv.shape[0]
    wkv = w_ukv.reshape(rk, MLA_HEADS, MLA_NOPE + MLA_V)
    wk = jnp.pad(wkv[:, :, :MLA_NOPE], ((0, 0), (0, 0), (0, QPAD - MLA_NOPE)))
    wk = wk.reshape(rk, MLA_HEADS * QPAD)
    wv_t = wkv[:, :, MLA_NOPE:].reshape(rk, MLA_DIM).T
    return wq.astype(BF16), wk.astype(BF16), wv_t.astype(BF16)


def _rope_tables(seq_len, ctx_len):
    rows = seq_len // GRID_W
    row = jnp.repeat(jnp.arange(rows, dtype=F32), GRID_W)
    col = jnp.tile(jnp.arange(GRID_W, dtype=F32), rows)
    n_freq = MLA_ROPE // 4
    inv = jnp.power(ROPE_BASE, -jnp.arange(n_freq, dtype=F32) / n_freq)
    ang = jnp.concatenate([row[:, None] * inv, col[:, None] * inv], axis=-1)
    cos, sin = jnp.cos(ang), jnp.sin(ang)
    pad_l, pad_r = MLA_NOPE, QPAD - MLA_QK
    cos_f = jnp.concatenate([jnp.ones((seq_len, pad_l), F32), cos, cos,
                             jnp.ones((seq_len, pad_r), F32)], axis=1)
    sin_f = jnp.concatenate([jnp.zeros((seq_len, pad_l), F32), sin, sin,
                             jnp.zeros((seq_len, pad_r), F32)], axis=1)
    cos_f = jnp.concatenate([jnp.ones((ctx_len, QPAD), F32), cos_f], axis=0)
    sin_f = jnp.concatenate([jnp.zeros((ctx_len, QPAD), F32), sin_f], axis=0)
    return cos_f, sin_f


def _gained_tables(cos_f, sin_f, gain, scale):
    g = jnp.pad(gain, (0, QPAD - gain.shape[0]))
    half = MLA_ROPE // 2
    lo, mid, hi = MLA_NOPE, MLA_NOPE + half, MLA_QK
    g_swap = jnp.concatenate([g[:lo], g[mid:hi], g[lo:mid], g[hi:]])
    sign = jnp.where(jnp.arange(QPAD) < mid, -1.0, 1.0).astype(F32)
    return cos_f * (g * scale), sin_f * (g_swap * sign * scale)


def _gate_bias(i_b, f_b):
    gb = jnp.stack([i_b, f_b], axis=1).reshape(-1)
    return jnp.pad(gb, (0, LANES - gb.shape[0])).reshape(1, LANES)


def kernel(x, c, ctx, c_ctx, mod_w, mod_b, norm1_g, norm2_g, w_in, w_out, ffn_w_in, ffn_w_out, rwkv_mu, rwkv_w0, rwkv_w2, rwkv_a0, rwkv_a2, rwkv_g2, rwkv_k_k, rwkv_k_a, rwkv_r_k, rwkv_ln_g, rwkv_ln_b, mla_q_norm_g, mla_w_uq, mla_kv_norm_g, mla_w_ukv, mla_q_qknorm_g, mla_k_qknorm_g, mlstm_conv_w, mlstm_conv_b, mlstm_i_b, mlstm_f_b, mlstm_norm_g):
    bsz, seq, d = x.shape
    ctx_len = ctx.shape[1]
    depth = mod_w.shape[0]
    tm = min(256, ctx_len)
    assert ctx_len % tm == 0 and seq % tm == 0 and ctx_len % CHUNK == 0 and seq % CHUNK == 0
    nb = ROWS_PER_STEP if bsz % ROWS_PER_STEP == 0 else 1
    nb_seq = SEQ_ROWS_PER_STEP if bsz % SEQ_ROWS_PER_STEP == 0 else nb

    xu = jnp.concatenate([ctx, x], axis=1)
    n_ctx_tiles = ctx_len // tm

    rows = -(-(bsz + 1) // SUBLANES) * SUBLANES
    cvec = jnp.zeros((rows, d), F32).at[:bsz].set(c).at[bsz].set(c_ctx)
    mod = _modulation(cvec, mod_w, mod_b).reshape(depth, rows, 6, d)
    mod_lat = mod[:, :bsz]
    mod_ctx = jnp.broadcast_to(mod[:, bsz][:, None], mod_lat.shape)
    modsel = jnp.stack([mod_ctx, mod_lat], axis=2)

    cos_f, sin_f = _rope_tables(seq, ctx_len)

    for i in range(depth):
        last = i == depth - 1
        first_tile = n_ctx_tiles if last else 0
        w_r = _arrange_w_in(w_in[i])
        za, zm, zg, zb = _in_proj(xu, modsel[i, :, :, 0:2], norm1_g[i], w_r, rwkv_mu[i],
                              mlstm_conv_w[i], mlstm_conv_b[i], tm=tm, ctx_len=ctx_len)
        g_loc, y_loc, a_tr, b_st, bonus, gate = _rwkv_local(
            za, rwkv_k_k[i], rwkv_k_a[i], rwkv_r_k[i], rwkv_w0[i], rwkv_w2[i], rwkv_a0[i],
            rwkv_a2[i], rwkv_g2[i], nb=nb)
        yf, yr = _rwkv_seq(g_loc, y_loc, a_tr, b_st, ctx_len=ctx_len, nb=nb_seq)
        wq_r, wk_r, wv_r = _arrange_mla_weights(mla_w_uq[i], mla_w_ukv[i])
        tables = (_gained_tables(cos_f, sin_f, mla_q_qknorm_g[i], MLA_QK ** -0.5 * LOG2_E)
                  + _gained_tables(cos_f, sin_f, mla_k_qknorm_g[i], 1.0))
        q, k, v = _mla_proj(zb, tables, mla_q_norm_g[i], mla_kv_norm_g[i],
                            wq_r, wk_r, wv_r, tm=tm)
        attn = _mla_attn(q, k, v, tq=tm, ctx_len=ctx_len, first_tile=first_tile)
        gb = _gate_bias(mlstm_i_b[i], mlstm_f_b[i])
        hf, hr = _mlstm_seq(zm, *_mlstm_local(zm, zg, gb, nb=nb), ctx_len=ctx_len, nb=nb_seq)
        wo = w_out[i].astype(BF16)
        xu = _mix_ffn(xu, yf, yr, bonus, gate, attn, hf, hr, zm, modsel[i, :, :, 2:6],
                      rwkv_ln_g[i], rwkv_ln_b[i], mlstm_norm_g[i],
                      wo[:RWKV_DIM], wo[RWKV_DIM:RWKV_DIM + MLA_DIM], wo[RWKV_DIM + MLA_DIM:],
                      norm2_g[i], ffn_w_in[i].astype(BF16), ffn_w_out[i].astype(BF16),
                      tm=tm, ctx_len=ctx_len, first_tile=first_tile)
    return xu
```

```python
import functools

import jax
import jax.numpy as jnp
from jax import lax
from jax.experimental import pallas as pl
from jax.experimental.pallas import tpu as pltpu

F32 = jnp.float32
BF16 = jnp.bfloat16

HEAD_DIM = 64
NORM_EPS = 1e-6
GRID_W = 64
ROPE_BASE = 10000.0
LOG2_E = 1.4426950408889634
RWKV_HEADS = 4
RWKV_DIM = RWKV_HEADS * HEAD_DIM
RWKV_W_LORA = 64
RWKV_A_LORA = 64
RWKV_G_LORA = 128
RWKV_COLS = 3 * RWKV_DIM + 2 * RWKV_W_LORA + 2 * RWKV_A_LORA + RWKV_G_LORA
RWKV_GN_EPS = 64e-5
MLA_HEADS = 8
MLA_NOPE = 64
MLA_ROPE = 32
MLA_V = HEAD_DIM
MLA_QK = MLA_NOPE + MLA_ROPE
MLA_Q_RANK = 512
MLA_KV_RANK = 256
MLA_DIM = MLA_HEADS * MLA_V
MLA_COLS = MLA_Q_RANK + MLA_KV_RANK + MLA_ROPE
MLSTM_HEADS = 4
MLSTM_QK = 32
MLSTM_V = HEAD_DIM
MLSTM_DIM = MLSTM_HEADS * MLSTM_V
MLSTM_QKW = 2 * MLSTM_HEADS * MLSTM_QK
GATE_SOFTCAP = 15.0
MLSTM_COLS = MLSTM_QKW + 2 * MLSTM_DIM + 4 * MLSTM_HEADS

LANES = 128
SUBLANES = 8
CHUNK = HEAD_DIM
ROWS_PER_STEP = 8
SEQ_ROWS_PER_STEP = 8
KV_BLOCK = 256
STAT_SHIFT = 16
ATTN_HEADS_PER_STEP = 4
SCORE_LOOKAHEAD = 1
QPAD = LANES
ZB_COLS = MLA_Q_RANK + MLA_KV_RANK + LANES
ZM_COLS = MLSTM_QKW + 2 * MLSTM_DIM
SHIFT_COLS = RWKV_COLS + MLSTM_QKW
IN_COLS_PAD = RWKV_COLS + ZM_COLS + LANES + ZB_COLS
ACT = BF16
VMEM_LIMIT = 56 * 1024 * 1024


def _split(a):
    hi = a.astype(BF16)
    lo = (a - hi.astype(F32)).astype(BF16)
    return hi, lo


_NN = (((1,), (0,)), ((), ()))
_NT = (((1,), (1,)), ((), ()))
_TN = (((0,), (0,)), ((), ()))


def _dg(a, b, dims):
    return lax.dot_general(a, b, dims, preferred_element_type=F32)


def _dot1(a, b, dims=_NN):
    return _dg(a.astype(BF16), b.astype(BF16), dims)


def _dot3(a, b, dims=_NN):
    ah, al = _split(a)
    bh, bl = _split(b)
    return _dg(ah, bh, dims) + (_dg(ah, bl, dims) + _dg(al, bh, dims))


def _dot_exact_rhs(a, b01, dims=_NN):
    a1 = a.astype(BF16)
    r1 = a - a1.astype(F32)
    a2 = r1.astype(BF16)
    a3 = (r1 - a2.astype(F32)).astype(BF16)
    return _dg(a1, b01, dims) + (_dg(a2, b01, dims) + _dg(a3, b01, dims))


def _dot_exact_lhs(a01, b, dims=_NN):
    b1 = b.astype(BF16)
    r1 = b - b1.astype(F32)
    b2 = r1.astype(BF16)
    b3 = (r1 - b2.astype(F32)).astype(BF16)
    return _dg(a01, b1, dims) + (_dg(a01, b2, dims) + _dg(a01, b3, dims))


def _dot2_rhs01(a, b01):
    ah, al = _split(a)
    return _dg(ah, b01, _NN) + _dg(al, b01, _NN)


def _iota(shape, dim):
    return lax.broadcasted_iota(jnp.int32, shape, dim)


def _seg_ones(n, seg):
    r = _iota((n, n), 0) // seg
    c = _iota((n, n), 1) // seg
    return (r == c).astype(BF16)


def _segsum(x, ones_bd):
    return _dot2_rhs01(x, ones_bd)


def _softplus(x):
    return jnp.maximum(x, 0.0) + jnp.log(1.0 + jnp.exp(-jnp.abs(x)))


def _sigmoid(x):
    return 1.0 / (1.0 + jnp.exp(-x))


def _silu(x):
    return x * _sigmoid(x)


def _rmsnorm_mod(x, g, shift, scale):
    y = x * lax.rsqrt(jnp.mean(x * x, axis=-1, keepdims=True) + NORM_EPS)
    return (y * g) * (1.0 + scale) + shift


def _chunk_index(i, n_ctx_chunks, n_chunks, rev):
    if not rev:
        return i
    return jnp.where(i < n_ctx_chunks, n_ctx_chunks - 1 - i,
                     n_chunks - 1 - (i - n_ctx_chunks))


def _incl_mask(n, rev):
    r = _iota((n, n), 0)
    c = _iota((n, n), 1)
    return (c >= r) if rev else (c <= r)


def _mod_kernel(c_ref, w_ref, b_ref, o_ref):
    o_ref[0] = _dot3(_silu(c_ref[...]), w_ref[0]) + b_ref[0]


def _modulation(cvec, mod_w, mod_b):
    depth, d, n = mod_w.shape
    tn = 1536
    rows = cvec.shape[0]
    return pl.pallas_call(
        _mod_kernel,
        grid=(depth, n // tn),
        in_specs=[
            pl.BlockSpec((rows, d), lambda l, j: (0, 0)),
            pl.BlockSpec((1, d, tn), lambda l, j: (l, 0, j)),
            pl.BlockSpec((1, 1, tn), lambda l, j: (l, 0, j)),
        ],
        out_specs=pl.BlockSpec((1, rows, tn), lambda l, j: (l, 0, j)),
        out_shape=jax.ShapeDtypeStruct((depth, rows, n), F32),
        compiler_params=pltpu.CompilerParams(
            dimension_semantics=("arbitrary", "arbitrary"), vmem_limit_bytes=VMEM_LIMIT),
        name="adaln_mod",
    )(cvec, mod_w, mod_b.reshape(depth, 1, n))


def _inproj_kernel(*refs, tm, n_ctx_tiles, n_tiles, split):
    j = pl.program_id(1)
    if split:
        (c_ref, cp_ref, cn_ref, l_ref, lp_ref, ln_ref, mod_ref, g_ref, w_ref, mu_ref, cw_ref,
         cb_ref, za_ref, zm_ref, zg_ref, zb_ref, xu_ref) = refs
        is_ctx = j < n_ctx_tiles
        x_tile = jnp.where(is_ctx, c_ref[0], l_ref[0])
        x_prev = jnp.where(is_ctx, cp_ref[0], lp_ref[0])
        x_next = jnp.where(is_ctx, cn_ref[0], ln_ref[0])
        xu_ref[0] = x_tile
    else:
        (x_ref, xp_ref, xn_ref, mod_ref, g_ref, w_ref, mu_ref, cw_ref, cb_ref,
         za_ref, zm_ref, zg_ref, zb_ref) = refs
        x_tile, x_prev, x_next = x_ref[0], xp_ref[0], xn_ref[0]
    g = g_ref[...]
    shift = mod_ref[0, 0, 0:1, :]
    scale = mod_ref[0, 0, 1:2, :]
    h = _rmsnorm_mod(x_tile, g, shift, scale).astype(BF16)
    z = _dg(h, w_ref[...], _NN)
    halo = jnp.concatenate([x_prev, x_next], axis=0)
    hh = _rmsnorm_mod(halo, g, shift, scale).astype(BF16)
    zh = _dg(hh, w_ref[:, :SHIFT_COLS], _NN)
    zs = z[:, :SHIFT_COLS]
    z_rest = z[:, SHIFT_COLS:]
    prev_ok = jnp.logical_and(j != 0, j != n_ctx_tiles).astype(F32)
    next_ok = jnp.logical_and(j != n_ctx_tiles - 1, j != n_tiles - 1).astype(F32)
    row = _iota((tm, SHIFT_COLS), 0)
    prev = jnp.where(row == 0, zh[SUBLANES - 1:SUBLANES, :] * prev_ok, pltpu.roll(zs, 1, 0))
    nxt = jnp.where(row == tm - 1, zh[SUBLANES:SUBLANES + 1, :] * next_ok,
                    pltpu.roll(zs, tm - 1, 0))
    za = zs[:, :RWKV_COLS]
    shifted = za + mu_ref[...] * (0.5 * (prev[:, :RWKV_COLS] + nxt[:, :RWKV_COLS]) - za)
    za_ref[0] = shifted.astype(za_ref.dtype)
    qk = (prev[:, RWKV_COLS:] * cw_ref[0:1, :] + zs[:, RWKV_COLS:] * cw_ref[1:2, :]
          + nxt[:, RWKV_COLS:] * cw_ref[2:3, :] + cb_ref[...])
    zm_ref[0, :, :MLSTM_QKW] = _silu(qk).astype(zm_ref.dtype)
    vo_cols = ZM_COLS - MLSTM_QKW
    zm_ref[0, :, MLSTM_QKW:] = z_rest[:, :vo_cols].astype(zm_ref.dtype)
    zg_ref[0] = z_rest[:, vo_cols:vo_cols + LANES]
    zb_ref[0] = z_rest[:, vo_cols + LANES:].astype(zb_ref.dtype)


def _row_specs(tm, d, n_rows, first_tile):
    tm8 = tm // SUBLANES
    last_tile = n_rows // tm - 1
    last8 = n_rows // SUBLANES - 1
    clip = lambda v, hi: jnp.minimum(jnp.maximum(v, 0), hi)
    return [
        pl.BlockSpec((1, tm, d), lambda i, j: (i, clip(j - first_tile, last_tile), 0)),
        pl.BlockSpec((1, SUBLANES, d), lambda i, j: (i, clip((j - first_tile) * tm8 - 1, last8), 0)),
        pl.BlockSpec((1, SUBLANES, d),
                     lambda i, j: (i, clip((j - first_tile + 1) * tm8, last8), 0)),
    ]


def _in_proj(src, modsel, norm_g, w_r, mu, conv_w, conv_b, *, tm, ctx_len):
    split = isinstance(src, tuple)
    if split:
        ctx, x = src
        b, _, d = x.shape
        l = ctx_len + x.shape[1]
    else:
        b, l, d = src.shape
    n_tiles = l // tm
    n_ctx_tiles = ctx_len // tm
    kern = functools.partial(_inproj_kernel, tm=tm, n_ctx_tiles=n_ctx_tiles, n_tiles=n_tiles,
                             split=split)
    seg = lambda j: jnp.where(j >= n_ctx_tiles, 1, 0)
    if split:
        row_specs = _row_specs(tm, d, ctx_len, 0) + _row_specs(tm, d, l - ctx_len, n_ctx_tiles)
        rows = (ctx, ctx, ctx, x, x, x)
    else:
        row_specs = _row_specs(tm, d, l, 0)
        rows = (src, src, src)
    extra_spec = [pl.BlockSpec((1, tm, d), lambda i, j: (i, j, 0))] if split else []
    extra_shape = [jax.ShapeDtypeStruct((b, l, d), F32)] if split else []
    return pl.pallas_call(
        kern,
        grid=(b, n_tiles),
        in_specs=row_specs + [
            pl.BlockSpec((1, 1, 2, d), lambda i, j: (i, seg(j), 0, 0)),
            pl.BlockSpec((1, d), lambda i, j: (0, 0)),
            pl.BlockSpec((d, IN_COLS_PAD), lambda i, j: (0, 0)),
            pl.BlockSpec((1, RWKV_COLS), lambda i, j: (0, 0)),
            pl.BlockSpec((3, MLSTM_QKW), lambda i, j: (0, 0)),
            pl.BlockSpec((1, MLSTM_QKW), lambda i, j: (0, 0)),
        ],
        out_specs=[
            pl.BlockSpec((1, tm, RWKV_COLS), lambda i, j: (i, j, 0)),
            pl.BlockSpec((1, tm, ZM_COLS), lambda i, j: (i, j, 0)),
            pl.BlockSpec((1, tm, LANES), lambda i, j: (i, j, 0)),
            pl.BlockSpec((1, tm, ZB_COLS), lambda i, j: (i, j, 0)),
        ] + extra_spec,
        out_shape=[
            jax.ShapeDtypeStruct((b, l, RWKV_COLS), ACT),
            jax.ShapeDtypeStruct((b, l, ZM_COLS), ACT),
            jax.ShapeDtypeStruct((b, l, LANES), F32),
            jax.ShapeDtypeStruct((b, l, ZB_COLS), ACT),
        ] + extra_shape,
        compiler_params=pltpu.CompilerParams(
            dimension_semantics=("arbitrary", "arbitrary"), vmem_limit_bytes=VMEM_LIMIT),
        name="in_proj",
    )(*rows, modsel, norm_g.reshape(1, d), w_r, mu.reshape(1, RWKV_COLS), conv_w,
      conv_b.reshape(1, MLSTM_QKW))


def _rwkv_local_kernel(za_ref, kk_ref, ka_ref, rk_ref, w0_ref, w2_ref, a0_ref, a2_ref, g2_ref,
                       g_ref, yl_ref, at_ref, bs_ref, bonus_ref, gate_ref):
    c = CHUNK
    nd = RWKV_DIM
    hd = HEAD_DIM
    nh = RWKV_HEADS
    nb = za_ref.shape[0]
    ones_bd = _seg_ones(nd, hd)
    eye4 = _iota((hd, nd), 0) == (_iota((hd, nd), 1) % hd)
    eye4_f = eye4.astype(F32)
    row2 = _iota((2 * c, 2 * nd), 0)
    rr = row2 % c
    cc = _iota((2 * c, 2 * nd), 1) % c
    diag_ok = jnp.logical_and(cc == rr, row2 >= c)
    keep = _diag_mask(nd, nd, hd, hd)
    bd = lambda x: _block_diag(x, nh, hd, hd, keep)
    units, pr, rhs, vb, pm, rm, qe, ke, gam, gmask = ([] for _ in range(10))
    for rb in range(nb):
        za = za_ref[rb].astype(F32)
        r = za[:, 0:nd]
        k = za[:, nd:2 * nd]
        v = za[:, 2 * nd:3 * nd]
        kkr = k * kk_ref[...]
        kk = kkr / jnp.maximum(jnp.sqrt(_segsum(kkr * kkr, ones_bd)), 1e-12)
        bonus_ref[rb] = (_segsum(r * k * rk_ref[...], ones_bd) * v).astype(bonus_ref.dtype)
        gd = za[:, 3 * nd + 2 * RWKV_W_LORA + 2 * RWKV_A_LORA:]
        gate_ref[rb] = _dot1(_sigmoid(gd), g2_ref[...]).astype(gate_ref.dtype)
        v_b = v.astype(BF16)
        for d in range(2):
            rev = d == 1
            wo = 3 * nd + d * RWKV_W_LORA
            ao = 3 * nd + 2 * RWKV_W_LORA + d * RWKV_A_LORA
            w_lo = _dot1(jnp.tanh(za[:, wo:wo + RWKV_W_LORA]), w2_ref[d])
            log_w = -_softplus(-(w0_ref[d] + w_lo)) - 0.5
            logdec = -jnp.exp(log_w)
            a = _sigmoid(a0_ref[d] + _dot1(za[:, ao:ao + RWKV_A_LORA], a2_ref[d]))
            kd = k * (1.0 + (a - 1.0) * ka_ref[...])
            bvec = kk * a
            ld_hi, ld_lo = _split(logdec)
            tri = _incl_mask(c, rev).astype(BF16)
            cum = _dg(tri, ld_hi, _NN) + _dg(tri, ld_lo, _NN)
            total = cum[0:1, :] if rev else cum[c - 1:c, :]
            e_neg = jnp.exp(-cum)
            e_end = jnp.exp(total - cum)
            pm_d = kk * jnp.exp(cum - logdec)
            rm_d = r * jnp.exp(cum)
            units.append((rb, d))
            pm.append(pm_d)
            rm.append(rm_d)
            pr.append(jnp.concatenate([pm_d, rm_d], axis=0).astype(BF16))
            rhs.append(jnp.concatenate([bd((bvec * e_neg).astype(BF16)),
                                        bd((kd * e_neg).astype(BF16))], axis=0))
            vb.append(v_b)
            qe.append((bvec * e_end).astype(BF16))
            ke.append((kd * e_end).astype(BF16))
            gam.append(jnp.exp(total))
            gmask.append(jnp.logical_or(cc > rr if rev else cc < rr, diag_ok))
    n = len(units)
    a4 = [jnp.where(gmask[i], _dg(pr[i], rhs[i], _NT), 0.0) for i in range(n)]
    a4b = [x.astype(BF16) for x in a4]
    pw = [_dg(a4b[i][:c, :nd], bd(a4b[i][:c, :nd]), _NN) for i in range(n)]
    lv = [_dg(a4b[i][:, nd:], bd(vb[i]), _NN) for i in range(n)]
    tinv = [eye4_f - a4[i][:c, :nd] for i in range(n)]
    covered = 2
    while True:
        pwd = [bd(x.astype(BF16)) for x in pw]
        tinv = [tinv[i] + _dg(tinv[i].astype(BF16), pwd[i], _NN) for i in range(n)]
        covered *= 2
        if covered >= c:
            break
        pw = [_dg(pw[i].astype(BF16), pwd[i], _NN) for i in range(n)]
    wz = [_dg(tinv[i].astype(BF16),
              jnp.concatenate([bd(pm[i].astype(BF16)), bd(lv[i][:c].astype(BF16))], axis=1),
              _NN).astype(BF16) for i in range(n)]
    awz = [_dg(a4b[i][c:, :nd],
               jnp.concatenate([bd(wz[i][:, :nd]), bd(wz[i][:, nd:])], axis=1), _NN)
           for i in range(n)]
    eye = _iota((hd, hd), 0) == _iota((hd, hd), 1)
    for i, (rb, d) in enumerate(units):
        g_ref[d, rb] = (rm[i] - awz[i][:, :nd]).astype(g_ref.dtype)
        yl_ref[d, rb] = (lv[i][c:] - awz[i][:, nd:]).astype(yl_ref.dtype)
        ats, bss = [], []
        for h in range(nh):
            sl = slice(h * hd, (h + 1) * hd)
            wz_h = jnp.concatenate([wz[i][:, sl], wz[i][:, nd + h * hd:nd + (h + 1) * hd]], axis=1)
            qwz = _dg(qe[i][:, sl], wz_h, _TN)
            ats.append(jnp.where(eye, jnp.broadcast_to(gam[i][:, sl], (hd, hd)), 0.0)
                       - qwz[:, :hd])
            bss.append(_dg(ke[i][:, sl], vb[i][:, sl], _TN) - qwz[:, hd:])
        at_ref[d, rb] = jnp.concatenate(ats, axis=1).astype(at_ref.dtype)
        bs_ref[d, rb] = jnp.concatenate(bss, axis=1).astype(bs_ref.dtype)


def _rwkv_local(za, k_k, k_a, r_k, w0, w2, a0, a2, g2, *, nb):
    b, l, _ = za.shape
    c = CHUNK
    nd = RWKV_DIM
    full = lambda shape: pl.BlockSpec(shape, lambda i, j: (0,) * len(shape))
    dir_spec = pl.BlockSpec((2, nb, c, nd), lambda i, j: (0, i, j, 0))
    dir_shape = jax.ShapeDtypeStruct((2, b, l, nd), ACT)
    row_spec = pl.BlockSpec((nb, c, nd), lambda i, j: (i, j, 0))
    row_shape = jax.ShapeDtypeStruct((b, l, nd), ACT)
    return pl.pallas_call(
        _rwkv_local_kernel,
        grid=(b // nb, l // c),
        in_specs=[
            pl.BlockSpec((nb, c, RWKV_COLS), lambda i, j: (i, j, 0)),
            full((1, nd)), full((1, nd)), full((1, nd)),
            full((2, 1, nd)), full((2, RWKV_W_LORA, nd)),
            full((2, 1, nd)), full((2, RWKV_A_LORA, nd)),
            full((RWKV_G_LORA, nd)),
        ],
        out_specs=[dir_spec] * 4 + [row_spec] * 2,
        out_shape=[dir_shape] * 4 + [row_shape] * 2,
        compiler_params=pltpu.CompilerParams(
            dimension_semantics=("arbitrary", "arbitrary"), vmem_limit_bytes=VMEM_LIMIT),
        name="rwkv_local",
    )(za, k_k.reshape(1, -1), k_a.reshape(1, -1), r_k.reshape(1, -1), w0.reshape(2, 1, nd),
      w2, a0.reshape(2, 1, nd), a2, g2)


def _rwkv_seq_kernel(gf_ref, ylf_ref, atf_ref, bsf_ref, gr_ref, ylr_ref, atr_ref, bsr_ref,
                     yf_ref, yr_ref, m_ref):
    nb = yf_ref.shape[0]

    @pl.when(pl.program_id(1) == 0)
    def _():
        m_ref[...] = jnp.zeros_like(m_ref)

    dirs = ((gf_ref, ylf_ref, atf_ref, bsf_ref, yf_ref), (gr_ref, ylr_ref, atr_ref, bsr_ref, yr_ref))
    units = [(rb, d) for rb in range(nb) for d in range(2)]
    keep = _diag_mask(RWKV_DIM, RWKV_DIM, HEAD_DIM, HEAD_DIM)
    m_bd = [_block_diag(m_ref[u].astype(BF16), RWKV_HEADS, HEAD_DIM, HEAD_DIM, keep)
            for u in range(len(units))]
    y = [_dg(dirs[d][0][0, rb], m_bd[u], _NN) for u, (rb, d) in enumerate(units)]
    m_new = [_dg(dirs[d][2][0, rb], m_bd[u], _NN) for u, (rb, d) in enumerate(units)]
    for u, (rb, d) in enumerate(units):
        dirs[d][4][rb] = (y[u] + dirs[d][1][0, rb].astype(F32)).astype(dirs[d][4].dtype)
        m_ref[u] = m_new[u] + dirs[d][3][0, rb].astype(F32)


def _rwkv_seq(g, yl, at, bs, *, ctx_len, nb):
    _, b, l, nd = g.shape
    c = CHUNK
    n_chunks = l // c
    n_ctx_chunks = ctx_len // c
    rev_idx = lambda j: _chunk_index(j, n_ctx_chunks, n_chunks, True)
    fwd = pl.BlockSpec((1, nb, c, nd), lambda i, j: (0, i, j, 0))
    bwd = pl.BlockSpec((1, nb, c, nd), lambda i, j: (1, i, rev_idx(j), 0))
    y_shape = jax.ShapeDtypeStruct((b, l, nd), ACT)
    return pl.pallas_call(
        _rwkv_seq_kernel,
        grid=(b // nb, n_chunks),
        in_specs=[fwd] * 4 + [bwd] * 4,
        out_specs=[pl.BlockSpec((nb, c, nd), lambda i, j: (i, j, 0)),
                   pl.BlockSpec((nb, c, nd), lambda i, j: (i, rev_idx(j), 0))],
        out_shape=[y_shape, y_shape],
        scratch_shapes=[pltpu.VMEM((nb * 2, HEAD_DIM, RWKV_DIM), F32)],
        compiler_params=pltpu.CompilerParams(
            dimension_semantics=("arbitrary", "arbitrary"), vmem_limit_bytes=VMEM_LIMIT),
        name="rwkv_seq",
    )(g, yl, at, bs, g, yl, at, bs)


def _cummax_rows(x, rev):
    n = x.shape[0]
    row = _iota(x.shape, 0)
    shift = 1
    while shift < n:
        if rev:
            moved = jnp.where(row >= n - shift, -jnp.inf, pltpu.roll(x, n - shift, 0))
        else:
            moved = jnp.where(row < shift, -jnp.inf, pltpu.roll(x, shift, 0))
        x = jnp.maximum(x, moved)
        shift *= 2
    return x


def _head_lane(d):
    return d * 2 * MLSTM_HEADS + MLSTM_HEADS


def _sel_expand(d, width):
    n = MLSTM_HEADS * width
    return (_iota((LANES, n), 0) == _head_lane(d) + _iota((LANES, n), 1) // width).astype(BF16)


def _sel_reduce(d, width):
    n = MLSTM_HEADS * width
    return (_iota((n, LANES), 1) == _head_lane(d) + _iota((n, LANES), 0) // width).astype(BF16)


def _diag_mask(rows, cols, row_block, col_block):
    return (_iota((rows, cols), 0) // row_block) == (_iota((rows, cols), 1) // col_block)


def _block_diag(x, reps, row_block, col_block, keep=None):
    t = jnp.concatenate([x] * reps, axis=0)
    if keep is None:
        keep = _diag_mask(t.shape[0], t.shape[1], row_block, col_block)
    return jnp.where(keep, t, jnp.zeros_like(t))


def _mlstm_local_kernel(zm_ref, zg_ref, gb_ref, num_ref, st_ref, cl_ref, rows_ref):
    c = CHUNK
    nh = MLSTM_HEADS
    dk = MLSTM_QK
    dv = MLSTM_V
    nb = zm_ref.shape[0]
    lane_j = _iota((c, nh * c), 1) % c
    row_s = _iota((c, nh * c), 0)
    lane_g = _iota((c, LANES), 1)
    cl_keep = _diag_mask(nh * dk, MLSTM_DIM, dk, dv)
    k_keep = _diag_mask(nh * c, nh * dk, c, dk)
    v_keep = _diag_mask(nh * c, MLSTM_DIM, c, dv)
    sel_c = [_sel_expand(d, c) for d in range(2)]
    sel_k = [_sel_expand(d, dk) for d in range(2)]
    sel_r = [_sel_reduce(d, c) for d in range(2)]
    k, vb, log_f, li, qk, vbd = [], [], [], [], [], []
    for rb in range(nb):
        zm = zm_ref[rb]
        qb = (zm[:, :nh * dk].astype(F32) * (dk ** -0.5)).astype(BF16)
        k.append(zm[:, nh * dk:MLSTM_QKW].astype(F32))
        vb.append(zm[:, MLSTM_QKW:MLSTM_QKW + MLSTM_DIM].astype(BF16))
        gates = zg_ref[rb] + gb_ref[...]
        capped = GATE_SOFTCAP * jnp.tanh(gates * (1.0 / GATE_SOFTCAP))
        log_f.append(-_softplus(-capped))
        li.append(pltpu.roll(capped, nh, 1))
        qk.append(_dg(qb, _block_diag(k[rb].astype(BF16), nh, c, dk, k_keep), _NT))
        vbd.append(_block_diag(vb[rb], nh, c, dv, v_keep))
    units = [(rb, d) for rb in range(nb) for d in range(2)]
    b_all = [_dot_exact_lhs(_incl_mask(c, d == 1).astype(BF16), log_f[rb]) for rb, d in units]
    x = [li[rb] - b_all[u] for u, (rb, d) in enumerate(units)]
    cm = [_cummax_rows(x[u], d == 1) for u, (rb, d) in enumerate(units)]
    e1 = [_dot_exact_rhs(-cm[u], sel_c[d]) for u, (rb, d) in enumerate(units)]
    xe = [_dot_exact_rhs(x[u], sel_c[d]) for u, (rb, d) in enumerate(units)]
    g_row, m_loc, e32 = [], [], []
    for u, (rb, d) in enumerate(units):
        last = 0 if d == 1 else c - 1
        g_row.append(b_all[u][last:last + 1, :])
        w_end = g_row[u] - b_all[u] + li[rb]
        m_loc.append(jnp.max(w_end, axis=0, keepdims=True))
        e32.append(_dot2_rhs01(jnp.exp(w_end - m_loc[u]), sel_k[d]))
    wi = []
    for u, (rb, d) in enumerate(units):
        x_row = jnp.sum(jnp.where(row_s == lane_j, xe[u], 0.0), axis=0, keepdims=True)
        earlier = (lane_j >= row_s) if d == 1 else (lane_j <= row_s)
        wi.append((qk[rb] * jnp.exp(jnp.where(earlier, e1[u] + x_row, -jnp.inf))).astype(BF16))
    nd = [_dg(wi[u], jnp.concatenate([vbd[rb], sel_r[d]], axis=1), _NN)
          for u, (rb, d) in enumerate(units)]
    ke = [k[rb] * e32[u] for u, (rb, d) in enumerate(units)]
    full = [_dg(ke[u].astype(BF16), vb[rb], _TN) for u, (rb, d) in enumerate(units)]
    for u, (rb, d) in enumerate(units):
        num_ref[d, rb] = nd[u][:, :MLSTM_DIM].astype(num_ref.dtype)
        on_heads = jnp.logical_and(lane_g >= _head_lane(d), lane_g < _head_lane(d) + nh)
        st_ref[d, rb] = (nd[u][:, MLSTM_DIM:]
                         + pltpu.roll(jnp.where(on_heads, b_all[u] + cm[u], 0.0), STAT_SHIFT, 1)
                         + pltpu.roll(jnp.where(on_heads, b_all[u], 0.0), 2 * STAT_SHIFT, 1))
        f = jnp.where(cl_keep, full[u], 0.0)
        cl_ref[d, rb] = ((f[0:dk] + f[dk:2 * dk])
                         + (f[2 * dk:3 * dk] + f[3 * dk:4 * dk])).astype(cl_ref.dtype)
        rows_ref[d, rb] = jnp.concatenate(
            [jnp.sum(ke[u], axis=0, keepdims=True), m_loc[u], g_row[u],
             jnp.zeros((SUBLANES - 3, LANES), F32)], axis=0)


def _mlstm_local(zm, zg, gate_bias, *, nb):
    b, l, _ = zm.shape
    c = CHUNK
    n_chunks = l // c
    spec = lambda rows, w, dt: pl.BlockSpec((2, nb, rows, w), lambda i, j: (0, i, j, 0))
    shape = lambda rows, w, dt: jax.ShapeDtypeStruct((2, b, n_chunks * rows, w), dt)
    outs = [(c, MLSTM_DIM, ACT), (c, LANES, F32), (MLSTM_QK, MLSTM_DIM, ACT),
            (SUBLANES, LANES, F32)]
    return pl.pallas_call(
        _mlstm_local_kernel,
        grid=(b // nb, n_chunks),
        in_specs=[
            pl.BlockSpec((nb, c, ZM_COLS), lambda i, j: (i, j, 0)),
            pl.BlockSpec((nb, c, LANES), lambda i, j: (i, j, 0)),
            pl.BlockSpec((1, LANES), lambda i, j: (0, 0)),
        ],
        out_specs=[spec(*o) for o in outs],
        out_shape=[shape(*o) for o in outs],
        compiler_params=pltpu.CompilerParams(
            dimension_semantics=("arbitrary", "arbitrary"), vmem_limit_bytes=VMEM_LIMIT),
        name="mlstm_local",
    )(zm, zg, gate_bias)


def _mlstm_seq_kernel(qf_ref, numf_ref, stf_ref, clf_ref, rowsf_ref,
                      qr_ref, numr_ref, str_ref, clr_ref, rowsr_ref,
                      hf_ref, hr_ref, cbd_ref, n_ref, m_ref):
    c = CHUNK
    nh = MLSTM_HEADS
    dk = MLSTM_QK
    dv = MLSTM_V
    nb = hf_ref.shape[0]

    @pl.when(pl.program_id(1) == 0)
    def _():
        cbd_ref[...] = jnp.zeros_like(cbd_ref)
        n_ref[...] = jnp.zeros_like(n_ref)
        m_ref[...] = jnp.zeros_like(m_ref)

    dirs = ((qf_ref, numf_ref, stf_ref, clf_ref, rowsf_ref, hf_ref),
            (qr_ref, numr_ref, str_ref, clr_ref, rowsr_ref, hr_ref))
    units = [(rb, d) for rb in range(nb) for d in range(2)]
    nu = len(units)
    cl_keep = _diag_mask(nh * dk, MLSTM_DIM, dk, dv)
    sel_v = [_sel_expand(d, dv) for d in range(2)]
    sel_k = [_sel_expand(d, dk) for d in range(2)]
    sel_r = [_sel_reduce(d, dk) for d in range(2)]
    q = [dirs[d][0][rb].astype(F32) * (dk ** -0.5) for rb, d in units]
    rows = [dirs[d][4][0, rb] for rb, d in units]
    m_row = [m_ref[u] for u in range(nu)]
    n_row = [n_ref[u] for u in range(nu)]
    cbd = [cbd_ref[u] for u in range(nu)]
    qn = [_dot2_rhs01(q[u] * n_row[u], sel_r[d]) for u, (rb, d) in enumerate(units)]
    qc = [_dg(q[u].astype(BF16), cbd[u].astype(BF16), _NN) for u in range(nu)]
    a12, srow = [], []
    for u, (rb, d) in enumerate(units):
        stats = dirs[d][2][0, rb]
        mi = pltpu.roll(stats, LANES - STAT_SHIFT, 1)
        log_inter = pltpu.roll(stats, LANES - 2 * STAT_SHIFT, 1) + m_row[u]
        m_out = jnp.maximum(log_inter, mi)
        s_intra = jnp.exp(mi - m_out)
        s_inter = jnp.exp(log_inter - m_out)
        den = s_intra * stats + s_inter * qn[u]
        dinv = 1.0 / jnp.maximum(jnp.abs(den), jnp.exp(-m_out))
        a12.append(jnp.concatenate([s_intra * dinv, s_inter * dinv], axis=0))
        m_loc, g_row = rows[u][1:2], rows[u][2:3]
        m_new = jnp.maximum(g_row + m_row[u], m_loc)
        srow.append(jnp.concatenate(
            [jnp.exp(g_row + m_row[u] - m_new), jnp.exp(m_loc - m_new),
             jnp.zeros((SUBLANES - 2, LANES), F32)], axis=0))
        m_ref[u] = m_new
    a12e = [_dot2_rhs01(a12[u], sel_v[d]) for u, (rb, d) in enumerate(units)]
    s_v = [_dot2_rhs01(srow[u], sel_v[d]) for u, (rb, d) in enumerate(units)]
    s_k = [_dot2_rhs01(srow[u], sel_k[d]) for u, (rb, d) in enumerate(units)]
    for u, (rb, d) in enumerate(units):
        h_out = a12e[u][:c] * dirs[d][1][0, rb].astype(F32) + a12e[u][c:] * qc[u]
        dirs[d][5][rb] = h_out.astype(dirs[d][5].dtype)
        cl_full = _block_diag(dirs[d][3][0, rb].astype(F32), nh, dk, dv, cl_keep)
        cbd_ref[u] = s_v[u][0:1] * cbd[u] + s_v[u][1:2] * cl_full
        n_ref[u] = s_k[u][0:1] * n_row[u] + s_k[u][1:2] * rows[u][0:1]


def _mlstm_seq(zm, num, stats, cl, rows, *, ctx_len, nb):
    b, l, _ = zm.shape
    c = CHUNK
    n_chunks = l // c
    n_ctx_chunks = ctx_len // c
    rev_idx = lambda j: _chunk_index(j, n_ctx_chunks, n_chunks, True)

    def specs(d):
        cidx = (lambda j: j) if d == 0 else rev_idx
        blk = lambda r, w: pl.BlockSpec((1, nb, r, w), lambda i, j: (d, i, cidx(j), 0))
        return [pl.BlockSpec((nb, c, MLSTM_HEADS * MLSTM_QK), lambda i, j: (i, cidx(j), 0)),
                blk(c, MLSTM_DIM), blk(c, LANES), blk(MLSTM_QK, MLSTM_DIM),
                blk(SUBLANES, LANES)]

    h_shape = jax.ShapeDtypeStruct((b, l, MLSTM_DIM), ACT)
    args = (zm, num, stats, cl, rows)
    return pl.pallas_call(
        _mlstm_seq_kernel,
        grid=(b // nb, n_chunks),
        in_specs=specs(0) + specs(1),
        out_specs=[pl.BlockSpec((nb, c, MLSTM_DIM), lambda i, j: (i, j, 0)),
                   pl.BlockSpec((nb, c, MLSTM_DIM), lambda i, j: (i, rev_idx(j), 0))],
        out_shape=[h_shape, h_shape],
        scratch_shapes=[
            pltpu.VMEM((2 * nb, MLSTM_HEADS * MLSTM_QK, MLSTM_DIM), F32),
            pltpu.VMEM((2 * nb, 1, LANES), F32),
            pltpu.VMEM((2 * nb, 1, LANES), F32),
        ],
        compiler_params=pltpu.CompilerParams(
            dimension_semantics=("arbitrary", "arbitrary"), vmem_limit_bytes=VMEM_LIMIT),
        name="mlstm_seq",
    )(*args, *args)


def _rope_swap(x):
    lane = _iota(x.shape, 1)
    half = MLA_ROPE // 2
    return jnp.where(lane < MLA_NOPE + half, pltpu.roll(x, LANES - half, 1),
                     pltpu.roll(x, half, 1))


def _mla_proj_kernel(zb_ref, cq_ref, sq_ref, ck_ref, sk_ref, qg_ref, kvg_ref,
                     wq_ref, wk_ref, wv_ref, q_ref, k_ref, v_ref):
    hq = MLA_HEADS * QPAD
    zb = zb_ref[0].astype(F32)
    cq = zb[:, :MLA_Q_RANK]
    ckv = zb[:, MLA_Q_RANK:MLA_Q_RANK + MLA_KV_RANK]
    kr = zb[:, MLA_Q_RANK + MLA_KV_RANK:]
    cqn = cq * lax.rsqrt(jnp.mean(cq * cq, axis=-1, keepdims=True) + NORM_EPS) * qg_ref[...]
    ckn = ckv * lax.rsqrt(jnp.mean(ckv * ckv, axis=-1, keepdims=True) + NORM_EPS) * kvg_ref[...]
    q_all = _dot1(cqn, wq_ref[...])
    k_all = _dot1(ckn, wk_ref[...])
    v_ref[0] = _dg(wv_ref[...], ckn.astype(BF16), _NT).astype(BF16)
    cos_q, sin_q, cos_k = cq_ref[...], sq_ref[...], ck_ref[...]
    kr_rot = _rope_swap(kr) * sk_ref[...]
    inv_dim = 1.0 / MLA_QK
    for h in range(MLA_HEADS):
        sl = slice(h * QPAD, (h + 1) * QPAD)
        qh = q_all[:, sl]
        q_inv = lax.rsqrt(jnp.sum(qh * qh, axis=-1, keepdims=True) * inv_dim + NORM_EPS)
        q_ref[0, :, sl] = ((qh * cos_q + q_all[:, hq + h * QPAD:hq + (h + 1) * QPAD] * sin_q)
                           * q_inv).astype(BF16)
        kh = k_all[:, sl] + kr
        k_inv = lax.rsqrt(jnp.sum(kh * kh, axis=-1, keepdims=True) * inv_dim + NORM_EPS)
        k_ref[0, :, sl] = ((kh * cos_k + kr_rot) * k_inv).astype(BF16)


def _mla_proj(zb, tables, q_norm_g, kv_norm_g, wq_r, wk_r, wv_r, *, tm):
    b, l, _ = zb.shape
    hq = MLA_HEADS * QPAD
    full = lambda shape: pl.BlockSpec(shape, lambda i, j: (0,) * len(shape))
    table = pl.BlockSpec((tm, QPAD), lambda i, j: (j, 0))
    return pl.pallas_call(
        _mla_proj_kernel,
        grid=(b, l // tm),
        in_specs=[
            pl.BlockSpec((1, tm, ZB_COLS), lambda i, j: (i, j, 0)),
            table, table, table, table,
            full((1, MLA_Q_RANK)), full((1, MLA_KV_RANK)),
            full((MLA_Q_RANK, 2 * hq)), full((MLA_KV_RANK, hq)), full((MLA_DIM, MLA_KV_RANK)),
        ],
        out_specs=[
            pl.BlockSpec((1, tm, hq), lambda i, j: (i, j, 0)),
            pl.BlockSpec((1, tm, hq), lambda i, j: (i, j, 0)),
            pl.BlockSpec((1, MLA_DIM, tm), lambda i, j: (i, 0, j)),
        ],
        out_shape=[
            jax.ShapeDtypeStruct((b, l, hq), BF16),
            jax.ShapeDtypeStruct((b, l, hq), BF16),
            jax.ShapeDtypeStruct((b, MLA_DIM, l), BF16),
        ],
        compiler_params=pltpu.CompilerParams(
            dimension_semantics=("arbitrary", "arbitrary"), vmem_limit_bytes=VMEM_LIMIT),
        name="mla_proj",
    )(zb, *tables, q_norm_g.reshape(1, -1), kv_norm_g.reshape(1, -1), wq_r, wk_r, wv_r)


def _attend_tile(q_ref, k_ref, vt_ref, o_ref, n_keys):
    tq = q_ref.shape[1]
    starts = list(range(0, n_keys, KV_BLOCK))
    bounds = [(s0, min(s0 + KV_BLOCK, n_keys)) for s0 in starts]
    n_blocks = len(bounds)
    heads = range(ATTN_HEADS_PER_STEP)
    q = [q_ref[0, :, h * QPAD:(h + 1) * QPAD] for h in heads]

    def scores(j):
        lo, hi = bounds[j]
        return [_dg(k_ref[0, lo:hi, h * QPAD:(h + 1) * QPAD], q[h], _NT)
                for h in heads]

    m = [jnp.full((1, tq), -jnp.inf, F32) for _ in heads]
    acc = [jnp.zeros((MLA_V + 2 * SUBLANES, tq), F32) for _ in heads]
    pending = [scores(j) for j in range(min(SCORE_LOOKAHEAD, n_blocks))]
    for j in range(n_blocks):
        if j + SCORE_LOOKAHEAD < n_blocks:
            pending.append(scores(j + SCORE_LOOKAHEAD))
        s = pending.pop(0)
        lo, hi = bounds[j]
        ones_rows = jnp.ones((2 * SUBLANES, hi - lo), BF16)
        for h in heads:
            vt = jnp.concatenate([vt_ref[0, h * MLA_V:(h + 1) * MLA_V, lo:hi], ones_rows], axis=0)
            m_new = jnp.maximum(m[h], jnp.max(s[h], axis=0, keepdims=True))
            p = jnp.exp2(s[h] - m_new).astype(BF16)
            acc[h] = jnp.exp2(m[h] - m_new) * acc[h] + _dg(vt, p, _NN)
            m[h] = m_new
    out_t = jnp.concatenate([acc[h][:MLA_V] / acc[h][MLA_V:MLA_V + 1] for h in heads],
                            axis=0)
    o_ref[0] = out_t.T.astype(o_ref.dtype)


def _mla_attn_kernel(q_ref, k_ref, v_ref, o_ref, *, ctx_len, first_tile, n_ctx_tiles):
    n_all = k_ref.shape[1]
    if first_tile >= n_ctx_tiles:
        _attend_tile(q_ref, k_ref, v_ref, o_ref, n_all)
        return
    is_ctx = (pl.program_id(2) + first_tile) < n_ctx_tiles

    @pl.when(is_ctx)
    def _():
        _attend_tile(q_ref, k_ref, v_ref, o_ref, ctx_len)

    @pl.when(jnp.logical_not(is_ctx))
    def _():
        _attend_tile(q_ref, k_ref, v_ref, o_ref, n_all)


def _mla_attn(q, k, v, *, tq, ctx_len, first_tile):
    b, l, _ = q.shape
    assert l % LANES == 0 and ctx_len % LANES == 0
    n_q = l // tq - first_tile
    hps = ATTN_HEADS_PER_STEP
    kern = functools.partial(_mla_attn_kernel, ctx_len=ctx_len, first_tile=first_tile,
                             n_ctx_tiles=ctx_len // tq)
    return pl.pallas_call(
        kern,
        grid=(b, MLA_HEADS // hps, n_q),
        in_specs=[
            pl.BlockSpec((1, tq, hps * QPAD), lambda i, p, j: (i, j + first_tile, p)),
            pl.BlockSpec((1, l, hps * QPAD), lambda i, p, j: (i, 0, p)),
            pl.BlockSpec((1, hps * MLA_V, l), lambda i, p, j: (i, p, 0)),
        ],
        out_specs=pl.BlockSpec((1, tq, hps * MLA_V), lambda i, p, j: (i, j + first_tile, p)),
        out_shape=jax.ShapeDtypeStruct((b, l, MLA_DIM), ACT),
        compiler_params=pltpu.CompilerParams(
            dimension_semantics=("arbitrary", "arbitrary", "arbitrary"),
            vmem_limit_bytes=VMEM_LIMIT),
        name="mla_attn",
    )(q, k, v)


def _mix_ffn_kernel(x_ref, yf_ref, yr_ref, bonus_ref, gate_ref, attn_ref, hf_ref, hr_ref,
                    o_ref, mod_ref, lng_ref, lnb_ref, mng_ref, wa_ref, wb_ref, wm_ref,
                    g2_ref, w1_ref, w2_ref, out_ref, *, hidden):
    ones_bd = _seg_ones(RWKV_DIM, HEAD_DIM)
    inv = 1.0 / HEAD_DIM
    y = yf_ref[0].astype(F32) + yr_ref[0].astype(F32)
    mean = _segsum(y, ones_bd) * inv
    yc = y - mean
    var = _segsum(yc * yc, ones_bd) * inv
    yn = yc * lax.rsqrt(var + RWKV_GN_EPS) * lng_ref[...] + lnb_ref[...]
    ya = (yn + bonus_ref[0].astype(F32)) * gate_ref[0].astype(F32)
    hm = hf_ref[0].astype(F32) + hr_ref[0].astype(F32)
    hn = hm * lax.rsqrt(_segsum(hm * hm, ones_bd) * inv + NORM_EPS)
    ym = hn * mng_ref[...] * _sigmoid(o_ref[0].astype(F32))
    mix = (_dot1(ya, wa_ref[...]) + _dot1(attn_ref[0], wb_ref[...]) + _dot1(ym, wm_ref[...]))
    x = x_ref[0] + mod_ref[0, 0, 0:1, :] * mix
    h = _rmsnorm_mod(x, g2_ref[...], mod_ref[0, 0, 1:2, :], mod_ref[0, 0, 2:3, :]).astype(BF16)
    gu = _dg(h, w1_ref[...], _NN)
    act = (_silu(gu[:, :hidden]) * gu[:, hidden:]).astype(BF16)
    out_ref[0] = x + mod_ref[0, 0, 3:4, :] * _dg(act, w2_ref[...], _NN)


def _mix_ffn(x, yf, yr, bonus, gate, attn, hf, hr, zm, modsel, ln_g, ln_b, mnorm_g,
             wa, wb, wm, norm2_g, w1, w2, *, tm, ctx_len, first_tile):
    b, l, d = x.shape
    hidden = w2.shape[0]
    n_ctx_tiles = ctx_len // tm
    n_tiles = l // tm - first_tile
    row = lambda w: pl.BlockSpec((1, tm, w), lambda i, j: (i, j + first_tile, 0))
    full = lambda shape: pl.BlockSpec(shape, lambda i, j: (0,) * len(shape))
    seg = lambda j: jnp.where(j + first_tile >= n_ctx_tiles, 1, 0)
    o_block = (MLSTM_QKW + MLSTM_DIM) // MLSTM_DIM
    return pl.pallas_call(
        functools.partial(_mix_ffn_kernel, hidden=hidden),
        grid=(b, n_tiles),
        in_specs=[
            row(d), row(RWKV_DIM), row(RWKV_DIM), row(RWKV_DIM), row(RWKV_DIM), row(MLA_DIM),
            row(MLSTM_DIM), row(MLSTM_DIM),
            pl.BlockSpec((1, tm, MLSTM_DIM), lambda i, j: (i, j + first_tile, o_block)),
            pl.BlockSpec((1, 1, 4, d), lambda i, j: (i, seg(j), 0, 0)),
            full((1, RWKV_DIM)), full((1, RWKV_DIM)), full((1, MLSTM_DIM)),
            full((RWKV_DIM, d)), full((MLA_DIM, d)), full((MLSTM_DIM, d)),
            full((1, d)), full((d, 2 * hidden)), full((hidden, d)),
        ],
        out_specs=pl.BlockSpec((1, tm, d), lambda i, j: (i, j, 0)),
        out_shape=jax.ShapeDtypeStruct((b, n_tiles * tm, d), F32),
        compiler_params=pltpu.CompilerParams(
            dimension_semantics=("arbitrary", "arbitrary"), vmem_limit_bytes=VMEM_LIMIT),
        name="mix_ffn",
    )(x, yf, yr, bonus, gate, attn, hf, hr, zm, modsel, ln_g.reshape(1, -1),
      ln_b.reshape(1, -1), mnorm_g.reshape(1, -1), wa, wb, wm, norm2_g.reshape(1, d), w1, w2)


def _arrange_w_in(w_in):
    d = w_in.shape[0]
    wa = w_in[:, :RWKV_COLS]
    wb = w_in[:, RWKV_COLS:RWKV_COLS + MLA_COLS]
    wm = w_in[:, RWKV_COLS + MLA_COLS:]
    gates = wm[:, MLSTM_QKW + 2 * MLSTM_DIM:]
    gate_blk = jnp.pad(gates, ((0, 0), (0, LANES - gates.shape[1])))
    kr = wb[:, MLA_Q_RANK + MLA_KV_RANK:]
    kr_blk = jnp.pad(kr, ((0, 0), (MLA_NOPE, LANES - MLA_NOPE - MLA_ROPE)))
    out = jnp.concatenate([wa, wm[:, :MLSTM_QKW + 2 * MLSTM_DIM], gate_blk,
                           wb[:, :MLA_Q_RANK + MLA_KV_RANK], kr_blk], axis=1)
    assert out.shape == (d, IN_COLS_PAD)
    return out.astype(BF16)


def _arrange_mla_weights(w_uq, w_ukv):
    rq = w_uq.shape[0]
    wq = w_uq.reshape(rq, MLA_HEADS, MLA_QK)
    wq = jnp.pad(wq, ((0, 0), (0, 0), (0, QPAD - MLA_QK)))
    half = MLA_ROPE // 2
    wq_swap = jnp.concatenate(
        [jnp.zeros_like(wq[:, :, :MLA_NOPE]), wq[:, :, MLA_NOPE + half:MLA_QK],
         wq[:, :, MLA_NOPE:MLA_NOPE + half], jnp.zeros_like(wq[:, :, MLA_QK:])], axis=2)
    wq = jnp.concatenate([wq.reshape(rq, -1), wq_swap.reshape(rq, -1)], axis=1)
    rk = w_ukv.shape[0]
    wkv = w_ukv.reshape(rk, MLA_HEADS, MLA_NOPE + MLA_V)
    wk = jnp.pad(wkv[:, :, :MLA_NOPE], ((0, 0), (0, 0), (0, QPAD - MLA_NOPE)))
    wk = wk.reshape(rk, MLA_HEADS * QPAD)
    wv_t = wkv[:, :, MLA_NOPE:].reshape(rk, MLA_DIM).T
    return wq.astype(BF16), wk.astype(BF16), wv_t.astype(BF16)


def _rope_tables(seq_len, ctx_len):
    rows = seq_len // GRID_W
    row = jnp.repeat(jnp.arange(rows, dtype=F32), GRID_W)
    col = jnp.tile(jnp.arange(GRID_W, dtype=F32), rows)
    n_freq = MLA_ROPE // 4
    inv = jnp.power(ROPE_BASE, -jnp.arange(n_freq, dtype=F32) / n_freq)
    ang = jnp.concatenate([row[:, None] * inv, col[:, None] * inv], axis=-1)
    cos, sin = jnp.cos(ang), jnp.sin(ang)
    pad_l, pad_r = MLA_NOPE, QPAD - MLA_QK
    cos_f = jnp.concatenate([jnp.ones((seq_len, pad_l), F32), cos, cos,
                             jnp.ones((seq_len, pad_r), F32)], axis=1)
    sin_f = jnp.concatenate([jnp.zeros((seq_len, pad_l), F32), sin, sin,
                             jnp.zeros((seq_len, pad_r), F32)], axis=1)
    cos_f = jnp.concatenate([jnp.ones((ctx_len, QPAD), F32), cos_f], axis=0)
    sin_f = jnp.concatenate([jnp.zeros((ctx_len, QPAD), F32), sin_f], axis=0)
    return cos_f, sin_f


def _gained_tables(cos_f, sin_f, gain, scale):
    g = jnp.pad(gain, (0, QPAD - gain.shape[0]))
    half = MLA_ROPE // 2
    lo, mid, hi = MLA_NOPE, MLA_NOPE + half, MLA_QK
    g_swap = jnp.concatenate([g[:lo], g[mid:hi], g[lo:mid], g[hi:]])
    sign = jnp.where(jnp.arange(QPAD) < mid, -1.0, 1.0).astype(F32)
    return cos_f * (g * scale), sin_f * (g_swap * sign * scale)


def _gate_bias(i_b, f_b):
    gb = jnp.stack([i_b, f_b], axis=1).reshape(-1)
    return jnp.pad(gb, (0, LANES - gb.shape[0])).reshape(1, LANES)


def kernel(x, c, ctx, c_ctx, mod_w, mod_b, norm1_g, norm2_g, w_in, w_out, ffn_w_in, ffn_w_out, rwkv_mu, rwkv_w0, rwkv_w2, rwkv_a0, rwkv_a2, rwkv_g2, rwkv_k_k, rwkv_k_a, rwkv_r_k, rwkv_ln_g, rwkv_ln_b, mla_q_norm_g, mla_w_uq, mla_kv_norm_g, mla_w_ukv, mla_q_qknorm_g, mla_k_qknorm_g, mlstm_conv_w, mlstm_conv_b, mlstm_i_b, mlstm_f_b, mlstm_norm_g):
    bsz, seq, d = x.shape
    ctx_len = ctx.shape[1]
    depth = mod_w.shape[0]
    tm = min(256, ctx_len)
    assert ctx_len % tm == 0 and seq % tm == 0 and ctx_len % CHUNK == 0 and seq % CHUNK == 0
    nb = ROWS_PER_STEP if bsz % ROWS_PER_STEP == 0 else 1
    nb_seq = SEQ_ROWS_PER_STEP if bsz % SEQ_ROWS_PER_STEP == 0 else nb

    xu = (ctx, x)
    n_ctx_tiles = ctx_len // tm

    rows = -(-(bsz + 1) // SUBLANES) * SUBLANES
    cvec = jnp.zeros((rows, d), F32).at[:bsz].set(c).at[bsz].set(c_ctx)
    mod = _modulation(cvec, mod_w, mod_b).reshape(depth, rows, 6, d)
    mod_lat = mod[:, :bsz]
    mod_ctx = jnp.broadcast_to(mod[:, bsz][:, None], mod_lat.shape)
    modsel = jnp.stack([mod_ctx, mod_lat], axis=2)

    cos_f, sin_f = _rope_tables(seq, ctx_len)

    for i in range(depth):
        last = i == depth - 1
        first_tile = n_ctx_tiles if last else 0
        w_r = _arrange_w_in(w_in[i])
        proj = _in_proj(xu, modsel[i, :, :, 0:2], norm1_g[i], w_r, rwkv_mu[i],
                        mlstm_conv_w[i], mlstm_conv_b[i], tm=tm, ctx_len=ctx_len)
        za, zm, zg, zb = proj[:4]
        if isinstance(xu, tuple):
            xu = proj[4]
        g_loc, y_loc, a_tr, b_st, bonus, gate = _rwkv_local(
            za, rwkv_k_k[i], rwkv_k_a[i], rwkv_r_k[i], rwkv_w0[i], rwkv_w2[i], rwkv_a0[i],
            rwkv_a2[i], rwkv_g2[i], nb=nb)
        yf, yr = _rwkv_seq(g_loc, y_loc, a_tr, b_st, ctx_len=ctx_len, nb=nb_seq)
        wq_r, wk_r, wv_r = _arrange_mla_weights(mla_w_uq[i], mla_w_ukv[i])
        tables = (_gained_tables(cos_f, sin_f, mla_q_qknorm_g[i], MLA_QK ** -0.5 * LOG2_E)
                  + _gained_tables(cos_f, sin_f, mla_k_qknorm_g[i], 1.0))
        q, k, v = _mla_proj(zb, tables, mla_q_norm_g[i], mla_kv_norm_g[i],
                            wq_r, wk_r, wv_r, tm=tm)
        attn = _mla_attn(q, k, v, tq=tm, ctx_len=ctx_len, first_tile=first_tile)
        gb = _gate_bias(mlstm_i_b[i], mlstm_f_b[i])
        hf, hr = _mlstm_seq(zm, *_mlstm_local(zm, zg, gb, nb=nb), ctx_len=ctx_len, nb=nb_seq)
        wo = w_out[i].astype(BF16)
        xu = _mix_ffn(xu, yf, yr, bonus, gate, attn, hf, hr, zm, modsel[i, :, :, 2:6],
                      rwkv_ln_g[i], rwkv_ln_b[i], mlstm_norm_g[i],
                      wo[:RWKV_DIM], wo[RWKV_DIM:RWKV_DIM + MLA_DIM], wo[RWKV_DIM + MLA_DIM:],
                      norm2_g[i], ffn_w_in[i].astype(BF16), ffn_w_out[i].astype(BF16),
                      tm=tm, ctx_len=ctx_len, first_tile=first_tile)
    return xu
```
